```python
import math
import jax
import jax.numpy as jnp
from jax import lax
import numpy as np

D_MODEL = 1024
BATCH = 8
SEQ = 2048
DEPTH = 4
DEC_BATCH = 32
DEC_SEQ = 8
PAST_LEN = 8192
PAGE_SIZE = 128

HEAD_DIM = 64
W_A = D_MODEL // 2
W_B = D_MODEL // 4
W_C = D_MODEL // 4
W_MIX = W_A + W_B + W_C
H_A = W_A // HEAD_DIM
KV_A = 2
REP_A = H_A // KV_A
Q_COLS = H_A * HEAD_DIM
KV_COLS = KV_A * HEAD_DIM
L_CMP = 32
SEL_BLOCK = 64
TOP_K = 16
N_LOCAL_FORCED = 2
FORCE_SCORE = 1.0e4
WINDOW = 512
WIN_QBLK = 128
SEL_QBLK = 64
CONV_B = 31
GN_GROUPS = 4
CONV_C = 4
LRU_BLOCKS = 4
LRU_BW = W_C // LRU_BLOCKS
LRU_C = 8.0
NUM_BUCKETS = 32
MAX_DISTANCE = 128
SM_SCALE = HEAD_DIM ** -0.5
EPS = 1e-6
NEG = -1e30
_SIZES = (Q_COLS, KV_COLS, KV_COLS, KV_COLS, KV_COLS, KV_COLS, KV_COLS, 3 * H_A, W_A, 2 * W_B, W_B, W_C, W_C)
D_IN = Q_COLS + 6 * KV_COLS + 3 * H_A + W_A + 3 * W_B + 2 * W_C

kernel_name = 'hymba_nsa_conformer_rglru_step'


def _rms_norm(x, g):
    xf = x.astype(jnp.float32)
    y = xf * lax.rsqrt(jnp.mean(xf * xf, axis=-1, keepdims=True) + EPS)
    return (y * g.astype(jnp.float32)).astype(x.dtype)


def _split_points():
    pts, acc = [], 0
    for s in _SIZES[:-1]:
        acc += s
        pts.append(acc)
    return pts


def _t5_bucket(dist):
    n = jnp.maximum(dist, 0)
    max_exact = NUM_BUCKETS // 2
    nf = jnp.maximum(n, 1).astype(jnp.float32)
    large = max_exact + (jnp.log(nf / max_exact) / math.log(MAX_DISTANCE / max_exact) * (NUM_BUCKETS - max_exact)).astype(jnp.int32)
    return jnp.where(n < max_exact, n, jnp.minimum(large, NUM_BUCKETS - 1))


def _last_rows(a, n):
    length = a.shape[1]
    if length >= n:
        return a[:, length - n:]
    pad = [(0, 0)] * a.ndim
    pad[1] = (n - length, 0)
    return jnp.pad(a, pad)


def _causal_dwconv(x_ext, w, b):
    y = lax.conv_general_dilated(x_ext, w[:, None, :].astype(x_ext.dtype), (1,), 'VALID',
                                 dimension_numbers=('NWC', 'WIO', 'NWC'),
                                 feature_group_count=x_ext.shape[-1])
    return y + b


def _masked_softmax(logits, mask):
    p = jax.nn.softmax(jnp.where(mask, logits, NEG), axis=-1)
    return jnp.where(mask, p, 0.0)


def _q_heads(a):
    return a.reshape(a.shape[:2] + (KV_A, REP_A, HEAD_DIM))


def _kv_heads(a):
    return a.reshape(a.shape[:2] + (KV_A, HEAD_DIM))


def _compress_blocks(k, w):
    bsz, length, g, d = k.shape
    n = length // L_CMP
    kb = k[:, :n * L_CMP].reshape(bsz, n, L_CMP, g, d)
    return jnp.einsum('bnjgd,jd->bngd', kb, w)


def _nsa_compressed(q, q_pos, kc, vc, rel_bias):
    t, n = q.shape[1], kc.shape[1]
    blk_end = (jnp.arange(n, dtype=jnp.int32) + 1) * L_CMP - 1
    dist = q_pos[:, None] - blk_end[None, :]
    bias = rel_bias[_t5_bucket(dist)].reshape(t, n, KV_A, REP_A).transpose(0, 2, 3, 1)
    logits = jnp.einsum('btgrd,bngd->btgrn', q, kc).astype(jnp.float32) * SM_SCALE + bias.astype(jnp.float32)
    p = _masked_softmax(logits, (dist >= 0)[:, None, None, :])
    o = jnp.einsum('btgrn,bngd->btgrd', p.astype(vc.dtype), vc)
    return o, p


def _select_blocks(p_cmp, q_pos, n_sel):
    bsz, t = p_cmp.shape[:2]
    per = SEL_BLOCK // L_CMP
    imp = p_cmp.sum(axis=3)
    imp = jnp.pad(imp, ((0, 0), (0, 0), (0, 0), (0, n_sel * per - imp.shape[-1])))
    imp = imp.reshape(bsz, t, KV_A, n_sel, per).sum(-1)
    blk = jnp.arange(n_sel, dtype=jnp.int32)[None, :]
    cur = (q_pos // SEL_BLOCK)[:, None]
    forced = (blk == 0) | ((blk <= cur) & (blk > cur - N_LOCAL_FORCED))
    score = jnp.where(forced[None, :, None, :], FORCE_SCORE, imp)
    score = jnp.where((blk <= cur)[None, :, None, :], score, NEG)
    _, top_i = lax.top_k(score, min(TOP_K, n_sel))
    valid = top_i <= (q_pos // SEL_BLOCK)[None, :, None, None]
    return top_i, valid


def _sel_attend(q, q_pos, ks, vs, key_pos, valid, rel_bias):
    bsz, t, g, r, d = q.shape
    k_top, s = ks.shape[3], ks.shape[4]
    dist = q_pos[None, :, None, None, None] - key_pos
    g_idx = jnp.arange(KV_A)[None, None, :, None, None]
    bias = jnp.moveaxis(rel_bias.reshape(NUM_BUCKETS, KV_A, REP_A)[_t5_bucket(dist), g_idx], -1, 3)
    logits = jnp.einsum('btgrd,btgksd->btgrks', q, ks).astype(jnp.float32) * SM_SCALE + bias.astype(jnp.float32)
    mask = ((dist >= 0) & valid[..., None])[:, :, :, None]
    p = _masked_softmax(logits.reshape(bsz, t, g, r, k_top * s), mask.reshape(bsz, t, g, 1, k_top * s))
    return jnp.einsum('btgrks,btgksd->btgrd', p.reshape(bsz, t, g, r, k_top, s).astype(vs.dtype), vs)


def _window_banded(q, k, v, rel_bias):
    bsz, t = q.shape[:2]
    nb = t // WIN_QBLK
    span = WINDOW + WIN_QBLK
    pad = ((0, 0), (WINDOW, 0), (0, 0), (0, 0))
    idx = jnp.arange(nb)[:, None] * WIN_QBLK + jnp.arange(span)[None, :]
    kb = jnp.pad(k, pad)[:, idx]
    vb = jnp.pad(v, pad)[:, idx]
    qb = q.reshape(bsz, nb, WIN_QBLK, KV_A, REP_A, HEAD_DIM)
    dist = WINDOW + jnp.arange(WIN_QBLK)[:, None] - jnp.arange(span)[None, :]
    bias = rel_bias[_t5_bucket(dist)].reshape(WIN_QBLK, span, KV_A, REP_A).transpose(2, 3, 0, 1)
    logits = jnp.einsum('bnqgrd,bnsgd->bngrqs', qb, kb).astype(jnp.float32) * SM_SCALE + bias.astype(jnp.float32)
    mask = ((dist >= 0) & (dist < WINDOW))[None] & (idx >= WINDOW)[:, None, :]
    p = _masked_softmax(logits, mask[None, :, None, None])
    o = jnp.einsum('bngrqs,bnsgd->bnqgrd', p.astype(v.dtype), vb)
    return o.reshape(bsz, t, KV_A, REP_A, HEAD_DIM)


def _window_dense(q, q_pos, k, v, k_pos, rel_bias):
    t, length = q.shape[1], k.shape[1]
    dist = q_pos[:, None] - k_pos[None, :]
    bias = rel_bias[_t5_bucket(dist)].reshape(t, length, KV_A, REP_A).transpose(0, 2, 3, 1)
    logits = jnp.einsum('btgrd,blgd->btgrl', q, k).astype(jnp.float32) * SM_SCALE + bias.astype(jnp.float32)
    p = _masked_softmax(logits, ((dist >= 0) & (dist < WINDOW))[:, None, None, :])
    return jnp.einsum('btgrl,blgd->btgrd', p.astype(v.dtype), v)


def _nsa_prompt(q, k_c, v_c, k_s, v_s, k_w, v_w, w_ck, w_cv, rel_bias):
    bsz, t = q.shape[:2]
    q_pos = jnp.arange(t, dtype=jnp.int32)
    o_c, p_c = _nsa_compressed(q, q_pos, _compress_blocks(k_c, w_ck), _compress_blocks(v_c, w_cv), rel_bias)
    top_i, valid = _select_blocks(p_c, q_pos, -(-t // SEL_BLOCK))
    n_chunks = t // SEL_QBLK
    s_ar = jnp.arange(SEL_BLOCK, dtype=jnp.int32)
    b5 = jnp.arange(bsz)[:, None, None, None, None]
    g5 = jnp.arange(KV_A)[None, None, :, None, None]

    def chunk(args):
        q_c, i_c, ok_c, pos_c = args
        key_pos = i_c[..., None] * SEL_BLOCK + s_ar
        return _sel_attend(q_c, pos_c, k_s[b5, key_pos, g5], v_s[b5, key_pos, g5], key_pos, ok_c, rel_bias)

    def to_chunks(a):
        return jnp.moveaxis(a.reshape((bsz, n_chunks, SEL_QBLK) + a.shape[2:]), 1, 0)

    o_s = lax.map(chunk, (to_chunks(q), to_chunks(top_i), to_chunks(valid), q_pos.reshape(n_chunks, SEL_QBLK)))
    o_s = jnp.moveaxis(o_s, 0, 1).reshape(q.shape)
    o_w = _window_banded(q, k_w, v_w, rel_bias)
    return o_c, o_s, o_w


def _nsa_decode(q, k_c, v_c, k_s, v_s, k_w, v_w, pool_ck, pool_cv, pool_sk, pool_sv, win_k, win_v, page_table, w_ck, w_cv, rel_bias):
    bsz, t = q.shape[:2]
    n_pages = page_table.shape[1]
    past = n_pages * PAGE_SIZE
    q_pos = past + jnp.arange(t, dtype=jnp.int32)

    def past_rows(pool):
        return pool[page_table].reshape(bsz, past, KV_A, HEAD_DIM)

    kc = _compress_blocks(jnp.concatenate([past_rows(pool_ck), k_c], axis=1), w_ck)
    vc = _compress_blocks(jnp.concatenate([past_rows(pool_cv), v_c], axis=1), w_cv)
    o_c, p_c = _nsa_compressed(q, q_pos, kc, vc, rel_bias)
    n_sel = -(-(past + t) // SEL_BLOCK)
    n_past_sel = past // SEL_BLOCK
    n_new_sel = n_sel - n_past_sel
    top_i, valid = _select_blocks(p_c, q_pos, n_sel)
    s_ar = jnp.arange(SEL_BLOCK, dtype=jnp.int32)
    b4 = jnp.arange(bsz)[:, None, None, None]
    b5 = b4[..., None]
    g5 = jnp.arange(KV_A)[None, None, :, None, None]
    start = jnp.clip(top_i, 0, n_past_sel - 1) * SEL_BLOCK
    phys = page_table[b4, start // PAGE_SIZE]
    rows = (start % PAGE_SIZE)[..., None] + s_ar
    new_rows = jnp.clip(top_i - n_past_sel, 0, n_new_sel - 1)[..., None] * SEL_BLOCK + s_ar
    pad = ((0, 0), (0, n_new_sel * SEL_BLOCK - t), (0, 0), (0, 0))
    is_past = (top_i < n_past_sel)[..., None, None]
    ks = jnp.where(is_past, pool_sk[phys[..., None], rows, g5], jnp.pad(k_s, pad)[b5, new_rows, g5])
    vs = jnp.where(is_past, pool_sv[phys[..., None], rows, g5], jnp.pad(v_s, pad)[b5, new_rows, g5])
    key_pos = top_i[..., None] * SEL_BLOCK + s_ar
    o_s = _sel_attend(q, q_pos, ks, vs, key_pos, valid, rel_bias)
    w_buf = win_k.shape[1]
    kw = jnp.concatenate([win_k, k_w], axis=1)
    vw = jnp.concatenate([win_v, v_w], axis=1)
    k_pos = past - w_buf + jnp.arange(w_buf + t, dtype=jnp.int32)
    o_w = _window_dense(q, q_pos, kw, vw, k_pos, rel_bias)
    return o_c, o_s, o_w, _last_rows(kw, w_buf), _last_rows(vw, w_buf)


def _nsa_merge(o_c, o_s, o_w, gate, z_a):
    bsz, t = gate.shape[:2]
    g = jax.nn.sigmoid(gate).reshape(bsz, t, 3, KV_A, REP_A, 1)
    o = g[:, :, 0] * o_c + g[:, :, 1] * o_s + g[:, :, 2] * o_w
    return o.reshape(bsz, t, W_A) * jax.nn.silu(z_a)


def _conformer(glu, z, buf, conv_w, conv_b, gn_gain, gn_bias, w_pw):
    val, gate = jnp.split(glu, 2, axis=-1)
    u = val * jax.nn.sigmoid(gate)
    u_ext = jnp.concatenate([buf, u], axis=1)
    c = _causal_dwconv(u_ext, conv_w, conv_b)
    bsz, t = c.shape[:2]
    cg = c.astype(jnp.float32).reshape(bsz, t, GN_GROUPS, W_B // GN_GROUPS)
    mu = jnp.mean(cg, axis=-1, keepdims=True)
    var = jnp.mean(jnp.square(cg - mu), axis=-1, keepdims=True)
    cn = ((cg - mu) * lax.rsqrt(var + EPS)).reshape(bsz, t, W_B) * gn_gain.astype(jnp.float32) + gn_bias.astype(jnp.float32)
    out = jax.nn.silu(cn).astype(c.dtype) @ w_pw
    return out * jax.nn.silu(z), _last_rows(u_ext, CONV_B - 1)


def _rglru(x_c, z, buf, h0, conv_w, conv_b, w_a, b_a, w_x, b_x, lam):
    x_ext = jnp.concatenate([buf, x_c], axis=1)
    u = _causal_dwconv(x_ext, conv_w, conv_b)
    bsz, t = u.shape[:2]
    ub = u.reshape(bsz, t, LRU_BLOCKS, LRU_BW)
    r = jax.nn.sigmoid(jnp.einsum('btnc,ncd->btnd', ub, w_a).reshape(bsz, t, W_C) + b_a).astype(jnp.float32)
    i = jax.nn.sigmoid(jnp.einsum('btnc,ncd->btnd', ub, w_x).reshape(bsz, t, W_C) + b_x)
    log_a = -LRU_C * r * jax.nn.softplus(-lam.astype(jnp.float32))
    a = jnp.exp(log_a)
    b = jnp.sqrt(-jnp.expm1(2.0 * log_a)) * (i * u).astype(jnp.float32)
    b = b.at[:, 0].add(a[:, 0] * h0.astype(jnp.float32))

    def combine(left, right):
        return left[0] * right[0], right[0] * left[1] + right[1]

    _, h = lax.associative_scan(combine, (a, b), axis=1)
    y = h.astype(u.dtype) * jax.nn.silu(z)
    return y, _last_rows(x_ext, CONV_C - 1), h[:, -1].astype(x_c.dtype)


def _project(h, g_pre, w_in):
    return jnp.split(_rms_norm(h, g_pre) @ w_in, _split_points(), axis=-1)


def _output(h, a_out, b_out, c_out, w_out, g_post):
    y = jnp.concatenate([a_out, b_out, c_out], axis=-1) @ w_out
    return h + _rms_norm(y, g_post)


def setup_inputs(seed: int = 0) -> dict:
    key = jax.random.key(seed)
    ks = jax.random.split(key, 32)
    n_pages = PAST_LEN // PAGE_SIZE
    n_phys = (5 * DEC_BATCH * n_pages) // 4
    w_buf = min(WINDOW, PAST_LEN)

    def nrm(k, shape, scale=1.0):
        return scale * jax.random.normal(k, shape, jnp.float32)

    pool = (DEPTH, n_phys, PAGE_SIZE, KV_A, HEAD_DIM)
    win = (DEPTH, DEC_BATCH, w_buf, KV_A, HEAD_DIM)
    page_table = jax.random.permutation(ks[11], n_phys)[:DEC_BATCH * n_pages].reshape(DEC_BATCH, n_pages).astype(jnp.int32)
    a0 = jax.random.uniform(ks[30], (DEPTH, W_C), jnp.float32, 0.9, 0.999)
    return {
        'x_prompt': nrm(ks[0], (BATCH, SEQ, D_MODEL)),
        'x_sample': nrm(ks[1], (DEC_BATCH, DEC_SEQ, D_MODEL)),
        'cache_cmp_k': nrm(ks[2], pool),
        'cache_cmp_v': nrm(ks[3], pool),
        'cache_sel_k': nrm(ks[4], pool),
        'cache_sel_v': nrm(ks[5], pool),
        'cache_win_k': nrm(ks[6], win),
        'cache_win_v': nrm(ks[7], win),
        'state_conv_b': nrm(ks[8], (DEPTH, DEC_BATCH, CONV_B - 1, W_B), 0.5),
        'state_conv_c': nrm(ks[9], (DEPTH, DEC_BATCH, CONV_C - 1, W_C)),
        'state_rglru': nrm(ks[10], (DEPTH, DEC_BATCH, W_C), 0.5),
        'page_table': page_table,
        'rel_bias': nrm(ks[12], (NUM_BUCKETS, H_A), 0.5),
        'g_pre': 1.0 + nrm(ks[13], (DEPTH, D_MODEL), 0.05),
        'g_post': 1.0 + nrm(ks[14], (DEPTH, D_MODEL), 0.05),
        'w_in': nrm(ks[15], (DEPTH, D_MODEL, D_IN), D_MODEL ** -0.5),
        'w_out': nrm(ks[16], (DEPTH, W_MIX, D_MODEL), W_MIX ** -0.5),
        'w_cmp_k': (1.0 + nrm(ks[17], (DEPTH, L_CMP, HEAD_DIM), 0.1)) * (L_CMP ** -0.5),
        'w_cmp_v': (1.0 + nrm(ks[18], (DEPTH, L_CMP, HEAD_DIM), 0.1)) * (L_CMP ** -0.5),
        'conv_b_w': nrm(ks[19], (DEPTH, CONV_B, W_B), CONV_B ** -0.5),
        'conv_b_b': nrm(ks[20], (DEPTH, W_B), 0.02),
        'gn_gain': 1.0 + nrm(ks[21], (DEPTH, W_B), 0.05),
        'gn_bias': nrm(ks[22], (DEPTH, W_B), 0.02),
        'w_pw_b': nrm(ks[23], (DEPTH, W_B, W_B), W_B ** -0.5),
        'conv_c_w': nrm(ks[24], (DEPTH, CONV_C, W_C), CONV_C ** -0.5),
        'conv_c_b': nrm(ks[25], (DEPTH, W_C), 0.02),
        'w_lru_a': nrm(ks[26], (DEPTH, LRU_BLOCKS, LRU_BW, LRU_BW), LRU_BW ** -0.5),
        'b_lru_a': nrm(ks[27], (DEPTH, W_C), 0.02),
        'w_lru_x': nrm(ks[28], (DEPTH, LRU_BLOCKS, LRU_BW, LRU_BW), LRU_BW ** -0.5),
        'b_lru_x': nrm(ks[29], (DEPTH, W_C), 0.02),
        'lru_lambda': jnp.log(a0) - jnp.log1p(-a0),
    }


def reference(x_prompt, x_sample, cache_cmp_k, cache_cmp_v, cache_sel_k, cache_sel_v, cache_win_k, cache_win_v,
              state_conv_b, state_conv_c, state_rglru, page_table, rel_bias, g_pre, g_post, w_in, w_out,
              w_cmp_k, w_cmp_v, conv_b_w, conv_b_b, gn_gain, gn_bias, w_pw_b, conv_c_w, conv_c_b,
              w_lru_a, b_lru_a, w_lru_x, b_lru_x, lru_lambda):
    w_buf = cache_win_k.shape[2]
    hp, hs = x_prompt, x_sample
    prompt_states, sample_states = [], []
    for l in range(DEPTH):
        bsz = hp.shape[0]
        q, k_c, v_c, k_s, v_s, k_w, v_w, gate, z_a, glu, z_b, x_c, z_c = _project(hp, g_pre[l], w_in[l])
        o_c, o_s, o_w = _nsa_prompt(_q_heads(q), _kv_heads(k_c), _kv_heads(v_c), _kv_heads(k_s), _kv_heads(v_s),
                                    _kv_heads(k_w), _kv_heads(v_w), w_cmp_k[l], w_cmp_v[l], rel_bias)
        a_out = _nsa_merge(o_c, o_s, o_w, gate, z_a)
        b_out, cb = _conformer(glu, z_b, jnp.zeros((bsz, CONV_B - 1, W_B), hp.dtype), conv_b_w[l], conv_b_b[l],
                               gn_gain[l], gn_bias[l], w_pw_b[l])
        c_out, cc, hc = _rglru(x_c, z_c, jnp.zeros((bsz, CONV_C - 1, W_C), hp.dtype), jnp.zeros((bsz, W_C), hp.dtype),
                               conv_c_w[l], conv_c_b[l], w_lru_a[l], b_lru_a[l], w_lru_x[l], b_lru_x[l], lru_lambda[l])
        hp = _output(hp, a_out, b_out, c_out, w_out[l], g_post[l])
        prompt_states.append((_kv_heads(k_c), _kv_heads(v_c), _kv_heads(k_s), _kv_heads(v_s),
                              _last_rows(_kv_heads(k_w), w_buf), _last_rows(_kv_heads(v_w), w_buf), cb, cc, hc))
        q, k_c, v_c, k_s, v_s, k_w, v_w, gate, z_a, glu, z_b, x_c, z_c = _project(hs, g_pre[l], w_in[l])
        o_c, o_s, o_w, wk, wv = _nsa_decode(_q_heads(q), _kv_heads(k_c), _kv_heads(v_c), _kv_heads(k_s), _kv_heads(v_s),
                                            _kv_heads(k_w), _kv_heads(v_w), cache_cmp_k[l], cache_cmp_v[l],
                                            cache_sel_k[l], cache_sel_v[l], cache_win_k[l], cache_win_v[l],
                                            page_table, w_cmp_k[l], w_cmp_v[l], rel_bias)
        a_out = _nsa_merge(o_c, o_s, o_w, gate, z_a)
        b_out, cb = _conformer(glu, z_b, state_conv_b[l], conv_b_w[l], conv_b_b[l], gn_gain[l], gn_bias[l], w_pw_b[l])
        c_out, cc, hc = _rglru(x_c, z_c, state_conv_c[l], state_rglru[l], conv_c_w[l], conv_c_b[l],
                               w_lru_a[l], b_lru_a[l], w_lru_x[l], b_lru_x[l], lru_lambda[l])
        hs = _output(hs, a_out, b_out, c_out, w_out[l], g_post[l])
        sample_states.append((_kv_heads(k_c), _kv_heads(v_c), _kv_heads(k_s), _kv_heads(v_s), wk, wv, cb, cc, hc))
    ck_p, cv_p, sk_p, sv_p, wk_p, wv_p, cb_p, cc_p, h_p = [jnp.stack(a) for a in zip(*prompt_states)]
    ck_s, cv_s, sk_s, sv_s, wk_s, wv_s, cb_s, cc_s, h_s = [jnp.stack(a) for a in zip(*sample_states)]
    return (hp, hs, ck_p, ck_s, cv_p, cv_s, sk_p, sk_s, sv_p, sv_s, wk_p, wk_s, wv_p, wv_s, cb_p, cb_s, cc_p, cc_s, h_p, h_s)
```

```python
import functools
import math

import numpy as np
import jax
import jax.numpy as jnp
from jax import lax
from jax.experimental import pallas as pl
from jax.experimental.pallas import tpu as pltpu

F32 = jnp.float32
BF16 = jnp.bfloat16

D_MODEL = 1024
HEAD_DIM = 64
W_A = D_MODEL // 2
W_B = D_MODEL // 4
W_C = D_MODEL // 4
H_A = W_A // HEAD_DIM
KV_A = 2
REP_A = H_A // KV_A
KV_COLS = KV_A * HEAD_DIM
L_CMP = 32
SEL_BLOCK = 64
TOP_K = 16
FORCE_SCORE = 1.0e4
WINDOW = 512
PAGE = 128
CONV_B = 31
CONV_C = 4
GN_GROUP = W_B // 4
LRU_BLOCKS = 4
LRU_C = 8.0
NUM_BUCKETS = 32
MAX_DISTANCE = 128
SM_SCALE = HEAD_DIM ** -0.5
EPS = 1e-6
NEG = -1e30

TQ = 128
D_IN = 3096
D_IN_PAD = 3200
COL_Q, COL_ZA, COL_GLU, COL_KV, COL_ZB, COL_XC, COL_ZC, COL_GATE = 0, 512, 1024, 1536, 2304, 2560, 2816, 3072
VMEM_LIMIT = 56 * 1024 * 1024


def _cparams(sem):
    return pltpu.CompilerParams(dimension_semantics=sem, vmem_limit_bytes=VMEM_LIMIT)


def _bucket_np(dist):
    n = np.maximum(dist, 0)
    max_exact = NUM_BUCKETS // 2
    nf = np.maximum(n, 1).astype(np.float32)
    large = max_exact + (np.log(nf / np.float32(max_exact)) / np.float32(math.log(MAX_DISTANCE / max_exact))
                         * np.float32(NUM_BUCKETS - max_exact)).astype(np.int32)
    return np.where(n < max_exact, n, np.minimum(large, NUM_BUCKETS - 1)).astype(np.int32)


def _expand_np(n_blocks, rows):
    e = np.zeros((rows, n_blocks * SEL_BLOCK), np.float32)
    for m in range(n_blocks):
        e[m, m * SEL_BLOCK:(m + 1) * SEL_BLOCK] = 1.0
    return e


def _gate_expand_np():
    e = np.zeros((128, 3 * W_A), np.float32)
    for br in range(3):
        for h in range(H_A):
            e[br * H_A + h, br * W_A + h * HEAD_DIM: br * W_A + (h + 1) * HEAD_DIM] = 1.0
    return e


def _group_ones_np():
    g = np.zeros((W_B, W_B), np.float32)
    for k in range(W_B // GN_GROUP):
        g[k * GN_GROUP:(k + 1) * GN_GROUP, k * GN_GROUP:(k + 1) * GN_GROUP] = 1.0
    return g


def _split3(a):
    hi = a.astype(BF16)
    r1 = a - hi.astype(F32)
    mid = r1.astype(BF16)
    lo = (r1 - mid.astype(F32)).astype(BF16)
    return hi, mid, lo


def _dot(a, b):
    return jnp.dot(a, b, preferred_element_type=F32)


def _dot_nt(a, b):
    return lax.dot_general(a, b, (((1,), (1,)), ((), ())), preferred_element_type=F32)


def _dot_exact_rhs(a, b_bf16):
    hi, mid, lo = _split3(a)
    return _dot(hi, b_bf16) + _dot(mid, b_bf16) + _dot(lo, b_bf16)


def _dot_nt_f32(a, b):
    ah = a.astype(BF16)
    al = (a - ah.astype(F32)).astype(BF16)
    bh = b.astype(BF16)
    bl = (b - bh.astype(F32)).astype(BF16)
    return _dot_nt(ah, bh) + _dot_nt(al, bh) + _dot_nt(ah, bl)


def _sigmoid(x):
    return 1.0 / (1.0 + jnp.exp(-x))


def _silu(x):
    return x * _sigmoid(x)


def _expm1(x):
    u = jnp.exp(x)
    safe = jnp.where((u == 1.0) | (u == 0.0), 0.5, u)
    return jnp.where(u == 1.0, x, jnp.where(u == 0.0, -1.0, (safe - 1.0) * x / jnp.log(safe)))


def _attend(q, k, v, add, m, l, acc):
    s = _dot_nt(q, k) + add
    m_new = jnp.maximum(m, jnp.max(s, axis=-1, keepdims=True))
    alpha = jnp.exp(m - m_new)
    p = jnp.exp(s - m_new)
    l_new = alpha * l + jnp.sum(p, axis=-1, keepdims=True)
    acc_new = alpha * acc + _dot(p.astype(BF16), v)
    return m_new, l_new, acc_new


def _stack_heads(q, g):
    parts = [q[:, (g * REP_A + r) * HEAD_DIM:(g * REP_A + r + 1) * HEAD_DIM] for r in range(REP_A)]
    return (jnp.concatenate(parts, axis=0) * SM_SCALE).astype(BF16)


def _unstack_heads(o, t):
    return [o[r * t:(r + 1) * t, :] for r in range(REP_A)]


def _merge(gate, za, o_c, o_s, o_w, e2):
    sg = _sigmoid(gate)
    ge = _dot_exact_rhs(sg, e2)
    o = ge[:, 0:W_A] * o_c + ge[:, W_A:2 * W_A] * o_s + ge[:, 2 * W_A:3 * W_A] * o_w
    return o * _silu(za)


def _bias_kernel(rb_ref, bk_ref, o_ref):
    h = pl.program_id(0)
    bk = bk_ref[...]
    acc = jnp.zeros(bk.shape, F32)
    for b in range(NUM_BUCKETS):
        acc = jnp.where(bk == b, rb_ref[b, h], acc)
    o_ref[...] = acc


def _bias_lookup(rel_bias, bucket):
    r, c = bucket.shape
    return pl.pallas_call(
        _bias_kernel,
        grid=(H_A,),
        in_specs=[pl.BlockSpec(memory_space=pltpu.SMEM),
                  pl.BlockSpec((r, c), lambda h: (0, 0))],
        out_specs=pl.BlockSpec((None, r, c), lambda h: (h, 0, 0)),
        out_shape=jax.ShapeDtypeStruct((H_A, r, c), F32),
        compiler_params=_cparams(("arbitrary",)),
        name="bias_lookup",
    )(rel_bias, jnp.asarray(bucket))


def _proj_kernel(x_ref, g_ref, w_ref, o_ref):
    x = x_ref[...]
    ms = jnp.mean(x * x, axis=-1, keepdims=True)
    u = (x * lax.rsqrt(ms + EPS) * g_ref[...]).astype(BF16)
    n = o_ref.shape[1]
    step = 640
    for c in range(0, n, step):
        o_ref[:, c:c + step] = _dot(u, w_ref[:, c:c + step])


def _project(h, g, w, tm):
    m = h.shape[0]
    return pl.pallas_call(
        _proj_kernel,
        grid=(m // tm,),
        in_specs=[pl.BlockSpec((tm, D_MODEL), lambda i: (i, 0)),
                  pl.BlockSpec((1, D_MODEL), lambda i: (0, 0)),
                  pl.BlockSpec((D_MODEL, D_IN_PAD), lambda i: (0, 0))],
        out_specs=pl.BlockSpec((tm, D_IN_PAD), lambda i: (i, 0)),
        out_shape=jax.ShapeDtypeStruct((m, D_IN_PAD), F32),
        compiler_params=_cparams(("arbitrary",)),
        name="project",
    )(h, g, w)


def _out_kernel(h_ref, a_ref, b_ref, c_ref, w_ref, g_ref, o_ref):
    y = (_dot(a_ref[...].astype(BF16), w_ref[0:W_A, :])
         + _dot(b_ref[...].astype(BF16), w_ref[W_A:W_A + W_B, :])
         + _dot(c_ref[...].astype(BF16), w_ref[W_A + W_B:, :]))
    ms = jnp.mean(y * y, axis=-1, keepdims=True)
    o_ref[...] = h_ref[...] + y * lax.rsqrt(ms + EPS) * g_ref[...]


def _output(h, a, b, c, w, g, tm):
    m = h.shape[0]
    return pl.pallas_call(
        _out_kernel,
        grid=(m // tm,),
        in_specs=[pl.BlockSpec((tm, D_MODEL), lambda i: (i, 0)),
                  pl.BlockSpec((tm, W_A), lambda i: (i, 0)),
                  pl.BlockSpec((tm, W_B), lambda i: (i, 0)),
                  pl.BlockSpec((tm, W_C), lambda i: (i, 0)),
                  pl.BlockSpec((D_MODEL, D_MODEL), lambda i: (0, 0)),
                  pl.BlockSpec((1, D_MODEL), lambda i: (0, 0))],
        out_specs=pl.BlockSpec((tm, D_MODEL), lambda i: (i, 0)),
        out_shape=jax.ShapeDtypeStruct((m, D_MODEL), F32),
        compiler_params=_cparams(("arbitrary",)),
        name="output",
    )(h, a, b, c, w, g)


def _attn_prompt_kernel(q_ref, za_ref, kv_ref, gate_ref, wk_ref, wv_ref, bt_ref, bc_ref, e_ref, e2_ref,
                        o_ref, kc_s, vc_s, ks_s, vs_s, kw_s, vw_s, madd_s, *, ns, nt):
    i = pl.program_id(1)
    t_len = ns * SEL_BLOCK

    @pl.when(i == 0)
    def _():
        for src, w_ref, dst in ((0, wk_ref, kc_s), (1, wv_ref, vc_s)):
            x3 = kv_ref[:, src * KV_COLS:(src + 1) * KV_COLS].reshape(ns, SEL_BLOCK, KV_COLS)
            w = w_ref[...]
            dst[0:ns, :] = jnp.sum(x3[:, :L_CMP, :] * w[None], axis=1)
            dst[ns:2 * ns, :] = jnp.sum(x3[:, L_CMP:, :] * w[None], axis=1)
        for src, dst in ((2, ks_s), (3, vs_s), (4, kw_s), (5, vw_s)):
            for g in range(KV_A):
                lo = src * KV_COLS + g * HEAD_DIM
                dst[g] = kv_ref[:, lo:lo + HEAD_DIM].astype(BF16)

    q = q_ref[...]
    q0 = i * TQ

    row_c = lax.broadcasted_iota(jnp.int32, (TQ, 2 * ns), 0)
    col_c = lax.broadcasted_iota(jnp.int32, (TQ, 2 * ns), 1)
    blk_c = jnp.where(col_c < ns, 2 * col_c, 2 * (col_c - ns) + 1)
    mask_c = (q0 + row_c) >= blk_c * L_CMP + (L_CMP - 1)

    row_s = lax.broadcasted_iota(jnp.int32, (TQ, 128), 0)
    blk_s = lax.broadcasted_iota(jnp.int32, (TQ, 128), 1)
    cur_s = (q0 + row_s) // SEL_BLOCK
    forced = (blk_s == 0) | ((blk_s <= cur_s) & (blk_s > cur_s - 2))
    nsp = -(-ns // 8) * 8
    blk_t = lax.broadcasted_iota(jnp.int32, (nsp, TQ), 0)
    cur_t = (q0 + lax.broadcasted_iota(jnp.int32, (nsp, TQ), 1)) // SEL_BLOCK
    kpos = lax.broadcasted_iota(jnp.int32, (TQ, t_len), 1)
    qpos = q0 + lax.broadcasted_iota(jnp.int32, (TQ, t_len), 0)

    oc_parts = []
    for g in range(KV_A):
        kcg = kc_s[:, g * HEAD_DIM:(g + 1) * HEAD_DIM]
        vcg = vc_s[:, g * HEAD_DIM:(g + 1) * HEAD_DIM].astype(BF16)
        imp = jnp.zeros((TQ, 2 * ns), F32)
        for r in range(REP_A):
            h = g * REP_A + r
            qh = q[:, h * HEAD_DIM:(h + 1) * HEAD_DIM]
            s = _dot_nt_f32(qh, kcg) * SM_SCALE + bc_ref[h]
            s = jnp.where(mask_c, s, NEG)
            e = jnp.exp(s - jnp.max(s, axis=-1, keepdims=True))
            p = e / jnp.sum(e, axis=-1, keepdims=True)
            p = jnp.where(mask_c, p, 0.0)
            oc_parts.append(_dot(p.astype(BF16), vcg))
            imp = imp + p
        simp = imp[:, 0:ns] + imp[:, ns:2 * ns]
        simp = jnp.concatenate([simp, jnp.zeros((TQ, 128 - ns), F32)], axis=1)
        score = jnp.where(forced, FORCE_SCORE, simp)
        score = jnp.where(blk_s <= cur_s, score, NEG)
        st = score.T[0:nsp, :]
        rank = jnp.zeros((nsp, TQ), F32)
        for m in range(ns):
            rm = st[m:m + 1, :]
            beats = (rm > st) | ((rm == st) & (blk_t > m))
            rank = rank + jnp.where(beats, 1.0, 0.0)
        sel_t = jnp.where((rank < TOP_K) & (blk_t <= cur_t), 1.0, 0.0)
        sel = jnp.concatenate([sel_t, jnp.zeros((128 - nsp, TQ), F32)], axis=0).T
        madd = (_dot(sel.astype(BF16), e_ref[...]) - 1.0) * (-NEG)
        madd = jnp.where(kpos > qpos, NEG, madd)
        for j in range(nt):
            madd_s[g, j] = madd[:, j * TQ:(j + 1) * TQ]
    o_c = jnp.concatenate(oc_parts, axis=1)

    row_t = lax.broadcasted_iota(jnp.int32, (REP_A * TQ, TQ), 0) % TQ
    col_t = lax.broadcasted_iota(jnp.int32, (REP_A * TQ, TQ), 1)
    init = (jnp.full((REP_A * TQ, 1), NEG, F32), jnp.zeros((REP_A * TQ, 1), F32),
            jnp.zeros((REP_A * TQ, HEAD_DIM), F32))

    os_parts, ow_parts = [], []
    for g in range(KV_A):
        qs = _stack_heads(q, g)

        def sel_body(j, carry, g=g, qs=qs):
            r0 = pl.multiple_of(j * TQ, TQ)
            kt = ks_s[g, pl.ds(r0, TQ), :]
            vt = vs_s[g, pl.ds(r0, TQ), :]
            ma = madd_s[g, j]
            add = bt_ref[g, jnp.minimum(i - j, 2)] + jnp.concatenate([ma] * REP_A, axis=0)
            return _attend(qs, kt, vt, add, *carry)

        m, l, acc = lax.fori_loop(0, i + 1, sel_body, init)
        os_parts += _unstack_heads(acc / l, TQ)

        carry = init
        for k in range(WINDOW // TQ + 1):
            j = i - k
            jc = jnp.maximum(j, 0)
            r0 = pl.multiple_of(jc * TQ, TQ)
            kt = kw_s[g, pl.ds(r0, TQ), :]
            vt = vw_s[g, pl.ds(r0, TQ), :]
            add = bt_ref[g, min(k, 2)]
            if k == 0:
                add = jnp.where(col_t <= row_t, add, NEG)
            else:
                ok = j >= 0
                if k == WINDOW // TQ:
                    add = jnp.where((col_t > row_t) & ok, add, NEG)
                else:
                    add = jnp.where(ok, add, NEG)
            carry = _attend(qs, kt, vt, add, *carry)
        m, l, acc = carry
        ow_parts += _unstack_heads(acc / l, TQ)

    o_s = jnp.concatenate(os_parts, axis=1)
    o_w = jnp.concatenate(ow_parts, axis=1)
    o_ref[...] = _merge(gate_ref[...], za_ref[...], o_c, o_s, o_w, e2_ref[...])


def _attn_prompt(proj, wk, wv, bt, bc, e_sel, e2, b, t):
    ns, nt = t // SEL_BLOCK, t // TQ
    kern = functools.partial(_attn_prompt_kernel, ns=ns, nt=nt)
    return pl.pallas_call(
        kern,
        grid=(b, nt),
        in_specs=[pl.BlockSpec((TQ, W_A), lambda bi, i: (bi * nt + i, COL_Q // W_A)),
                  pl.BlockSpec((TQ, W_A), lambda bi, i: (bi * nt + i, COL_ZA // W_A)),
                  pl.BlockSpec((t, 6 * KV_COLS), lambda bi, i: (bi, COL_KV // (6 * KV_COLS))),
                  pl.BlockSpec((TQ, 128), lambda bi, i: (bi * nt + i, COL_GATE // 128)),
                  pl.BlockSpec((L_CMP, KV_COLS), lambda bi, i: (0, 0)),
                  pl.BlockSpec((L_CMP, KV_COLS), lambda bi, i: (0, 0)),
                  pl.BlockSpec((KV_A, 3, REP_A * TQ, TQ), lambda bi, i: (0, 0, 0, 0)),
                  pl.BlockSpec((H_A, TQ, 2 * ns), lambda bi, i: (0, i, 0)),
                  pl.BlockSpec((128, t), lambda bi, i: (0, 0)),
                  pl.BlockSpec((128, 3 * W_A), lambda bi, i: (0, 0))],
        out_specs=pl.BlockSpec((TQ, W_A), lambda bi, i: (bi * nt + i, 0)),
        out_shape=jax.ShapeDtypeStruct((b * t, W_A), F32),
        scratch_shapes=[pltpu.VMEM((2 * ns, KV_COLS), F32), pltpu.VMEM((2 * ns, KV_COLS), F32),
                        pltpu.VMEM((KV_A, t, HEAD_DIM), BF16), pltpu.VMEM((KV_A, t, HEAD_DIM), BF16),
                        pltpu.VMEM((KV_A, t, HEAD_DIM), BF16), pltpu.VMEM((KV_A, t, HEAD_DIM), BF16),
                        pltpu.VMEM((KV_A, nt, TQ, TQ), F32)],
        compiler_params=_cparams(("arbitrary", "arbitrary")),
        name="attn_prompt",
    )(proj, proj, proj, proj, wk, wv, bt, bc, e_sel, e2)


def _dcmp_kernel(pt_ref, *refs, pages):
    k_refs, v_refs = refs[0:pages], refs[pages:2 * pages]
    wk_ref, wv_ref = refs[2 * pages], refs[2 * pages + 1]
    kce_ref, kco_ref, vce_ref, vco_ref = refs[2 * pages + 2:]
    for src, w_ref, e_ref, o_ref in ((k_refs, wk_ref, kce_ref, kco_ref), (v_refs, wv_ref, vce_ref, vco_ref)):
        w = w_ref[...]
        for p in range(pages):
            x3 = src[p][...].reshape(PAGE // SEL_BLOCK, SEL_BLOCK, KV_COLS)
            e_ref[2 * p:2 * p + 2, :] = jnp.sum(x3[:, :L_CMP, :] * w[None], axis=1)
            o_ref[2 * p:2 * p + 2, :] = jnp.sum(x3[:, L_CMP:, :] * w[None], axis=1)


def _decode_compress(page_table, pool_k, pool_v, wk, wv, pages):
    depth, db, n_pages = pool_k.shape[0], page_table.shape[0], page_table.shape[1]
    nps = n_pages * (PAGE // SEL_BLOCK)

    def page_spec(p):
        return pl.BlockSpec((None, None, PAGE, KV_COLS), lambda l, b, c, pt: (l, pt[b, c * pages + p], 0, 0))

    w_spec = pl.BlockSpec((None, L_CMP, KV_COLS), lambda l, b, c, pt: (l, 0, 0))
    o_spec = pl.BlockSpec((None, None, 2 * pages, KV_COLS), lambda l, b, c, pt: (l, b, c, 0))
    o_shape = jax.ShapeDtypeStruct((depth, db, nps, KV_COLS), F32)
    return pl.pallas_call(
        functools.partial(_dcmp_kernel, pages=pages),
        grid_spec=pltpu.PrefetchScalarGridSpec(
            num_scalar_prefetch=1,
            grid=(depth, db, n_pages // pages),
            in_specs=[page_spec(p) for p in range(pages)] * 2 + [w_spec, w_spec],
            out_specs=[o_spec] * 4),
        out_shape=[o_shape] * 4,
        compiler_params=_cparams(("arbitrary", "arbitrary", "arbitrary")),
        name="decode_compress",
    )(page_table, *([pool_k] * pages), *([pool_v] * pages), wk, wv)


def _attn_decode_kernel(pt_ref, *refs, pages, nps, nc, dt):
    del pt_ref
    (q_ref, za_ref, kv_ref, gate_ref, kce_ref, kco_ref, vce_ref, vco_ref) = refs[0:8]
    sk_refs, sv_refs = refs[8:8 + pages], refs[8 + pages:8 + 2 * pages]
    (wink_ref, winv_ref, bce_ref, bco_ref, bsel_ref, bnew_ref, bwin_ref, e_ref, e2_ref,
     o_ref, wk_o, wv_o, m_s, l_s, acc_s, madd_s, oc_s) = refs[8 + 2 * pages:]
    c = pl.program_id(1)
    ck = pages * PAGE
    rows = REP_A * dt
    q = q_ref[...]

    @pl.when(c == 0)
    def _():
        oc_parts = []
        blk = lax.broadcasted_iota(jnp.int32, (dt, nps), 1)
        forced = (blk == 0) | (blk >= nps - 1)
        for g in range(KV_A):
            sl = slice(g * HEAD_DIM, (g + 1) * HEAD_DIM)
            kce, kco = kce_ref[:, sl], kco_ref[:, sl]
            vce, vco = vce_ref[:, sl].astype(BF16), vco_ref[:, sl].astype(BF16)
            imp_e = jnp.zeros((dt, nps), F32)
            imp_o = jnp.zeros((dt, nps), F32)
            for r in range(REP_A):
                h = g * REP_A + r
                qh = q[:, h * HEAD_DIM:(h + 1) * HEAD_DIM]
                s_e = _dot_nt_f32(qh, kce) * SM_SCALE + bce_ref[h]
                s_o = _dot_nt_f32(qh, kco) * SM_SCALE + bco_ref[h]
                mx = jnp.maximum(jnp.max(s_e, axis=-1, keepdims=True), jnp.max(s_o, axis=-1, keepdims=True))
                e_e, e_o = jnp.exp(s_e - mx), jnp.exp(s_o - mx)
                den = jnp.sum(e_e, axis=-1, keepdims=True) + jnp.sum(e_o, axis=-1, keepdims=True)
                p_e, p_o = e_e / den, e_o / den
                oc_parts.append(_dot(p_e.astype(BF16), vce) + _dot(p_o.astype(BF16), vco))
                imp_e, imp_o = imp_e + p_e, imp_o + p_o
            score = jnp.where(forced, FORCE_SCORE, imp_e + imp_o)
            rank = jnp.zeros((dt, nps), F32)
            for m in range(nps):
                sm = score[:, m:m + 1]
                beats = (sm > score) | ((sm == score) & (blk > m))
                rank = rank + jnp.where(beats, 1.0, 0.0)
            sel = jnp.where(rank < TOP_K - 1, 1.0, 0.0)
            madd = (_dot(sel.astype(BF16), e_ref[...]) - 1.0) * (-NEG)
            for j in range(nc):
                madd_s[g, j] = madd[:, j * ck:(j + 1) * ck]
        oc_s[...] = jnp.concatenate(oc_parts, axis=1)
        m_s[...] = jnp.full(m_s.shape, NEG, F32)
        l_s[...] = jnp.zeros(l_s.shape, F32)
        acc_s[...] = jnp.zeros(acc_s.shape, F32)

    for g in range(KV_A):
        sl = slice(g * HEAD_DIM, (g + 1) * HEAD_DIM)
        qs = _stack_heads(q, g)
        kt = jnp.concatenate([sk_refs[p][:, sl] for p in range(pages)], axis=0).astype(BF16)
        vt = jnp.concatenate([sv_refs[p][:, sl] for p in range(pages)], axis=0).astype(BF16)
        add = bsel_ref[g] + jnp.concatenate([madd_s[g, c]] * REP_A, axis=0)
        m, l, acc = _attend(qs, kt, vt, add, m_s[g], l_s[g], acc_s[g])
        m_s[g], l_s[g], acc_s[g] = m, l, acc

    @pl.when(c == nc - 1)
    def _():
        row_t = lax.broadcasted_iota(jnp.int32, (rows, 128), 0) % dt
        col = lax.broadcasted_iota(jnp.int32, (rows, 128), 1)
        new_ok = (col <= row_t) & (col < dt)
        row_w = lax.broadcasted_iota(jnp.int32, (rows, WINDOW), 0) % dt
        col_w = lax.broadcasted_iota(jnp.int32, (rows, WINDOW), 1)
        pad = jnp.zeros((128 - dt, HEAD_DIM), F32)
        init = (jnp.full((rows, 1), NEG, F32), jnp.zeros((rows, 1), F32), jnp.zeros((rows, HEAD_DIM), F32))
        os_parts, ow_parts = [], []
        for g in range(KV_A):
            qs = _stack_heads(q, g)
            new_add = jnp.where(new_ok, bnew_ref[g], NEG)

            def new_tile(src, g=g):
                lo = COL_KV - COL_KV + src * KV_COLS + g * HEAD_DIM
                return jnp.concatenate([kv_ref[:, lo:lo + HEAD_DIM], pad], axis=0).astype(BF16)

            m, l, acc = _attend(qs, new_tile(2), new_tile(3), new_add, m_s[g], l_s[g], acc_s[g])
            os_parts += _unstack_heads(acc / l, dt)
            sl = slice(g * HEAD_DIM, (g + 1) * HEAD_DIM)
            win_add = jnp.where(col_w > row_w, bwin_ref[g], NEG)
            carry = _attend(qs, wink_ref[:, sl].astype(BF16), winv_ref[:, sl].astype(BF16), win_add, *init)
            m, l, acc = _attend(qs, new_tile(4), new_tile(5), new_add, *carry)
            ow_parts += _unstack_heads(acc / l, dt)
        o_s = jnp.concatenate(os_parts, axis=1)
        o_w = jnp.concatenate(ow_parts, axis=1)
        o_ref[...] = _merge(gate_ref[...], za_ref[...], oc_s[...], o_s, o_w, e2_ref[...])
        wk_o[...] = jnp.concatenate([wink_ref[dt:, :], kv_ref[:, 4 * KV_COLS:5 * KV_COLS]], axis=0)
        wv_o[...] = jnp.concatenate([winv_ref[dt:, :], kv_ref[:, 5 * KV_COLS:6 * KV_COLS]], axis=0)


def _attn_decode(layer, page_table, proj, cmp_tabs, pool_sk, pool_sv, win_k, win_v, tabs, e_sel, e2, pages, dt):
    db, n_pages = page_table.shape
    nps = n_pages * (PAGE // SEL_BLOCK)
    nc = n_pages // pages
    ck = pages * PAGE
    rows = REP_A * dt
    bce, bco, bsel, bnew, bwin = tabs

    def row_spec(width, col):
        return pl.BlockSpec((dt, width), lambda b, c, pt: (b, col // width))

    def page_spec(p):
        return pl.BlockSpec((None, None, PAGE, KV_COLS), lambda b, c, pt: (layer, pt[b, c * pages + p], 0, 0))

    cmp_spec = pl.BlockSpec((None, None, nps, KV_COLS), lambda b, c, pt: (layer, b, 0, 0))
    win_spec = pl.BlockSpec((None, None, WINDOW, KV_COLS), lambda b, c, pt: (layer, b, 0, 0))

    def const_spec(shape):
        return pl.BlockSpec(shape, lambda b, c, pt: (0,) * len(shape))

    in_specs = ([row_spec(W_A, COL_Q), row_spec(W_A, COL_ZA), row_spec(6 * KV_COLS, COL_KV), row_spec(128, COL_GATE)]
                + [cmp_spec] * 4 + [page_spec(p) for p in range(pages)] * 2 + [win_spec] * 2
                + [const_spec(bce.shape), const_spec(bco.shape),
                   pl.BlockSpec((KV_A, rows, ck), lambda b, c, pt: (0, 0, c)),
                   const_spec(bnew.shape), const_spec(bwin.shape), const_spec(e_sel.shape), const_spec(e2.shape)])
    out_specs = [pl.BlockSpec((dt, W_A), lambda b, c, pt: (b, 0)),
                 pl.BlockSpec((None, WINDOW, KV_COLS), lambda b, c, pt: (b, 0, 0)),
                 pl.BlockSpec((None, WINDOW, KV_COLS), lambda b, c, pt: (b, 0, 0))]
    out_shape = [jax.ShapeDtypeStruct((db * dt, W_A), F32),
                 jax.ShapeDtypeStruct((db, WINDOW, KV_COLS), F32),
                 jax.ShapeDtypeStruct((db, WINDOW, KV_COLS), F32)]
    scratch = [pltpu.VMEM((KV_A, rows, 1), F32), pltpu.VMEM((KV_A, rows, 1), F32),
               pltpu.VMEM((KV_A, rows, HEAD_DIM), F32), pltpu.VMEM((KV_A, nc, dt, ck), F32),
               pltpu.VMEM((dt, W_A), F32)]
    return pl.pallas_call(
        functools.partial(_attn_decode_kernel, pages=pages, nps=nps, nc=nc, dt=dt),
        grid_spec=pltpu.PrefetchScalarGridSpec(
            num_scalar_prefetch=1, grid=(db, nc), in_specs=in_specs, out_specs=out_specs, scratch_shapes=scratch),
        out_shape=out_shape,
        compiler_params=_cparams(("arbitrary", "arbitrary")),
        name="attn_decode",
    )(page_table, proj, proj, proj, proj, *cmp_tabs, *([pool_sk] * pages), *([pool_sv] * pages),
      win_k, win_v, bce, bco, bsel, bnew, bwin, e_sel, e2)


EXT_B0 = 32
EXT_C0 = 8
CONV_ROWS = 32


def _mixer_kernel(glu_ref, zb_ref, xc_ref, zc_ref, bufb_ref, bufc_ref, h0_ref,
                  cbw_ref, cbb_ref, gng_ref, gnb_ref, wpw_ref, gones_ref,
                  ccw_ref, ccb_ref, wa_ref, ba_ref, wx_ref, bx_ref, lam_ref,
                  bo_ref, co_ref, cbs_ref, ccs_ref, hs_ref,
                  extb, extc, hcar, cbuf, *, tt):
    j = pl.program_id(1)
    nb, nc = CONV_B - 1, CONV_C - 1

    @pl.when(j == 0)
    def _():
        extb[EXT_B0 - nb:EXT_B0, :] = bufb_ref[...]
        extc[EXT_C0 - nc:EXT_C0, :] = bufc_ref[...]
        hcar[...] = h0_ref[...]

    glu = glu_ref[...]
    extb[EXT_B0:EXT_B0 + tt, :] = glu[:, 0:W_B] * _sigmoid(glu[:, W_B:2 * W_B])
    step = min(CONV_ROWS, tt)
    for r0 in range(0, tt, step):
        acc = jnp.zeros((step, W_B), F32)
        for k in range(CONV_B):
            lo = EXT_B0 - nb + k + r0
            acc = acc + cbw_ref[k:k + 1, :] * extb[lo:lo + step, :]
        cbuf[r0:r0 + step, :] = acc + cbb_ref[...]
    cv = cbuf[...]
    gones = gones_ref[...]
    mu = _dot_exact_rhs(cv, gones) * (1.0 / GN_GROUP)
    d = cv - mu
    var = _dot_exact_rhs(d * d, gones) * (1.0 / GN_GROUP)
    cn = d * lax.rsqrt(var + EPS) * gng_ref[...] + gnb_ref[...]
    bo_ref[...] = _dot(_silu(cn).astype(BF16), wpw_ref[...]) * _silu(zb_ref[...])

    extc[EXT_C0:EXT_C0 + tt, :] = xc_ref[...]
    u = jnp.zeros((tt, W_C), F32)
    for k in range(CONV_C):
        lo = EXT_C0 - nc + k
        u = u + ccw_ref[k:k + 1, :] * extc[lo:lo + tt, :]
    u = u + ccb_ref[...]
    ub = u.astype(BF16)
    r = _sigmoid(_dot(ub, wa_ref[...]) + ba_ref[...])
    ig = _sigmoid(_dot(ub, wx_ref[...]) + bx_ref[...])
    nl = -lam_ref[...]
    softplus = jnp.maximum(nl, 0.0) + jnp.log1p(jnp.exp(-jnp.abs(nl)))
    log_a = -LRU_C * r * softplus
    a = jnp.exp(log_a)
    b = jnp.sqrt(-_expm1(2.0 * log_a)) * (ig * u)
    row = lax.broadcasted_iota(jnp.int32, (tt, W_C), 0)
    s = 1
    while s < tt:
        a_sh = jnp.where(row < s, 1.0, pltpu.roll(a, s, axis=0))
        b_sh = jnp.where(row < s, 0.0, pltpu.roll(b, s, axis=0))
        b = a * b_sh + b
        a = a * a_sh
        s *= 2
    hh = a * hcar[...] + b
    co_ref[...] = hh * _silu(zc_ref[...])
    hcar[...] = hh[tt - 1:tt, :]

    new_b = extb[EXT_B0 + tt - nb:EXT_B0 + tt, :]
    new_c = extc[EXT_C0 + tt - nc:EXT_C0 + tt, :]
    extb[EXT_B0 - nb:EXT_B0, :] = new_b
    extc[EXT_C0 - nc:EXT_C0, :] = new_c

    @pl.when(j == pl.num_programs(1) - 1)
    def _():
        cbs_ref[...] = new_b
        ccs_ref[...] = new_c
        hs_ref[...] = hh[tt - 1:tt, :]


def _mixers(proj, bufb, bufc, h0, lw, gones, b, t, tt):
    nj = t // tt

    def row_spec(width, col):
        return pl.BlockSpec((tt, width), lambda bi, j: (bi * nj + j, col // width))

    def st_spec(n, w):
        return pl.BlockSpec((None, n, w), lambda bi, j: (bi, 0, 0))

    def const_spec(a):
        return pl.BlockSpec(a.shape, lambda bi, j: (0,) * a.ndim)

    consts = [lw["cbw"], lw["cbb"], lw["gng"], lw["gnb"], lw["wpw"], gones,
              lw["ccw"], lw["ccb"], lw["wa"], lw["ba"], lw["wx"], lw["bx"], lw["lam"]]
    in_specs = ([row_spec(2 * W_B, COL_GLU), row_spec(W_B, COL_ZB), row_spec(W_C, COL_XC), row_spec(W_C, COL_ZC),
                 st_spec(CONV_B - 1, W_B), st_spec(CONV_C - 1, W_C), st_spec(1, W_C)]
                + [const_spec(a) for a in consts])
    out_specs = [pl.BlockSpec((tt, W_B), lambda bi, j: (bi * nj + j, 0)),
                 pl.BlockSpec((tt, W_C), lambda bi, j: (bi * nj + j, 0)),
                 st_spec(CONV_B - 1, W_B), st_spec(CONV_C - 1, W_C), st_spec(1, W_C)]
    out_shape = [jax.ShapeDtypeStruct((b * t, W_B), F32), jax.ShapeDtypeStruct((b * t, W_C), F32),
                 jax.ShapeDtypeStruct((b, CONV_B - 1, W_B), F32), jax.ShapeDtypeStruct((b, CONV_C - 1, W_C), F32),
                 jax.ShapeDtypeStruct((b, 1, W_C), F32)]
    scratch = [pltpu.VMEM((EXT_B0 + tt, W_B), F32), pltpu.VMEM((EXT_C0 + tt, W_C), F32),
               pltpu.VMEM((1, W_C), F32), pltpu.VMEM((tt, W_B), F32)]
    return pl.pallas_call(
        functools.partial(_mixer_kernel, tt=tt),
        grid=(b, nj),
        in_specs=in_specs, out_specs=out_specs, out_shape=out_shape, scratch_shapes=scratch,
        compiler_params=_cparams(("arbitrary", "arbitrary")),
        name="mixers",
    )(proj, proj, proj, proj, bufb, bufc, h0, *consts)


def _block_diag(w):
    nblk, c, d = w.shape
    eye = jnp.eye(nblk, dtype=w.dtype)
    return (eye[:, None, :, None] * w[:, :, None, :]).reshape(nblk * c, nblk * d)


def _row_tile(m, cap):
    t = min(m, cap)
    while m % t:
        t -= 8
    return t


def kernel(x_prompt, x_sample, cache_cmp_k, cache_cmp_v, cache_sel_k, cache_sel_v, cache_win_k, cache_win_v,
           state_conv_b, state_conv_c, state_rglru, page_table, rel_bias, g_pre, g_post, w_in, w_out,
           w_cmp_k, w_cmp_v, conv_b_w, conv_b_b, gn_gain, gn_bias, w_pw_b, conv_c_w, conv_c_b,
           w_lru_a, b_lru_a, w_lru_x, b_lru_x, lru_lambda):
    depth = w_in.shape[0]
    b, t, _ = x_prompt.shape
    db, dt, _ = x_sample.shape
    n_pages = page_table.shape[1]
    past = n_pages * PAGE
    nps = past // SEL_BLOCK
    ns, nt = t // SEL_BLOCK, t // TQ
    assert t % TQ == 0 and TOP_K <= ns <= 128 and TOP_K < nps <= 128
    assert dt <= 8 and (past + dt) // L_CMP == past // L_CMP and cache_win_k.shape[2] == WINDOW
    pages = min(8, n_pages)
    assert n_pages % pages == 0

    w_r = jnp.concatenate(
        [w_in[..., 0:512], w_in[..., 1304:1816], w_in[..., 1816:2328], w_in[..., 512:1280],
         w_in[..., 2328:3096], w_in[..., 1280:1304],
         jnp.zeros((depth, D_MODEL, D_IN_PAD - D_IN), w_in.dtype)], axis=-1).astype(BF16)
    w_o = w_out.astype(BF16)
    wck = jnp.tile(w_cmp_k, (1, 1, KV_A))
    wcv = jnp.tile(w_cmp_v, (1, 1, KV_A))
    gones = jnp.asarray(_group_ones_np(), BF16)
    e2 = jnp.asarray(_gate_expand_np(), BF16)
    e_p = jnp.asarray(_expand_np(ns, 128), BF16)
    e_d = jnp.asarray(_expand_np(nps, nps), BF16)

    tq_i, tk_i = np.arange(TQ)[:, None], np.arange(TQ)[None, :]
    bk_tiles = np.concatenate([_bucket_np(d0 + tq_i - tk_i) for d0 in (0, TQ, 2 * TQ)], axis=0)
    bt = _bias_lookup(rel_bias, bk_tiles).reshape(KV_A, REP_A, 3, TQ, TQ)
    bt = bt.transpose(0, 2, 1, 3, 4).reshape(KV_A, 3, REP_A * TQ, TQ)
    qp = np.arange(t)[:, None]
    blk_eo = np.concatenate([2 * np.arange(ns), 2 * np.arange(ns) + 1])[None, :]
    bc = _bias_lookup(rel_bias, _bucket_np(qp - (blk_eo * L_CMP + L_CMP - 1)))
    qd = past + np.arange(dt)[:, None]
    rows = REP_A * dt
    bce = _bias_lookup(rel_bias, _bucket_np(qd - (2 * np.arange(nps)[None, :] * L_CMP + L_CMP - 1)))
    bco = _bias_lookup(rel_bias, _bucket_np(qd - ((2 * np.arange(nps)[None, :] + 1) * L_CMP + L_CMP - 1)))
    bsel = _bias_lookup(rel_bias, _bucket_np(qd - np.arange(past)[None, :])).reshape(KV_A, rows, past)
    bnew = _bias_lookup(rel_bias, _bucket_np(np.arange(dt)[:, None] - np.arange(128)[None, :])).reshape(KV_A, rows, 128)
    bwin = _bias_lookup(rel_bias, _bucket_np(WINDOW + np.arange(dt)[:, None] - np.arange(WINDOW)[None, :]))
    bwin = bwin.reshape(KV_A, rows, WINDOW)
    dtabs = (bce, bco, bsel, bnew, bwin)

    pool = lambda a: a.reshape(a.shape[0], a.shape[1], PAGE, KV_COLS)
    cmp_tabs = _decode_compress(page_table, pool(cache_cmp_k), pool(cache_cmp_v), wck, wcv, pages)
    pool_sk, pool_sv = pool(cache_sel_k), pool(cache_sel_v)
    win_k = cache_win_k.reshape(depth, db, WINDOW, KV_COLS)
    win_v = cache_win_v.reshape(depth, db, WINDOW, KV_COLS)

    hp = x_prompt.reshape(b * t, D_MODEL)
    hs = x_sample.reshape(db * dt, D_MODEL)
    zeros_b = jnp.zeros((b, CONV_B - 1, W_B), F32)
    zeros_c = jnp.zeros((b, CONV_C - 1, W_C), F32)
    zeros_h = jnp.zeros((b, 1, W_C), F32)
    tm_p, tm_s = _row_tile(b * t, 512), _row_tile(db * dt, 512)
    tt_p = _row_tile(t, 256)

    p_states, s_states = [], []
    for l in range(depth):
        row = lambda a: a[l][None, :]
        lw = dict(cbw=conv_b_w[l], cbb=row(conv_b_b), gng=row(gn_gain), gnb=row(gn_bias), wpw=w_pw_b[l].astype(BF16),
                  ccw=conv_c_w[l], ccb=row(conv_c_b), wa=_block_diag(w_lru_a[l]).astype(BF16), ba=row(b_lru_a),
                  wx=_block_diag(w_lru_x[l]).astype(BF16), bx=row(b_lru_x), lam=row(lru_lambda))
        proj = _project(hp, row(g_pre), w_r[l], tm_p)
        a_out = _attn_prompt(proj, wck[l], wcv[l], bt, bc, e_p, e2, b, t)
        b_out, c_out, cb, cc, hc = _mixers(proj, zeros_b, zeros_c, zeros_h, lw, gones, b, t, tt_p)
        hp = _output(hp, a_out, b_out, c_out, w_o[l], row(g_post), tm_p)
        kv = proj[:, COL_KV:COL_KV + 6 * KV_COLS].reshape(b, t, 6, KV_A, HEAD_DIM)
        p_states.append((kv[:, :, 0], kv[:, :, 1], kv[:, :, 2], kv[:, :, 3],
                         kv[:, t - WINDOW:, 4], kv[:, t - WINDOW:, 5], cb, cc, hc[:, 0]))
        proj = _project(hs, row(g_pre), w_r[l], tm_s)
        a_out, wk_n, wv_n = _attn_decode(l, page_table, proj, cmp_tabs, pool_sk, pool_sv, win_k, win_v,
                                         dtabs, e_d, e2, pages, dt)
        b_out, c_out, cb, cc, hc = _mixers(proj, state_conv_b[l], state_conv_c[l], state_rglru[l][:, None, :],
                                           lw, gones, db, dt, dt)
        hs = _output(hs, a_out, b_out, c_out, w_o[l], row(g_post), tm_s)
        kv = proj[:, COL_KV:COL_KV + 6 * KV_COLS].reshape(db, dt, 6, KV_A, HEAD_DIM)
        s_states.append((kv[:, :, 0], kv[:, :, 1], kv[:, :, 2], kv[:, :, 3],
                         wk_n.reshape(db, WINDOW, KV_A, HEAD_DIM), wv_n.reshape(db, WINDOW, KV_A, HEAD_DIM),
                         cb, cc, hc[:, 0]))

    ck_p, cv_p, sk_p, sv_p, wk_p, wv_p, cb_p, cc_p, h_p = [jnp.stack(a) for a in zip(*p_states)]
    ck_s, cv_s, sk_s, sv_s, wk_s, wv_s, cb_s, cc_s, h_s = [jnp.stack(a) for a in zip(*s_states)]
    return (hp.reshape(b, t, D_MODEL), hs.reshape(db, dt, D_MODEL),
            ck_p, ck_s, cv_p, cv_s, sk_p, sk_s, sv_p, sv_s, wk_p, wk_s, wv_p, wv_s,
            cb_p, cb_s, cc_p, cc_s, h_p, h_s)
```

```python
import functools
import math

import numpy as np
import jax
import jax.numpy as jnp
from jax import lax
from jax.experimental import pallas as pl
from jax.experimental.pallas import tpu as pltpu

F32 = jnp.float32
BF16 = jnp.bfloat16

D_MODEL = 1024
HEAD_DIM = 64
W_A = D_MODEL // 2
W_B = D_MODEL // 4
W_C = D_MODEL // 4
H_A = W_A // HEAD_DIM
KV_A = 2
REP_A = H_A // KV_A
KV_COLS = KV_A * HEAD_DIM
L_CMP = 32
SEL_BLOCK = 64
TOP_K = 16
FORCE_SCORE = 1.0e4
WINDOW = 512
PAGE = 128
CONV_B = 31
CONV_C = 4
GN_GROUP = W_B // 4
LRU_C = 8.0
NUM_BUCKETS = 32
MAX_DISTANCE = 128
SM_SCALE = HEAD_DIM ** -0.5
LOG2E = 1.4426950408889634
EPS = 1e-6
NEG = -1e30

TQ = 256
TK = 256
QH = 128
D_IN = 3096
D_IN_PAD = 3200
COL_Q, COL_ZA, COL_GLU, COL_KV, COL_ZB, COL_XC, COL_ZC, COL_GATE = 0, 512, 1024, 1536, 2304, 2560, 2816, 3072
N_NAT = 1664
NAT_GLU, NAT_ZB, NAT_XC, NAT_ZC, NAT_KC, NAT_KS, NAT_KW = 0, 512, 768, 1024, 1280, 1408, 1536
N_TR = 1920
TR_Q, TR_ZA, TR_KV, TR_GATE = 0, 512, 1024, 1792
VMEM_LIMIT = 56 * 1024 * 1024


def _cparams(sem):
    return pltpu.CompilerParams(dimension_semantics=sem, vmem_limit_bytes=VMEM_LIMIT)


def _bucket_np(dist):
    n = np.maximum(dist, 0)
    max_exact = NUM_BUCKETS // 2
    nf = np.maximum(n, 1).astype(np.float32)
    large = max_exact + (np.log(nf / np.float32(max_exact)) / np.float32(math.log(MAX_DISTANCE / max_exact))
                         * np.float32(NUM_BUCKETS - max_exact)).astype(np.int32)
    return np.where(n < max_exact, n, np.minimum(large, NUM_BUCKETS - 1)).astype(np.int32)


def _expand_np(n_blocks, rows):
    e = np.zeros((rows, n_blocks * SEL_BLOCK), np.float32)
    for m in range(n_blocks):
        e[m, m * SEL_BLOCK:(m + 1) * SEL_BLOCK] = 1.0
    return e


def _gate_expand_np():
    e = np.zeros((128, 3 * W_A), np.float32)
    for br in range(3):
        for h in range(H_A):
            e[br * H_A + h, br * W_A + h * HEAD_DIM: br * W_A + (h + 1) * HEAD_DIM] = 1.0
    return e


def _group_ones_np():
    g = np.zeros((W_B, W_B), np.float32)
    for k in range(W_B // GN_GROUP):
        g[k * GN_GROUP:(k + 1) * GN_GROUP, k * GN_GROUP:(k + 1) * GN_GROUP] = 1.0
    return g


def _segment_np(n_rows, n_half):
    s = np.zeros((n_rows, 2 * n_half), np.float32)
    blk = np.arange(n_rows) // L_CMP
    s[np.arange(n_rows), np.where(blk % 2 == 0, blk // 2, n_half + blk // 2)] = 1.0
    return s


def _split2(a):
    hi = a.astype(BF16)
    return hi, (a - hi.astype(F32)).astype(BF16)


def _split3(a):
    hi = a.astype(BF16)
    r1 = a - hi.astype(F32)
    mid = r1.astype(BF16)
    return hi, mid, (r1 - mid.astype(F32)).astype(BF16)


def _dot(a, b):
    return jnp.dot(a, b, preferred_element_type=F32)


def _dot_nt(a, b):
    return lax.dot_general(a, b, (((1,), (1,)), ((), ())), preferred_element_type=F32)


def _dot_exact_rhs(a, b_bf16):
    hi, mid, lo = _split3(a)
    return _dot(hi, b_bf16) + _dot(mid, b_bf16) + _dot(lo, b_bf16)


def _dot_exact_lhs(a_bf16, b):
    hi, mid, lo = _split3(b)
    return _dot(a_bf16, hi) + _dot(a_bf16, mid) + _dot(a_bf16, lo)


def _dot_f32(a, b):
    ah, al = _split2(a)
    bh, bl = _split2(b)
    return _dot(ah, bh) + _dot(al, bh) + _dot(ah, bl)


def _sigmoid(x):
    return 1.0 / (1.0 + jnp.exp(-x))


def _silu(x):
    return x * _sigmoid(x)


def _expm1(x):
    u = jnp.exp(x)
    safe = jnp.where((u == 1.0) | (u == 0.0), 0.5, u)
    return jnp.where(u == 1.0, x, jnp.where(u == 0.0, -1.0, (safe - 1.0) * x / jnp.log(safe)))


def _rank_select(score, idx, n, axis):
    rank = jnp.zeros(score.shape, F32)
    for m in range(n):
        sm = score[m:m + 1, :] if axis == 0 else score[:, m:m + 1]
        beats = (sm > score) | ((sm == score) & (idx > m))
        rank = rank + jnp.where(beats, 1.0, 0.0)
    return rank


def _bias_kernel(rb_ref, bk_ref, o_ref, *, shift, scale):
    h = pl.program_id(0)
    bk = bk_ref[...]
    acc = jnp.zeros(bk.shape, F32)
    for b in range(NUM_BUCKETS):
        acc = jnp.where(bk == b, rb_ref[b, h], acc)
    if shift:
        acc = acc - rb_ref[NUM_BUCKETS - 1, h]
    o_ref[...] = acc * scale


def _bias_lookup(rel_bias, bucket, shift=False, scale=1.0):
    r, c = bucket.shape
    return pl.pallas_call(
        functools.partial(_bias_kernel, shift=shift, scale=scale),
        grid=(H_A,),
        in_specs=[pl.BlockSpec(memory_space=pltpu.SMEM),
                  pl.BlockSpec((r, c), lambda h: (0, 0))],
        out_specs=pl.BlockSpec((None, r, c), lambda h: (h, 0, 0)),
        out_shape=jax.ShapeDtypeStruct((H_A, r, c), F32),
        compiler_params=_cparams(("arbitrary",)),
        name="bias_lookup",
    )(rel_bias, jnp.asarray(bucket))


def _rms_bf16(x_ref, g_ref):
    x = x_ref[...]
    ms = jnp.mean(x * x, axis=-1, keepdims=True)
    return (x * lax.rsqrt(ms + EPS) * g_ref[...]).astype(BF16)


def _proj_kernel(x_ref, g_ref, w_ref, o_ref):
    u = _rms_bf16(x_ref, g_ref)
    step = 640
    for c in range(0, o_ref.shape[1], step):
        o_ref[:, c:c + step] = _dot(u, w_ref[:, c:c + step])


def _project(h, g, w, tm):
    m = h.shape[0]
    return pl.pallas_call(
        _proj_kernel,
        grid=(m // tm,),
        in_specs=[pl.BlockSpec((tm, D_MODEL), lambda i: (i, 0)),
                  pl.BlockSpec((1, D_MODEL), lambda i: (0, 0)),
                  pl.BlockSpec((D_MODEL, D_IN_PAD), lambda i: (0, 0))],
        out_specs=pl.BlockSpec((tm, D_IN_PAD), lambda i: (i, 0)),
        out_shape=jax.ShapeDtypeStruct((m, D_IN_PAD), F32),
        compiler_params=_cparams(("arbitrary",)),
        name="project",
    )(h, g, w)


def _proj2_kernel(x_ref, g_ref, wn_ref, wt_ref, on_ref, ot_ref):
    u = _rms_bf16(x_ref, g_ref)
    for c0, c1 in ((0, 768), (768, N_NAT)):
        on_ref[:, c0:c1] = _dot(u, wn_ref[:, c0:c1])
    for c in range(0, N_TR, 384):
        ot_ref[c:c + 384, :] = _dot_nt(wt_ref[c:c + 384, :], u)


def _project_prompt(h, g, wn, wt, b, t, tm):
    per = t // tm
    return pl.pallas_call(
        _proj2_kernel,
        grid=(b * per,),
        in_specs=[pl.BlockSpec((tm, D_MODEL), lambda i: (i, 0)),
                  pl.BlockSpec((1, D_MODEL), lambda i: (0, 0)),
                  pl.BlockSpec((D_MODEL, N_NAT), lambda i: (0, 0)),
                  pl.BlockSpec((N_TR, D_MODEL), lambda i: (0, 0))],
        out_specs=[pl.BlockSpec((tm, N_NAT), lambda i: (i, 0)),
                   pl.BlockSpec((None, N_TR, tm), lambda i: (i // per, 0, i % per))],
        out_shape=[jax.ShapeDtypeStruct((b * t, N_NAT), F32), jax.ShapeDtypeStruct((b, N_TR, t), F32)],
        compiler_params=_cparams(("arbitrary",)),
        name="project_prompt",
    )(h, g, wn, wt)


def _out_kernel(h_ref, a_ref, b_ref, c_ref, w_ref, g_ref, o_ref):
    y = (_dot(a_ref[...].astype(BF16), w_ref[0:W_A, :])
         + _dot(b_ref[...].astype(BF16), w_ref[W_A:W_A + W_B, :])
         + _dot(c_ref[...].astype(BF16), w_ref[W_A + W_B:, :]))
    ms = jnp.mean(y * y, axis=-1, keepdims=True)
    o_ref[...] = h_ref[...] + y * lax.rsqrt(ms + EPS) * g_ref[...]


def _output(h, a, b, c, w, g, tm):
    m = h.shape[0]
    return pl.pallas_call(
        _out_kernel,
        grid=(m // tm,),
        in_specs=[pl.BlockSpec((tm, D_MODEL), lambda i: (i, 0)),
                  pl.BlockSpec((tm, W_A), lambda i: (i, 0)),
                  pl.BlockSpec((tm, W_B), lambda i: (i, 0)),
                  pl.BlockSpec((tm, W_C), lambda i: (i, 0)),
                  pl.BlockSpec((D_MODEL, D_MODEL), lambda i: (0, 0)),
                  pl.BlockSpec((1, D_MODEL), lambda i: (0, 0))],
        out_specs=pl.BlockSpec((tm, D_MODEL), lambda i: (i, 0)),
        out_shape=jax.ShapeDtypeStruct((m, D_MODEL), F32),
        compiler_params=_cparams(("arbitrary",)),
        name="output",
    )(h, a, b, c, w, g)


def _softmax_steps(scores, vts, states):
    stats = []
    for s, st in zip(scores, states):
        m_new = jnp.max(s, axis=0, keepdims=True)
        alpha = None
        if st is not None:
            m_new = jnp.maximum(st[0], m_new)
            alpha = jnp.exp2(st[0] - m_new)
        p = jnp.exp2(s - m_new)
        l_new = jnp.sum(p, axis=0, keepdims=True)
        if st is not None:
            l_new = alpha * st[1] + l_new
        stats.append((m_new, alpha, l_new, p.astype(BF16)))
    out = []
    for (m_new, alpha, l_new, p), vt, st in zip(stats, vts, states):
        pv = _dot(vt, p)
        out.append((m_new, l_new, pv if st is None else alpha * st[2] + pv))
    return out


def _attn_prompt_kernel(qt_ref, zat_ref, gt_ref, vct_ref, vst_ref, vwt_ref, kc_ref, ks_ref, kw_ref,
                        wk_ref, wvt_ref, seg_ref, bnear_ref, bct_ref, ett_ref, e2t_ref,
                        o_ref, kc_s, vct_s, ks_s, kw_s, vst_s, vwt_s, madd_s, *, ns, nt):
    i = pl.program_id(1)
    nsp = -(-ns // 8) * 8

    @pl.when(i == 0)
    def _():
        x3 = kc_ref[...].reshape(ns, SEL_BLOCK, KV_COLS)
        w = wk_ref[...]
        kc_s[0:ns, :] = jnp.sum(x3[:, :L_CMP, :] * w[None], axis=1)
        kc_s[ns:2 * ns, :] = jnp.sum(x3[:, L_CMP:, :] * w[None], axis=1)
        hi, mid = _split2(vct_ref[...] * wvt_ref[...])
        seg = seg_ref[...]
        vct_s[...] = _dot(hi, seg) + _dot(mid, seg)
        for g in range(KV_A):
            ks_s[g] = ks_ref[:, g * HEAD_DIM:(g + 1) * HEAD_DIM].astype(BF16)
            kw_s[g] = kw_ref[:, g * HEAD_DIM:(g + 1) * HEAD_DIM].astype(BF16)
        for c in range(nt):
            vst_s[c] = vst_ref[:, c * TK:(c + 1) * TK].astype(BF16)
            vwt_s[c] = vwt_ref[:, c * TK:(c + 1) * TK].astype(BF16)

    qt = qt_ref[...]
    q0 = i * TQ

    row_c = lax.broadcasted_iota(jnp.int32, (2 * ns, TQ), 0)
    qpos_c = q0 + lax.broadcasted_iota(jnp.int32, (2 * ns, TQ), 1)
    blk_c = jnp.where(row_c < ns, 2 * row_c, 2 * (row_c - ns) + 1)
    mask_c = qpos_c >= blk_c * L_CMP + (L_CMP - 1)
    blk_t = lax.broadcasted_iota(jnp.int32, (nsp, TQ), 0)
    cur_t = (q0 + lax.broadcasted_iota(jnp.int32, (nsp, TQ), 1)) // SEL_BLOCK
    forced = (blk_t == 0) | ((blk_t <= cur_t) & (blk_t > cur_t - 2))
    kq_gap = (lax.broadcasted_iota(jnp.int32, (TK, TQ), 0) - lax.broadcasted_iota(jnp.int32, (TK, TQ), 1))

    oc_parts = []
    for g in range(KV_A):
        kh, kl = _split2(kc_s[:, g * HEAD_DIM:(g + 1) * HEAD_DIM])
        vcg = vct_s[g * HEAD_DIM:(g + 1) * HEAD_DIM, :].astype(BF16)
        imp = jnp.zeros((2 * ns, TQ), F32)
        for r in range(REP_A):
            h = g * REP_A + r
            qh, ql = _split2(qt[h * HEAD_DIM:(h + 1) * HEAD_DIM, :])
            s = (_dot(kh, qh) + _dot(kl, qh) + _dot(kh, ql)) * SM_SCALE + bct_ref[h]
            s = jnp.where(mask_c, s, NEG)
            e = jnp.exp(s - jnp.max(s, axis=0, keepdims=True))
            p = e / jnp.sum(e, axis=0, keepdims=True)
            p = jnp.where(mask_c, p, 0.0)
            oc_parts.append(_dot(vcg, p.astype(BF16)))
            imp = imp + p
        simp = imp[0:ns, :] + imp[ns:2 * ns, :]
        if nsp > ns:
            simp = jnp.concatenate([simp, jnp.zeros((nsp - ns, TQ), F32)], axis=0)
        score = jnp.where(forced, FORCE_SCORE, simp)
        score = jnp.where(blk_t <= cur_t, score, NEG)
        rank = _rank_select(score, blk_t, ns, 0)
        sel = jnp.where((rank < TOP_K) & (blk_t <= cur_t), 1.0, 0.0)
        sel = jnp.concatenate([sel, jnp.zeros((128 - nsp, TQ), F32)], axis=0).astype(BF16)
        def expand(c, _, g=g, sel=sel):
            r0 = pl.multiple_of(c * TK, TK)
            mm = (_dot(ett_ref[pl.ds(r0, TK), :], sel) - 1.0) * (-NEG)
            madd_s[g, c] = jnp.where(kq_gap + r0 > q0, NEG, mm)
            return 0

        lax.fori_loop(0, i + 1, expand, 0)
    oc_t = jnp.concatenate(oc_parts, axis=0)

    ok1, ok2 = i >= 1, i >= 2
    c1, c2 = jnp.maximum(i - 1, 0), jnp.maximum(i - 2, 0)
    n_far = jnp.maximum(i - 1, 0)
    os_rows, ow_rows = [], []
    for g in range(KV_A):
        gs = slice(g * HEAD_DIM, (g + 1) * HEAD_DIM)
        os_cols, ow_cols = [], []
        for qa in range(TQ // QH):
            ls = slice(qa * QH, (qa + 1) * QH)
            heads = [g * REP_A + r for r in range(REP_A)]
            qs = [(qt[h * HEAD_DIM:(h + 1) * HEAD_DIM, ls] * (SM_SCALE * LOG2E)).astype(BF16) for h in heads]
            init = tuple((jnp.full((1, QH), NEG, F32), jnp.zeros((1, QH), F32), jnp.zeros((HEAD_DIM, QH), F32))
                         for _ in heads)
            gap = kq_gap[:, ls]

            def kv_sel(c):
                r0 = pl.multiple_of(c * TK, TK)
                return ks_s[g, pl.ds(r0, TK), :], vst_s[c, gs, :]

            def kv_win(c):
                r0 = pl.multiple_of(c * TK, TK)
                return kw_s[g, pl.ds(r0, TK), :], vwt_s[c, gs, :]

            def far(c, carry):
                kt, vt = kv_sel(c)
                ma = madd_s[g, c, :, ls]
                return tuple(_softmax_steps([_dot(kt, q) + ma for q in qs], [vt] * REP_A, carry))

            sel_st = lax.fori_loop(0, n_far, far, init)
            win_st = [None] * REP_A
            for c, ok, d_idx in ((c1, ok1, 1), (i, None, 0)):
                (ks_t, vs_t), (kw_t, vw_t) = kv_sel(c), kv_win(c)
                kcat = jnp.concatenate([ks_t, kw_t], axis=0)
                ma = madd_s[g, c, :, ls]
                s2 = [_dot(kcat, q) for q in qs]
                scores = []
                for r, h in enumerate(heads):
                    bias = bnear_ref[h, d_idx, :, ls]
                    if ok is None:
                        a_sel, a_win = ma + bias, jnp.where(gap <= 0, bias, NEG)
                    else:
                        a_sel, a_win = jnp.where(ok, ma + bias, NEG), jnp.where(ok, bias, NEG)
                    scores += [s2[r][0:TK] + a_sel, s2[r][TK:2 * TK] + a_win]
                new = _softmax_steps(scores, [vs_t, vw_t] * REP_A, [st for pair in zip(sel_st, win_st) for st in pair])
                sel_st, win_st = new[0::2], new[1::2]
            kt, vt = kv_win(c2)
            far_add = jnp.where((gap > 0) & ok2, 0.0, NEG)
            win_st = _softmax_steps([_dot(kt, q) + far_add for q in qs], [vt] * REP_A, win_st)
            os_cols.append([acc / l for (_, l, acc) in sel_st])
            ow_cols.append([acc / l for (_, l, acc) in win_st])
        for r in range(REP_A):
            os_rows.append(jnp.concatenate([cols[r] for cols in os_cols], axis=1))
            ow_rows.append(jnp.concatenate([cols[r] for cols in ow_cols], axis=1))
    os_t = jnp.concatenate(os_rows, axis=0)
    ow_t = jnp.concatenate(ow_rows, axis=0)

    ge = _dot_exact_lhs(e2t_ref[...], _sigmoid(gt_ref[...]))
    o_t = ge[0:W_A] * oc_t + ge[W_A:2 * W_A] * os_t + ge[2 * W_A:3 * W_A] * ow_t
    o_ref[...] = (o_t * _silu(zat_ref[...])).T


def _attn_prompt(proj_n, proj_t, wk, wvt, seg, bnear, bct, ett, e2t, b, t):
    ns, nt = t // SEL_BLOCK, t // TQ

    def tr_spec(rows, row0, width):
        if width == t:
            return pl.BlockSpec((None, rows, t), lambda bi, i: (bi, row0 // rows, 0))
        return pl.BlockSpec((None, rows, width), lambda bi, i: (bi, row0 // rows, i))

    def nat_spec(col0):
        return pl.BlockSpec((t, KV_COLS), lambda bi, i: (bi, col0 // KV_COLS))

    def const_spec(a):
        return pl.BlockSpec(a.shape, lambda bi, i: (0,) * a.ndim)

    in_specs = [tr_spec(W_A, TR_Q, TQ), tr_spec(W_A, TR_ZA, TQ), tr_spec(128, TR_GATE, TQ),
                tr_spec(KV_COLS, TR_KV + 1 * KV_COLS, t), tr_spec(KV_COLS, TR_KV + 3 * KV_COLS, t),
                tr_spec(KV_COLS, TR_KV + 5 * KV_COLS, t),
                nat_spec(NAT_KC), nat_spec(NAT_KS), nat_spec(NAT_KW),
                const_spec(wk), const_spec(wvt), const_spec(seg), const_spec(bnear),
                pl.BlockSpec((H_A, 2 * ns, TQ), lambda bi, i: (0, 0, i)),
                const_spec(ett), const_spec(e2t)]
    scratch = [pltpu.VMEM((2 * ns, KV_COLS), F32), pltpu.VMEM((KV_COLS, 2 * ns), F32),
               pltpu.VMEM((KV_A, t, HEAD_DIM), BF16), pltpu.VMEM((KV_A, t, HEAD_DIM), BF16),
               pltpu.VMEM((nt, KV_COLS, TK), BF16), pltpu.VMEM((nt, KV_COLS, TK), BF16),
               pltpu.VMEM((KV_A, nt, TK, TQ), F32)]
    return pl.pallas_call(
        functools.partial(_attn_prompt_kernel, ns=ns, nt=nt),
        grid=(b, nt),
        in_specs=in_specs,
        out_specs=pl.BlockSpec((TQ, W_A), lambda bi, i: (bi * nt + i, 0)),
        out_shape=jax.ShapeDtypeStruct((b * t, W_A), F32),
        scratch_shapes=scratch,
        compiler_params=_cparams(("arbitrary", "arbitrary")),
        name="attn_prompt",
    )(proj_t, proj_t, proj_t, proj_t, proj_t, proj_t, proj_n, proj_n, proj_n,
      wk, wvt, seg, bnear, bct, ett, e2t)


def _dcmp_kernel(pt_ref, *refs, pages):
    del pt_ref
    k_refs, v_refs = refs[0:pages], refs[pages:2 * pages]
    wkt_ref, wvt_ref, seg_ref = refs[2 * pages:2 * pages + 3]
    ko_ref, vo_ref = refs[2 * pages + 3:]
    for src, w_ref, o_ref in ((k_refs, wkt_ref, ko_ref), (v_refs, wvt_ref, vo_ref)):
        w = w_ref[...]
        acc = jnp.zeros(o_ref.shape, F32)
        for p in range(pages):
            hi, mid = _split2(src[p][...] * w)
            seg = seg_ref[p * PAGE:(p + 1) * PAGE, :]
            acc = acc + _dot(hi, seg) + _dot(mid, seg)
        o_ref[...] = acc


def _decode_compress(page_table, pool_k, pool_v, wkt, wvt, seg, pages):
    depth, db, n_pages = pool_k.shape[0], page_table.shape[0], page_table.shape[1]
    nc = n_pages // pages
    ncol = seg.shape[1]

    def page_spec(p):
        return pl.BlockSpec((None, None, KV_COLS, PAGE), lambda l, b, c, pt: (l, pt[b, c * pages + p], 0, 0))

    w_spec = pl.BlockSpec((None, KV_COLS, PAGE), lambda l, b, c, pt: (l, 0, 0))
    s_spec = pl.BlockSpec(seg.shape, lambda l, b, c, pt: (0, 0))
    o_spec = pl.BlockSpec((None, None, None, KV_COLS, ncol), lambda l, b, c, pt: (l, b, c, 0, 0))
    o_shape = jax.ShapeDtypeStruct((depth, db, nc, KV_COLS, ncol), F32)
    return pl.pallas_call(
        functools.partial(_dcmp_kernel, pages=pages),
        grid_spec=pltpu.PrefetchScalarGridSpec(
            num_scalar_prefetch=1,
            grid=(depth, db, nc),
            in_specs=[page_spec(p) for p in range(pages)] * 2 + [w_spec, w_spec, s_spec],
            out_specs=[o_spec] * 2),
        out_shape=[o_shape] * 2,
        compiler_params=_cparams(("arbitrary", "arbitrary", "arbitrary")),
        name="decode_compress",
    )(page_table, *([pool_k] * pages), *([pool_v] * pages), wkt, wvt, seg)


def _softmax_rows(qbd, kt, add, m, l):
    s = _dot(qbd, kt) + add
    m_new = jnp.maximum(m, jnp.max(s, axis=-1, keepdims=True))
    alpha = jnp.exp2(m - m_new)
    p = jnp.exp2(s - m_new)
    return m_new, alpha, alpha * l + jnp.sum(p, axis=-1, keepdims=True), p.astype(BF16)


def _attn_decode_kernel(pt_ref, *refs, pages, nps, nc, dt):
    del pt_ref
    (q_ref, za_ref, kv_ref, gate_ref, kct_ref, vct_ref) = refs[0:6]
    sk_refs, sv_refs = refs[6:6 + pages], refs[6 + pages:6 + 2 * pages]
    (wink_ref, winv_ref, bcd_ref, blast_ref, bnew_ref, bwin_ref, e_ref, e2_ref,
     o_ref, wk_o, wv_o, qbd_s, m_s, l_s, acc_s, madd_s, oc_s) = refs[6 + 2 * pages:]
    c = pl.program_id(1)
    ck = pages * PAGE
    rows = KV_A * REP_A * dt
    half = 4 * PAGE
    zpad = jnp.zeros((dt, HEAD_DIM), F32)

    @pl.when(c == 0)
    def _():
        q = q_ref[...]
        blocks = []
        for g in range(KV_A):
            for r in range(REP_A):
                h = g * REP_A + r
                piece = q[:, h * HEAD_DIM:(h + 1) * HEAD_DIM]
                blocks.append(jnp.concatenate([piece, zpad] if g == 0 else [zpad, piece], axis=1))
        qbd = jnp.concatenate(blocks, axis=0)
        qbd_s[...] = qbd
        s = _dot_f32(qbd, kct_ref[...]) * SM_SCALE + bcd_ref[...]
        e = jnp.exp(s - jnp.max(s, axis=-1, keepdims=True))
        p = e / jnp.sum(e, axis=-1, keepdims=True)
        oc_s[...] = _dot_nt(p.astype(BF16), vct_ref[...].astype(BF16))
        scores = []
        for g in range(KV_A):
            pe = [p[(g * REP_A + r) * dt:(g * REP_A + r + 1) * dt, 0:nps] for r in range(REP_A)]
            po = [p[(g * REP_A + r) * dt:(g * REP_A + r + 1) * dt, nps:2 * nps] for r in range(REP_A)]
            scores.append((pe[0] + pe[1] + pe[2] + pe[3]) + (po[0] + po[1] + po[2] + po[3]))
        simp = jnp.concatenate(scores, axis=0)
        blk = lax.broadcasted_iota(jnp.int32, simp.shape, 1)
        score = jnp.where((blk == 0) | (blk >= nps - 1), FORCE_SCORE, simp)
        rank = _rank_select(score, blk, nps, 1)
        sel = jnp.where(rank < TOP_K - 1, 1.0, 0.0).astype(BF16)
        madd = (_dot(sel, e_ref[...]) - 1.0) * (-NEG)
        madd = jnp.concatenate([madd[0:dt]] * REP_A + [madd[dt:2 * dt]] * REP_A, axis=0)
        for j in range(nc):
            tile = madd[:, j * ck:(j + 1) * ck]
            if j == nc - 1:
                tile = jnp.concatenate([tile[:, :ck - PAGE], tile[:, ck - PAGE:] + blast_ref[...]], axis=1)
            madd_s[j] = tile
        m_s[...] = jnp.full(m_s.shape, NEG, F32)
        l_s[...] = jnp.zeros(l_s.shape, F32)
        acc_s[...] = jnp.zeros(acc_s.shape, F32)

    qbd = (qbd_s[...] * (SM_SCALE * LOG2E)).astype(BF16)
    m, l, acc = m_s[...], l_s[...], acc_s[...]
    for hf in range(ck // half):
        pg = range(hf * 4, hf * 4 + 4)
        kt = jnp.concatenate([sk_refs[p][...] for p in pg], axis=1).astype(BF16)
        vt = jnp.concatenate([sv_refs[p][...] for p in pg], axis=1).astype(BF16)
        m, alpha, l, p = _softmax_rows(qbd, kt, madd_s[c, :, hf * half:(hf + 1) * half], m, l)
        acc = alpha * acc + _dot_nt(p, vt)
    m_s[...], l_s[...], acc_s[...] = m, l, acc

    @pl.when(c == nc - 1)
    def _():
        row_t = lax.broadcasted_iota(jnp.int32, (rows, 128), 0) % dt
        col = lax.broadcasted_iota(jnp.int32, (rows, 128), 1)
        new_add = jnp.where((col <= row_t) & (col < dt), bnew_ref[...], NEG)
        row_w = lax.broadcasted_iota(jnp.int32, (rows, WINDOW), 0) % dt
        col_w = lax.broadcasted_iota(jnp.int32, (rows, WINDOW), 1)
        win_add = jnp.where(col_w > row_w, bwin_ref[...], NEG)
        pad = jnp.zeros((128 - dt, KV_COLS), F32)

        def new_tile(src):
            return jnp.concatenate([kv_ref[:, src * KV_COLS:(src + 1) * KV_COLS], pad], axis=0).astype(BF16)

        def attend_nat(kn, vn, add, m, l, acc):
            s = _dot_nt(qbd, kn) + add
            m_new = jnp.maximum(m, jnp.max(s, axis=-1, keepdims=True))
            alpha = jnp.exp2(m - m_new)
            p = jnp.exp2(s - m_new)
            return m_new, alpha * l + jnp.sum(p, axis=-1, keepdims=True), alpha * acc + _dot(p.astype(BF16), vn)

        m2, l2, acc2 = attend_nat(new_tile(2), new_tile(3), new_add, m, l, acc)
        os_full = acc2 / l2
        init = (jnp.full((rows, 1), NEG, F32), jnp.zeros((rows, 1), F32), jnp.zeros((rows, KV_COLS), F32))
        carry = attend_nat(wink_ref[...].astype(BF16), winv_ref[...].astype(BF16), win_add, *init)
        m3, l3, acc3 = attend_nat(new_tile(4), new_tile(5), new_add, *carry)
        ow_full = acc3 / l3

        def heads(full):
            parts = []
            for g in range(KV_A):
                for r in range(REP_A):
                    r0 = (g * REP_A + r) * dt
                    parts.append(full[r0:r0 + dt, g * HEAD_DIM:(g + 1) * HEAD_DIM])
            return jnp.concatenate(parts, axis=1)

        sg = _sigmoid(gate_ref[...])
        ge = _dot_exact_rhs(sg, e2_ref[...])
        o = ge[:, 0:W_A] * heads(oc_s[...]) + ge[:, W_A:2 * W_A] * heads(os_full) + ge[:, 2 * W_A:] * heads(ow_full)
        o_ref[...] = o * _silu(za_ref[...])
        wk_o[...] = jnp.concatenate([wink_ref[dt:, :], kv_ref[:, 4 * KV_COLS:5 * KV_COLS]], axis=0)
        wv_o[...] = jnp.concatenate([winv_ref[dt:, :], kv_ref[:, 5 * KV_COLS:6 * KV_COLS]], axis=0)


def _attn_decode(layer, page_table, proj, kct, vct, pool_sk, pool_sv, win_k, win_v, tabs, e_sel, e2, pages, dt):
    db, n_pages = page_table.shape
    nps = n_pages * (PAGE // SEL_BLOCK)
    nc = n_pages // pages
    ck = pages * PAGE
    rows = KV_A * REP_A * dt
    bcd, blast, bnew, bwin = tabs

    def row_spec(width, col):
        return pl.BlockSpec((dt, width), lambda b, c, pt: (b, col // width))

    def page_spec(p):
        return pl.BlockSpec((None, None, KV_COLS, PAGE), lambda b, c, pt: (layer, pt[b, c * pages + p], 0, 0))

    cmp_spec = pl.BlockSpec((None, None, KV_COLS, 2 * nps), lambda b, c, pt: (layer, b, 0, 0))
    win_spec = pl.BlockSpec((None, None, WINDOW, KV_COLS), lambda b, c, pt: (layer, b, 0, 0))

    def const_spec(a):
        return pl.BlockSpec(a.shape, lambda b, c, pt: (0,) * a.ndim)

    in_specs = ([row_spec(W_A, COL_Q), row_spec(W_A, COL_ZA), row_spec(6 * KV_COLS, COL_KV), row_spec(128, COL_GATE)]
                + [cmp_spec] * 2 + [page_spec(p) for p in range(pages)] * 2 + [win_spec] * 2
                + [const_spec(a) for a in (bcd, blast, bnew, bwin, e_sel, e2)])
    out_specs = [pl.BlockSpec((dt, W_A), lambda b, c, pt: (b, 0)),
                 pl.BlockSpec((None, WINDOW, KV_COLS), lambda b, c, pt: (b, 0, 0)),
                 pl.BlockSpec((None, WINDOW, KV_COLS), lambda b, c, pt: (b, 0, 0))]
    out_shape = [jax.ShapeDtypeStruct((db * dt, W_A), F32),
                 jax.ShapeDtypeStruct((db, WINDOW, KV_COLS), F32),
                 jax.ShapeDtypeStruct((db, WINDOW, KV_COLS), F32)]
    scratch = [pltpu.VMEM((rows, KV_COLS), F32), pltpu.VMEM((rows, 1), F32), pltpu.VMEM((rows, 1), F32),
               pltpu.VMEM((rows, KV_COLS), F32), pltpu.VMEM((nc, rows, ck), F32), pltpu.VMEM((rows, KV_COLS), F32)]
    return pl.pallas_call(
        functools.partial(_attn_decode_kernel, pages=pages, nps=nps, nc=nc, dt=dt),
        grid_spec=pltpu.PrefetchScalarGridSpec(
            num_scalar_prefetch=1, grid=(db, nc), in_specs=in_specs, out_specs=out_specs, scratch_shapes=scratch),
        out_shape=out_shape,
        compiler_params=_cparams(("arbitrary", "arbitrary")),
        name="attn_decode",
    )(page_table, proj, proj, proj, proj, kct, vct, *([pool_sk] * pages), *([pool_sv] * pages),
      win_k, win_v, bcd, blast, bnew, bwin, e_sel, e2)


EXT_B0 = 32
EXT_C0 = 8
CONV_ROWS = 32


def _mixer_kernel(glu_ref, zb_ref, xc_ref, zc_ref, bufb_ref, bufc_ref, h0_ref,
                  cbw_ref, cbb_ref, gng_ref, gnb_ref, wpw_ref, gones_ref,
                  ccw_ref, ccb_ref, wa_ref, ba_ref, wx_ref, bx_ref, lam_ref,
                  bo_ref, co_ref, cbs_ref, ccs_ref, hs_ref,
                  extb, extc, hcar, cbuf, *, tt):
    j = pl.program_id(1)
    nb, nc = CONV_B - 1, CONV_C - 1

    @pl.when(j == 0)
    def _():
        extb[EXT_B0 - nb:EXT_B0, :] = bufb_ref[...]
        extc[EXT_C0 - nc:EXT_C0, :] = bufc_ref[...]
        hcar[...] = h0_ref[...]

    glu = glu_ref[...]
    extb[EXT_B0:EXT_B0 + tt, :] = glu[:, 0:W_B] * _sigmoid(glu[:, W_B:2 * W_B])
    step = min(CONV_ROWS, tt)
    for r0 in range(0, tt, step):
        acc = jnp.zeros((step, W_B), F32)
        for k in range(CONV_B):
            lo = EXT_B0 - nb + k + r0
            acc = acc + cbw_ref[k:k + 1, :] * extb[lo:lo + step, :]
        cbuf[r0:r0 + step, :] = acc + cbb_ref[...]
    cv = cbuf[...]
    gones = gones_ref[...]
    mu = _dot_exact_rhs(cv, gones) * (1.0 / GN_GROUP)
    d = cv - mu
    var = _dot_exact_rhs(d * d, gones) * (1.0 / GN_GROUP)
    cn = d * lax.rsqrt(var + EPS) * gng_ref[...] + gnb_ref[...]
    bo_ref[...] = _dot(_silu(cn).astype(BF16), wpw_ref[...]) * _silu(zb_ref[...])

    extc[EXT_C0:EXT_C0 + tt, :] = xc_ref[...]
    u = jnp.zeros((tt, W_C), F32)
    for k in range(CONV_C):
        lo = EXT_C0 - nc + k
        u = u + ccw_ref[k:k + 1, :] * extc[lo:lo + tt, :]
    u = u + ccb_ref[...]
    ub = u.astype(BF16)
    r = _sigmoid(_dot(ub, wa_ref[...]) + ba_ref[...])
    ig = _sigmoid(_dot(ub, wx_ref[...]) + bx_ref[...])
    nl = -lam_ref[...]
    softplus = jnp.maximum(nl, 0.0) + jnp.log1p(jnp.exp(-jnp.abs(nl)))
    log_a = -LRU_C * r * softplus
    a = jnp.exp(log_a)
    b = jnp.sqrt(-_expm1(2.0 * log_a)) * (ig * u)
    row = lax.broadcasted_iota(jnp.int32, (tt, W_C), 0)
    s = 1
    while s < tt:
        a_sh = jnp.where(row < s, 1.0, pltpu.roll(a, s, axis=0))
        b_sh = jnp.where(row < s, 0.0, pltpu.roll(b, s, axis=0))
        b = a * b_sh + b
        a = a * a_sh
        s *= 2
    hh = a * hcar[...] + b
    co_ref[...] = hh * _silu(zc_ref[...])
    hcar[...] = hh[tt - 1:tt, :]

    new_b = extb[EXT_B0 + tt - nb:EXT_B0 + tt, :]
    new_c = extc[EXT_C0 + tt - nc:EXT_C0 + tt, :]
    extb[EXT_B0 - nb:EXT_B0, :] = new_b
    extc[EXT_C0 - nc:EXT_C0, :] = new_c

    @pl.when(j == pl.num_programs(1) - 1)
    def _():
        cbs_ref[...] = new_b
        ccs_ref[...] = new_c
        hs_ref[...] = hh[tt - 1:tt, :]


def _mixers(proj, cols, bufb, bufc, h0, lw, gones, b, t, tt):
    nj = t // tt
    c_glu, c_zb, c_xc, c_zc = cols

    def row_spec(width, col):
        return pl.BlockSpec((tt, width), lambda bi, j: (bi * nj + j, col // width))

    def st_spec(n, w):
        return pl.BlockSpec((None, n, w), lambda bi, j: (bi, 0, 0))

    def const_spec(a):
        return pl.BlockSpec(a.shape, lambda bi, j: (0,) * a.ndim)

    consts = [lw["cbw"], lw["cbb"], lw["gng"], lw["gnb"], lw["wpw"], gones,
              lw["ccw"], lw["ccb"], lw["wa"], lw["ba"], lw["wx"], lw["bx"], lw["lam"]]
    in_specs = ([row_spec(2 * W_B, c_glu), row_spec(W_B, c_zb), row_spec(W_C, c_xc), row_spec(W_C, c_zc),
                 st_spec(CONV_B - 1, W_B), st_spec(CONV_C - 1, W_C), st_spec(1, W_C)]
                + [const_spec(a) for a in consts])
    out_specs = [pl.BlockSpec((tt, W_B), lambda bi, j: (bi * nj + j, 0)),
                 pl.BlockSpec((tt, W_C), lambda bi, j: (bi * nj + j, 0)),
                 st_spec(CONV_B - 1, W_B), st_spec(CONV_C - 1, W_C), st_spec(1, W_C)]
    out_shape = [jax.ShapeDtypeStruct((b * t, W_B), F32), jax.ShapeDtypeStruct((b * t, W_C), F32),
                 jax.ShapeDtypeStruct((b, CONV_B - 1, W_B), F32), jax.ShapeDtypeStruct((b, CONV_C - 1, W_C), F32),
                 jax.ShapeDtypeStruct((b, 1, W_C), F32)]
    scratch = [pltpu.VMEM((EXT_B0 + tt, W_B), F32), pltpu.VMEM((EXT_C0 + tt, W_C), F32),
               pltpu.VMEM((1, W_C), F32), pltpu.VMEM((tt, W_B), F32)]
    return pl.pallas_call(
        functools.partial(_mixer_kernel, tt=tt),
        grid=(b, nj),
        in_specs=in_specs, out_specs=out_specs, out_shape=out_shape, scratch_shapes=scratch,
        compiler_params=_cparams(("arbitrary", "arbitrary")),
        name="mixers",
    )(proj, proj, proj, proj, bufb, bufc, h0, *consts)


def _block_diag(w):
    nblk, c, d = w.shape
    eye = jnp.eye(nblk, dtype=w.dtype)
    return (eye[:, None, :, None] * w[:, :, None, :]).reshape(nblk * c, nblk * d)


def _row_tile(m, cap, step=8):
    t = min(m, cap)
    while m % t:
        t -= step
    return t


def kernel(x_prompt, x_sample, cache_cmp_k, cache_cmp_v, cache_sel_k, cache_sel_v, cache_win_k, cache_win_v,
           state_conv_b, state_conv_c, state_rglru, page_table, rel_bias, g_pre, g_post, w_in, w_out,
           w_cmp_k, w_cmp_v, conv_b_w, conv_b_b, gn_gain, gn_bias, w_pw_b, conv_c_w, conv_c_b,
           w_lru_a, b_lru_a, w_lru_x, b_lru_x, lru_lambda):
    depth = w_in.shape[0]
    b, t, _ = x_prompt.shape
    db, dt, _ = x_sample.shape
    n_pages = page_table.shape[1]
    past = n_pages * PAGE
    nps = past // SEL_BLOCK
    ns, nt = t // SEL_BLOCK, t // TQ
    assert t % TQ == 0 and TQ == TK and WINDOW == 2 * TK and TOP_K <= ns <= 128 and TOP_K < nps <= 128
    assert dt <= 8 and (past + dt) // L_CMP == past // L_CMP and cache_win_k.shape[2] == WINDOW
    pages = min(8, n_pages)
    assert n_pages % pages == 0 and pages % 4 == 0
    nc = n_pages // pages
    rows = KV_A * REP_A * dt

    zpad = jnp.zeros((depth, D_MODEL, D_IN_PAD - D_IN), w_in.dtype)
    w_r = jnp.concatenate(
        [w_in[..., 0:512], w_in[..., 1304:1816], w_in[..., 1816:2328], w_in[..., 512:1280],
         w_in[..., 2328:3096], w_in[..., 1280:1304], zpad], axis=-1).astype(BF16)
    w_n = jnp.concatenate(
        [w_in[..., 1816:3096], w_in[..., 512:640], w_in[..., 768:896], w_in[..., 1024:1152]], axis=-1).astype(BF16)
    w_it = jnp.swapaxes(w_in, 1, 2)
    w_t = jnp.concatenate(
        [w_it[:, 0:512], w_it[:, 1304:1816], w_it[:, 512:1280], w_it[:, 1280:1304],
         jnp.zeros((depth, N_TR - 1816, D_MODEL), w_in.dtype)], axis=1).astype(BF16)
    w_o = w_out.astype(BF16)
    wck = jnp.tile(w_cmp_k, (1, 1, KV_A))
    wvt_p = jnp.tile(jnp.swapaxes(w_cmp_v, 1, 2), (1, KV_A, t // L_CMP))
    wkt_d = jnp.tile(jnp.swapaxes(w_cmp_k, 1, 2), (1, KV_A, PAGE // L_CMP))
    wvt_d = jnp.tile(jnp.swapaxes(w_cmp_v, 1, 2), (1, KV_A, PAGE // L_CMP))
    gones = jnp.asarray(_group_ones_np(), BF16)
    e2 = jnp.asarray(_gate_expand_np(), BF16)
    e2t = jnp.asarray(_gate_expand_np().T, BF16)
    ett = jnp.asarray(_expand_np(ns, 128).T, BF16)
    e_d = jnp.asarray(_expand_np(nps, nps), BF16)
    seg_p = jnp.asarray(_segment_np(t, ns), BF16)
    seg_d = jnp.asarray(_segment_np(pages * PAGE, pages * PAGE // SEL_BLOCK), BF16)

    kk, qq = np.arange(TK)[:, None], np.arange(TQ)[None, :]
    bk_near = np.concatenate([_bucket_np(d0 + qq - kk) for d0 in (0, TQ)], axis=0)
    bnear = _bias_lookup(rel_bias, bk_near, shift=True, scale=LOG2E).reshape(H_A, 2, TK, TQ)
    blk_eo = np.concatenate([2 * np.arange(ns), 2 * np.arange(ns) + 1])[:, None]
    bct = _bias_lookup(rel_bias, _bucket_np(np.arange(t)[None, :] - (blk_eo * L_CMP + L_CMP - 1)))
    qd = past + np.arange(dt)[:, None]
    blk_d = np.concatenate([2 * np.arange(nps), 2 * np.arange(nps) + 1])[None, :]
    bcd = _bias_lookup(rel_bias, _bucket_np(qd - (blk_d * L_CMP + L_CMP - 1))).reshape(rows, 2 * nps)
    blast = _bias_lookup(rel_bias, _bucket_np(qd - (past - PAGE + np.arange(PAGE))[None, :]), shift=True, scale=LOG2E)
    bnew = _bias_lookup(rel_bias, _bucket_np(np.arange(dt)[:, None] - np.arange(128)[None, :]), shift=True, scale=LOG2E)
    bwin = _bias_lookup(rel_bias, _bucket_np(WINDOW + np.arange(dt)[:, None] - np.arange(WINDOW)[None, :]),
                        shift=True, scale=LOG2E)
    dtabs = (bcd, blast.reshape(rows, PAGE), bnew.reshape(rows, 128), bwin.reshape(rows, WINDOW))

    pool = lambda a: jnp.transpose(a, (0, 1, 3, 4, 2)).reshape(a.shape[0], a.shape[1], KV_COLS, PAGE)
    kc_ch, vc_ch = _decode_compress(page_table, pool(cache_cmp_k), pool(cache_cmp_v), wkt_d, wvt_d, seg_d, pages)

    def eo_table(x):
        x = x.reshape(depth, db, nc, KV_COLS, 2, nps // nc).transpose(0, 1, 3, 4, 2, 5)
        return x.reshape(depth, db, KV_COLS, 2 * nps)

    kct, vct = eo_table(kc_ch), eo_table(vc_ch)
    pool_sk, pool_sv = pool(cache_sel_k), pool(cache_sel_v)
    win_k = cache_win_k.reshape(depth, db, WINDOW, KV_COLS)
    win_v = cache_win_v.reshape(depth, db, WINDOW, KV_COLS)

    hp = x_prompt.reshape(b * t, D_MODEL)
    hs = x_sample.reshape(db * dt, D_MODEL)
    zeros_b = jnp.zeros((b, CONV_B - 1, W_B), F32)
    zeros_c = jnp.zeros((b, CONV_C - 1, W_C), F32)
    zeros_h = jnp.zeros((b, 1, W_C), F32)
    tm_p, tm_s = _row_tile(t, 512, 128), _row_tile(db * dt, 512)
    tt_p = _row_tile(t, 256)
    nat_cols = (NAT_GLU, NAT_ZB, NAT_XC, NAT_ZC)
    dec_cols = (COL_GLU, COL_ZB, COL_XC, COL_ZC)

    p_states, s_states = [], []
    for l in range(depth):
        row = lambda a: a[l][None, :]
        lw = dict(cbw=conv_b_w[l], cbb=row(conv_b_b), gng=row(gn_gain), gnb=row(gn_bias), wpw=w_pw_b[l].astype(BF16),
                  ccw=conv_c_w[l], ccb=row(conv_c_b), wa=_block_diag(w_lru_a[l]).astype(BF16), ba=row(b_lru_a),
                  wx=_block_diag(w_lru_x[l]).astype(BF16), bx=row(b_lru_x), lam=row(lru_lambda))
        proj_n, proj_t = _project_prompt(hp, row(g_pre), w_n[l], w_t[l], b, t, tm_p)
        a_out = _attn_prompt(proj_n, proj_t, wck[l], wvt_p[l], seg_p, bnear, bct, ett, e2t, b, t)
        b_out, c_out, cb, cc, hc = _mixers(proj_n, nat_cols, zeros_b, zeros_c, zeros_h, lw, gones, b, t, tt_p)
        hp = _output(hp, a_out, b_out, c_out, w_o[l], row(g_post), tm_p)
        kvt = proj_t[:, TR_KV:TR_KV + 6 * KV_COLS, :].reshape(b, 6, KV_A, HEAD_DIM, t)
        p_states.append(tuple(kvt[:, k] for k in range(4))
                        + (kvt[:, 4, :, :, t - WINDOW:], kvt[:, 5, :, :, t - WINDOW:], cb, cc, hc[:, 0]))
        proj = _project(hs, row(g_pre), w_r[l], tm_s)
        a_out, wk_n, wv_n = _attn_decode(l, page_table, proj, kct, vct, pool_sk, pool_sv, win_k, win_v,
                                         dtabs, e_d, e2, pages, dt)
        b_out, c_out, cb, cc, hc = _mixers(proj, dec_cols, state_conv_b[l], state_conv_c[l],
                                           state_rglru[l][:, None, :], lw, gones, db, dt, dt)
        hs = _output(hs, a_out, b_out, c_out, w_o[l], row(g_post), tm_s)
        kv = proj[:, COL_KV:COL_KV + 6 * KV_COLS].reshape(db, dt, 6, KV_A, HEAD_DIM)
        s_states.append((kv[:, :, 0], kv[:, :, 1], kv[:, :, 2], kv[:, :, 3],
                         wk_n.reshape(db, WINDOW, KV_A, HEAD_DIM), wv_n.reshape(db, WINDOW, KV_A, HEAD_DIM),
                         cb, cc, hc[:, 0]))

    ck_p, cv_p, sk_p, sv_p, wk_p, wv_p, cb_p, cc_p, h_p = [jnp.stack(a) for a in zip(*p_states)]
    ck_p, cv_p, sk_p, sv_p, wk_p, wv_p = [jnp.transpose(a, (0, 1, 4, 2, 3)) for a in (ck_p, cv_p, sk_p, sv_p, wk_p, wv_p)]
    ck_s, cv_s, sk_s, sv_s, wk_s, wv_s, cb_s, cc_s, h_s = [jnp.stack(a) for a in zip(*s_states)]
    return (hp.reshape(b, t, D_MODEL), hs.reshape(db, dt, D_MODEL),
            ck_p, ck_s, cv_p, cv_s, sk_p, sk_s, sv_p, sv_s, wk_p, wk_s, wv_p, wv_s,
            cb_p, cb_s, cc_p, cc_s, h_p, h_s)
```

```python
import functools
import math

import numpy as np
import jax
import jax.numpy as jnp
from jax import lax
from jax.experimental import pallas as pl
from jax.experimental.pallas import tpu as pltpu

F32 = jnp.float32
BF16 = jnp.bfloat16

D_MODEL = 1024
HEAD_DIM = 64
W_A = D_MODEL // 2
W_B = D_MODEL // 4
W_C = D_MODEL // 4
H_A = W_A // HEAD_DIM
KV_A = 2
REP_A = H_A // KV_A
KV_COLS = KV_A * HEAD_DIM
L_CMP = 32
SEL_BLOCK = 64
TOP_K = 16
FORCE_SCORE = 1.0e4
WINDOW = 512
PAGE = 128
CONV_B = 31
CONV_C = 4
GN_GROUP = W_B // 4
LRU_C = 8.0
NUM_BUCKETS = 32
MAX_DISTANCE = 128
SM_SCALE = HEAD_DIM ** -0.5
LOG2E = 1.4426950408889634
EPS = 1e-6
NEG = -1e30

TQ = 256
TK = 256
D_IN = 3096
D_IN_PAD = 3200
COL_Q, COL_ZA, COL_GLU, COL_KV, COL_ZB, COL_XC, COL_ZC, COL_GATE = 0, 512, 1024, 1536, 2304, 2560, 2816, 3072
N_NAT = 1664
NAT_GLU, NAT_ZB, NAT_XC, NAT_ZC, NAT_KC, NAT_KS, NAT_KW = 0, 512, 768, 1024, 1280, 1408, 1536
N_TR = 1920
TR_Q, TR_ZA, TR_KV, TR_GATE = 0, 512, 1024, 1792
VMEM_LIMIT = 56 * 1024 * 1024


def _cparams(sem):
    return pltpu.CompilerParams(dimension_semantics=sem, vmem_limit_bytes=VMEM_LIMIT)


def _bucket_np(dist):
    n = np.maximum(dist, 0)
    max_exact = NUM_BUCKETS // 2
    nf = np.maximum(n, 1).astype(np.float32)
    large = max_exact + (np.log(nf / np.float32(max_exact)) / np.float32(math.log(MAX_DISTANCE / max_exact))
                         * np.float32(NUM_BUCKETS - max_exact)).astype(np.int32)
    return np.where(n < max_exact, n, np.minimum(large, NUM_BUCKETS - 1)).astype(np.int32)


def _expand_np(n_blocks, rows):
    e = np.zeros((rows, n_blocks * SEL_BLOCK), np.float32)
    for m in range(n_blocks):
        e[m, m * SEL_BLOCK:(m + 1) * SEL_BLOCK] = 1.0
    return e


def _gate_expand_np():
    e = np.zeros((128, 3 * W_A), np.float32)
    for br in range(3):
        for h in range(H_A):
            e[br * H_A + h, br * W_A + h * HEAD_DIM: br * W_A + (h + 1) * HEAD_DIM] = 1.0
    return e


def _group_ones_np():
    g = np.zeros((W_B, W_B), np.float32)
    for k in range(W_B // GN_GROUP):
        g[k * GN_GROUP:(k + 1) * GN_GROUP, k * GN_GROUP:(k + 1) * GN_GROUP] = 1.0
    return g


def _segment_np(n_rows, n_half):
    s = np.zeros((n_rows, 2 * n_half), np.float32)
    blk = np.arange(n_rows) // L_CMP
    s[np.arange(n_rows), np.where(blk % 2 == 0, blk // 2, n_half + blk // 2)] = 1.0
    return s


def _split2(a):
    hi = a.astype(BF16)
    return hi, (a - hi.astype(F32)).astype(BF16)


def _split3(a):
    hi = a.astype(BF16)
    r1 = a - hi.astype(F32)
    mid = r1.astype(BF16)
    return hi, mid, (r1 - mid.astype(F32)).astype(BF16)


def _dot(a, b):
    return jnp.dot(a, b, preferred_element_type=F32)


def _dot_nt(a, b):
    return lax.dot_general(a, b, (((1,), (1,)), ((), ())), preferred_element_type=F32)


def _dot_exact_rhs(a, b_bf16):
    hi, mid, lo = _split3(a)
    return _dot(hi, b_bf16) + _dot(mid, b_bf16) + _dot(lo, b_bf16)


def _dot_exact_lhs(a_bf16, b):
    hi, mid, lo = _split3(b)
    return _dot(a_bf16, hi) + _dot(a_bf16, mid) + _dot(a_bf16, lo)


def _dot_f32(a, b):
    ah, al = _split2(a)
    bh, bl = _split2(b)
    return _dot(ah, bh) + _dot(al, bh) + _dot(ah, bl)


def _sigmoid(x):
    return 1.0 / (1.0 + jnp.exp(-x))


def _silu(x):
    return x * _sigmoid(x)


def _expm1(x):
    u = jnp.exp(x)
    safe = jnp.where((u == 1.0) | (u == 0.0), 0.5, u)
    return jnp.where(u == 1.0, x, jnp.where(u == 0.0, -1.0, (safe - 1.0) * x / jnp.log(safe)))


def _rank_select(score, idx, n, axis):
    rank = jnp.zeros(score.shape, F32)
    for m in range(n):
        sm = score[m:m + 1, :] if axis == 0 else score[:, m:m + 1]
        beats = (sm > score) | ((sm == score) & (idx > m))
        rank = rank + jnp.where(beats, 1.0, 0.0)
    return rank


def _bias_kernel(rb_ref, bk_ref, o_ref, *, shift, scale):
    h = pl.program_id(0)
    bk = bk_ref[...]
    acc = jnp.zeros(bk.shape, F32)
    for b in range(NUM_BUCKETS):
        acc = jnp.where(bk == b, rb_ref[b, h], acc)
    if shift:
        acc = acc - rb_ref[NUM_BUCKETS - 1, h]
    o_ref[...] = acc * scale


def _bias_lookup(rel_bias, bucket, shift=False, scale=1.0):
    r, c = bucket.shape
    return pl.pallas_call(
        functools.partial(_bias_kernel, shift=shift, scale=scale),
        grid=(H_A,),
        in_specs=[pl.BlockSpec(memory_space=pltpu.SMEM),
                  pl.BlockSpec((r, c), lambda h: (0, 0))],
        out_specs=pl.BlockSpec((None, r, c), lambda h: (h, 0, 0)),
        out_shape=jax.ShapeDtypeStruct((H_A, r, c), F32),
        compiler_params=_cparams(("arbitrary",)),
        name="bias_lookup",
    )(rel_bias, jnp.asarray(bucket))


def _rms_bf16(x_ref, g_ref):
    x = x_ref[...]
    ms = jnp.mean(x * x, axis=-1, keepdims=True)
    return (x * lax.rsqrt(ms + EPS) * g_ref[...]).astype(BF16)


def _proj_kernel(x_ref, g_ref, w_ref, o_ref):
    u = _rms_bf16(x_ref, g_ref)
    step = 640
    for c in range(0, o_ref.shape[1], step):
        o_ref[:, c:c + step] = _dot(u, w_ref[:, c:c + step])


def _project(h, g, w, tm):
    m = h.shape[0]
    return pl.pallas_call(
        _proj_kernel,
        grid=(m // tm,),
        in_specs=[pl.BlockSpec((tm, D_MODEL), lambda i: (i, 0)),
                  pl.BlockSpec((1, D_MODEL), lambda i: (0, 0)),
                  pl.BlockSpec((D_MODEL, D_IN_PAD), lambda i: (0, 0))],
        out_specs=pl.BlockSpec((tm, D_IN_PAD), lambda i: (i, 0)),
        out_shape=jax.ShapeDtypeStruct((m, D_IN_PAD), F32),
        compiler_params=_cparams(("arbitrary",)),
        name="project",
    )(h, g, w)


def _proj2_kernel(x_ref, g_ref, wn_ref, wt_ref, on_ref, ot_ref):
    u = _rms_bf16(x_ref, g_ref)
    for c0, c1 in ((0, 768), (768, N_NAT)):
        on_ref[:, c0:c1] = _dot(u, wn_ref[:, c0:c1])
    for c in range(0, N_TR, 384):
        ot_ref[c:c + 384, :] = _dot_nt(wt_ref[c:c + 384, :], u)


def _project_prompt(h, g, wn, wt, b, t, tm):
    per = t // tm
    return pl.pallas_call(
        _proj2_kernel,
        grid=(b * per,),
        in_specs=[pl.BlockSpec((tm, D_MODEL), lambda i: (i, 0)),
                  pl.BlockSpec((1, D_MODEL), lambda i: (0, 0)),
                  pl.BlockSpec((D_MODEL, N_NAT), lambda i: (0, 0)),
                  pl.BlockSpec((N_TR, D_MODEL), lambda i: (0, 0))],
        out_specs=[pl.BlockSpec((tm, N_NAT), lambda i: (i, 0)),
                   pl.BlockSpec((None, N_TR, tm), lambda i: (i // per, 0, i % per))],
        out_shape=[jax.ShapeDtypeStruct((b * t, N_NAT), F32), jax.ShapeDtypeStruct((b, N_TR, t), F32)],
        compiler_params=_cparams(("arbitrary",)),
        name="project_prompt",
    )(h, g, wn, wt)


def _out_kernel(h_ref, a_ref, b_ref, c_ref, w_ref, g_ref, o_ref):
    y = (_dot(a_ref[...].astype(BF16), w_ref[0:W_A, :])
         + _dot(b_ref[...].astype(BF16), w_ref[W_A:W_A + W_B, :])
         + _dot(c_ref[...].astype(BF16), w_ref[W_A + W_B:, :]))
    ms = jnp.mean(y * y, axis=-1, keepdims=True)
    o_ref[...] = h_ref[...] + y * lax.rsqrt(ms + EPS) * g_ref[...]


def _output(h, a, b, c, w, g, tm):
    m = h.shape[0]
    return pl.pallas_call(
        _out_kernel,
        grid=(m // tm,),
        in_specs=[pl.BlockSpec((tm, D_MODEL), lambda i: (i, 0)),
                  pl.BlockSpec((tm, W_A), lambda i: (i, 0)),
                  pl.BlockSpec((tm, W_B), lambda i: (i, 0)),
                  pl.BlockSpec((tm, W_C), lambda i: (i, 0)),
                  pl.BlockSpec((D_MODEL, D_MODEL), lambda i: (0, 0)),
                  pl.BlockSpec((1, D_MODEL), lambda i: (0, 0))],
        out_specs=pl.BlockSpec((tm, D_MODEL), lambda i: (i, 0)),
        out_shape=jax.ShapeDtypeStruct((m, D_MODEL), F32),
        compiler_params=_cparams(("arbitrary",)),
        name="output",
    )(h, a, b, c, w, g)


def _softmax_steps(scores, vts, states):
    stats = []
    for s, st in zip(scores, states):
        m_new = jnp.max(s, axis=0, keepdims=True)
        alpha = None
        if st is not None:
            m_new = jnp.maximum(st[0], m_new)
            alpha = jnp.exp2(st[0] - m_new)
        p = jnp.exp2(s - m_new)
        l_new = jnp.sum(p, axis=0, keepdims=True)
        if st is not None:
            l_new = alpha * st[1] + l_new
        stats.append((m_new, alpha, l_new, p.astype(BF16)))
    out = []
    for (m_new, alpha, l_new, p), vt, st in zip(stats, vts, states):
        pv = _dot(vt, p)
        out.append((m_new, l_new, pv if st is None else alpha * st[2] + pv))
    return out


def _attn_prompt_kernel(qt_ref, zat_ref, gt_ref, vct_ref, vst_ref, vwt_ref, kc_ref, ks_ref, kw_ref,
                        wk_ref, wvt_ref, seg_ref, bnear_ref, bct_ref, ett_ref, e2t_ref,
                        o_ref, kc_s, vct_s, ks_s, kw_s, vst_s, vwt_s, madd_s, *, ns, nt):
    i = pl.program_id(1)
    nsp = -(-ns // 8) * 8

    @pl.when(i == 0)
    def _():
        x3 = kc_ref[...].reshape(ns, SEL_BLOCK, KV_COLS)
        w = wk_ref[...]
        kc_s[0:ns, :] = jnp.sum(x3[:, :L_CMP, :] * w[None], axis=1)
        kc_s[ns:2 * ns, :] = jnp.sum(x3[:, L_CMP:, :] * w[None], axis=1)
        hi, mid = _split2(vct_ref[...] * wvt_ref[...])
        seg = seg_ref[...]
        vct_s[...] = _dot(hi, seg) + _dot(mid, seg)
        for g in range(KV_A):
            ks_s[g] = ks_ref[:, g * HEAD_DIM:(g + 1) * HEAD_DIM].astype(BF16)
            kw_s[g] = kw_ref[:, g * HEAD_DIM:(g + 1) * HEAD_DIM].astype(BF16)
        for c in range(nt):
            vst_s[c] = vst_ref[:, c * TK:(c + 1) * TK].astype(BF16)
            vwt_s[c] = vwt_ref[:, c * TK:(c + 1) * TK].astype(BF16)

    qt = qt_ref[...]
    q0 = i * TQ

    row_c = lax.broadcasted_iota(jnp.int32, (2 * ns, TQ), 0)
    qpos_c = q0 + lax.broadcasted_iota(jnp.int32, (2 * ns, TQ), 1)
    blk_c = jnp.where(row_c < ns, 2 * row_c, 2 * (row_c - ns) + 1)
    mask_c = qpos_c >= blk_c * L_CMP + (L_CMP - 1)
    blk_t = lax.broadcasted_iota(jnp.int32, (nsp, TQ), 0)
    cur_t = (q0 + lax.broadcasted_iota(jnp.int32, (nsp, TQ), 1)) // SEL_BLOCK
    forced = (blk_t == 0) | ((blk_t <= cur_t) & (blk_t > cur_t - 2))
    kq_gap = (lax.broadcasted_iota(jnp.int32, (TK, TQ), 0) - lax.broadcasted_iota(jnp.int32, (TK, TQ), 1))

    oc_parts = []
    for g in range(KV_A):
        kh, kl = _split2(kc_s[:, g * HEAD_DIM:(g + 1) * HEAD_DIM])
        vcg = vct_s[g * HEAD_DIM:(g + 1) * HEAD_DIM, :].astype(BF16)
        imp = jnp.zeros((2 * ns, TQ), F32)
        for r in range(REP_A):
            h = g * REP_A + r
            qh, ql = _split2(qt[h * HEAD_DIM:(h + 1) * HEAD_DIM, :])
            s = (_dot(kh, qh) + _dot(kl, qh) + _dot(kh, ql)) * SM_SCALE + bct_ref[h]
            s = jnp.where(mask_c, s, NEG)
            e = jnp.exp(s - jnp.max(s, axis=0, keepdims=True))
            p = e / jnp.sum(e, axis=0, keepdims=True)
            p = jnp.where(mask_c, p, 0.0)
            oc_parts.append(_dot(vcg, p.astype(BF16)))
            imp = imp + p
        simp = imp[0:ns, :] + imp[ns:2 * ns, :]
        if nsp > ns:
            simp = jnp.concatenate([simp, jnp.zeros((nsp - ns, TQ), F32)], axis=0)
        score = jnp.where(forced, FORCE_SCORE, simp)
        score = jnp.where(blk_t <= cur_t, score, NEG)
        rank = _rank_select(score, blk_t, ns, 0)
        sel = jnp.where((rank < TOP_K) & (blk_t <= cur_t), 1.0, 0.0)
        sel = jnp.concatenate([sel, jnp.zeros((128 - nsp, TQ), F32)], axis=0).astype(BF16)
        def expand(c, _, g=g, sel=sel):
            r0 = pl.multiple_of(c * TK, TK)
            mm = (_dot(ett_ref[pl.ds(r0, TK), :], sel) - 1.0) * (-NEG)
            madd_s[g, c] = jnp.where(kq_gap + r0 > q0, NEG, mm)
            return 0

        lax.fori_loop(0, i + 1, expand, 0)
    oc_t = jnp.concatenate(oc_parts, axis=0)

    ok1, ok2 = i >= 1, i >= 2
    c1, c2 = jnp.maximum(i - 1, 0), jnp.maximum(i - 2, 0)
    n_far = jnp.maximum(i - 1, 0)
    qs = [(qt[h * HEAD_DIM:(h + 1) * HEAD_DIM, :] * (SM_SCALE * LOG2E)).astype(BF16) for h in range(H_A)]
    group = [h // REP_A for h in range(H_A)]

    def kv_sel(g, c):
        r0 = pl.multiple_of(c * TK, TK)
        return ks_s[g, pl.ds(r0, TK), :], vst_s[c, g * HEAD_DIM:(g + 1) * HEAD_DIM, :]

    def kv_win(g, c):
        r0 = pl.multiple_of(c * TK, TK)
        return kw_s[g, pl.ds(r0, TK), :], vwt_s[c, g * HEAD_DIM:(g + 1) * HEAD_DIM, :]

    def far(c, carry):
        kv = [kv_sel(g, c) for g in range(KV_A)]
        ma = [madd_s[g, c] for g in range(KV_A)]
        scores = [_dot(kv[group[h]][0], qs[h]) + ma[group[h]] for h in range(H_A)]
        return tuple(_softmax_steps(scores, [kv[group[h]][1] for h in range(H_A)], carry))

    init = tuple((jnp.full((1, TQ), NEG, F32), jnp.zeros((1, TQ), F32), jnp.zeros((HEAD_DIM, TQ), F32))
                 for _ in range(H_A))
    sel_st = list(lax.fori_loop(0, n_far, far, init))
    win_st = [None] * H_A
    for c, ok, d_idx in ((c1, ok1, 1), (i, None, 0)):
        kvs = [kv_sel(g, c) for g in range(KV_A)]
        kvw = [kv_win(g, c) for g in range(KV_A)]
        kcat = [jnp.concatenate([kvs[g][0], kvw[g][0]], axis=0) for g in range(KV_A)]
        ma = [madd_s[g, c] for g in range(KV_A)]
        s2 = [_dot(kcat[group[h]], qs[h]) for h in range(H_A)]
        scores, vts, states = [], [], []
        for h in range(H_A):
            bias = bnear_ref[h, d_idx]
            if ok is None:
                a_sel, a_win = ma[group[h]] + bias, jnp.where(kq_gap <= 0, bias, NEG)
            else:
                a_sel, a_win = jnp.where(ok, ma[group[h]] + bias, NEG), jnp.where(ok, bias, NEG)
            scores += [s2[h][0:TK] + a_sel, s2[h][TK:2 * TK] + a_win]
            vts += [kvs[group[h]][1], kvw[group[h]][1]]
            states += [sel_st[h], win_st[h]]
        new = _softmax_steps(scores, vts, states)
        sel_st, win_st = new[0::2], new[1::2]
    kvw = [kv_win(g, c2) for g in range(KV_A)]
    far_add = jnp.where((kq_gap > 0) & ok2, 0.0, NEG)
    win_st = _softmax_steps([_dot(kvw[group[h]][0], qs[h]) + far_add for h in range(H_A)],
                            [kvw[group[h]][1] for h in range(H_A)], win_st)
    os_t = jnp.concatenate([acc / l for (_, l, acc) in sel_st], axis=0)
    ow_t = jnp.concatenate([acc / l for (_, l, acc) in win_st], axis=0)

    ge = _dot_exact_lhs(e2t_ref[...], _sigmoid(gt_ref[...]))
    o_t = ge[0:W_A] * oc_t + ge[W_A:2 * W_A] * os_t + ge[2 * W_A:3 * W_A] * ow_t
    o_ref[...] = (o_t * _silu(zat_ref[...])).T


def _attn_prompt(proj_n, proj_t, wk, wvt, seg, bnear, bct, ett, e2t, b, t):
    ns, nt = t // SEL_BLOCK, t // TQ

    def tr_spec(rows, row0, width):
        if width == t:
            return pl.BlockSpec((None, rows, t), lambda bi, i: (bi, row0 // rows, 0))
        return pl.BlockSpec((None, rows, width), lambda bi, i: (bi, row0 // rows, i))

    def nat_spec(col0):
        return pl.BlockSpec((t, KV_COLS), lambda bi, i: (bi, col0 // KV_COLS))

    def const_spec(a):
        return pl.BlockSpec(a.shape, lambda bi, i: (0,) * a.ndim)

    in_specs = [tr_spec(W_A, TR_Q, TQ), tr_spec(W_A, TR_ZA, TQ), tr_spec(128, TR_GATE, TQ),
                tr_spec(KV_COLS, TR_KV + 1 * KV_COLS, t), tr_spec(KV_COLS, TR_KV + 3 * KV_COLS, t),
                tr_spec(KV_COLS, TR_KV + 5 * KV_COLS, t),
                nat_spec(NAT_KC), nat_spec(NAT_KS), nat_spec(NAT_KW),
                const_spec(wk), const_spec(wvt), const_spec(seg), const_spec(bnear),
                pl.BlockSpec((H_A, 2 * ns, TQ), lambda bi, i: (0, 0, i)),
                const_spec(ett), const_spec(e2t)]
    scratch = [pltpu.VMEM((2 * ns, KV_COLS), F32), pltpu.VMEM((KV_COLS, 2 * ns), F32),
               pltpu.VMEM((KV_A, t, HEAD_DIM), BF16), pltpu.VMEM((KV_A, t, HEAD_DIM), BF16),
               pltpu.VMEM((nt, KV_COLS, TK), BF16), pltpu.VMEM((nt, KV_COLS, TK), BF16),
               pltpu.VMEM((KV_A, nt, TK, TQ), F32)]
    return pl.pallas_call(
        functools.partial(_attn_prompt_kernel, ns=ns, nt=nt),
        grid=(b, nt),
        in_specs=in_specs,
        out_specs=pl.BlockSpec((TQ, W_A), lambda bi, i: (bi * nt + i, 0)),
        out_shape=jax.ShapeDtypeStruct((b * t, W_A), F32),
        scratch_shapes=scratch,
        compiler_params=_cparams(("arbitrary", "arbitrary")),
        name="attn_prompt",
    )(proj_t, proj_t, proj_t, proj_t, proj_t, proj_t, proj_n, proj_n, proj_n,
      wk, wvt, seg, bnear, bct, ett, e2t)


def _dcmp_kernel(pt_ref, *refs, pages):
    del pt_ref
    k_refs, v_refs = refs[0:pages], refs[pages:2 * pages]
    wkt_ref, wvt_ref, seg_ref = refs[2 * pages:2 * pages + 3]
    ko_ref, vo_ref = refs[2 * pages + 3:]
    for src, w_ref, o_ref in ((k_refs, wkt_ref, ko_ref), (v_refs, wvt_ref, vo_ref)):
        w = w_ref[...]
        acc = jnp.zeros(o_ref.shape, F32)
        for p in range(pages):
            hi, mid = _split2(src[p][...] * w)
            seg = seg_ref[p * PAGE:(p + 1) * PAGE, :]
            acc = acc + _dot(hi, seg) + _dot(mid, seg)
        o_ref[...] = acc


def _decode_compress(page_table, pool_k, pool_v, wkt, wvt, seg, pages):
    depth, db, n_pages = pool_k.shape[0], page_table.shape[0], page_table.shape[1]
    nc = n_pages // pages
    ncol = seg.shape[1]

    def page_spec(p):
        return pl.BlockSpec((None, None, KV_COLS, PAGE), lambda l, b, c, pt: (l, pt[b, c * pages + p], 0, 0))

    w_spec = pl.BlockSpec((None, KV_COLS, PAGE), lambda l, b, c, pt: (l, 0, 0))
    s_spec = pl.BlockSpec(seg.shape, lambda l, b, c, pt: (0, 0))
    o_spec = pl.BlockSpec((None, None, None, KV_COLS, ncol), lambda l, b, c, pt: (l, b, c, 0, 0))
    o_shape = jax.ShapeDtypeStruct((depth, db, nc, KV_COLS, ncol), F32)
    return pl.pallas_call(
        functools.partial(_dcmp_kernel, pages=pages),
        grid_spec=pltpu.PrefetchScalarGridSpec(
            num_scalar_prefetch=1,
            grid=(depth, db, nc),
            in_specs=[page_spec(p) for p in range(pages)] * 2 + [w_spec, w_spec, s_spec],
            out_specs=[o_spec] * 2),
        out_shape=[o_shape] * 2,
        compiler_params=_cparams(("arbitrary", "arbitrary", "arbitrary")),
        name="decode_compress",
    )(page_table, *([pool_k] * pages), *([pool_v] * pages), wkt, wvt, seg)


def _softmax_rows(qbd, kt, add, m, l):
    s = _dot(qbd, kt) + add
    m_new = jnp.maximum(m, jnp.max(s, axis=-1, keepdims=True))
    alpha = jnp.exp2(m - m_new)
    p = jnp.exp2(s - m_new)
    return m_new, alpha, alpha * l + jnp.sum(p, axis=-1, keepdims=True), p.astype(BF16)


def _attn_decode_kernel(pt_ref, *refs, pages, nps, nc, dt):
    del pt_ref
    (q_ref, za_ref, kv_ref, gate_ref, kct_ref, vct_ref) = refs[0:6]
    sk_refs, sv_refs = refs[6:6 + pages], refs[6 + pages:6 + 2 * pages]
    (wink_ref, winv_ref, bcd_ref, blast_ref, bnew_ref, bwin_ref, e_ref, e2_ref,
     o_ref, wk_o, wv_o, qbd_s, m_s, l_s, acc_s, madd_s, oc_s) = refs[6 + 2 * pages:]
    c = pl.program_id(1)
    ck = pages * PAGE
    rows = KV_A * REP_A * dt
    zpad = jnp.zeros((dt, HEAD_DIM), F32)

    @pl.when(c == 0)
    def _():
        q = q_ref[...]
        blocks = []
        for g in range(KV_A):
            for r in range(REP_A):
                h = g * REP_A + r
                piece = q[:, h * HEAD_DIM:(h + 1) * HEAD_DIM]
                blocks.append(jnp.concatenate([piece, zpad] if g == 0 else [zpad, piece], axis=1))
        qbd = jnp.concatenate(blocks, axis=0)
        qbd_s[...] = qbd
        s = _dot_f32(qbd, kct_ref[...]) * SM_SCALE + bcd_ref[...]
        e = jnp.exp(s - jnp.max(s, axis=-1, keepdims=True))
        p = e / jnp.sum(e, axis=-1, keepdims=True)
        oc_s[...] = _dot_nt(p.astype(BF16), vct_ref[...].astype(BF16))
        scores = []
        for g in range(KV_A):
            pe = [p[(g * REP_A + r) * dt:(g * REP_A + r + 1) * dt, 0:nps] for r in range(REP_A)]
            po = [p[(g * REP_A + r) * dt:(g * REP_A + r + 1) * dt, nps:2 * nps] for r in range(REP_A)]
            scores.append((pe[0] + pe[1] + pe[2] + pe[3]) + (po[0] + po[1] + po[2] + po[3]))
        simp = jnp.concatenate(scores, axis=0)
        blk = lax.broadcasted_iota(jnp.int32, simp.shape, 1)
        score = jnp.where((blk == 0) | (blk >= nps - 1), FORCE_SCORE, simp)
        rank = _rank_select(score, blk, nps, 1)
        sel = jnp.where(rank < TOP_K - 1, 1.0, 0.0).astype(BF16)
        madd = (_dot(sel, e_ref[...]) - 1.0) * (-NEG)
        madd = jnp.concatenate([madd[0:dt]] * REP_A + [madd[dt:2 * dt]] * REP_A, axis=0)
        for j in range(nc):
            tile = madd[:, j * ck:(j + 1) * ck]
            if j == nc - 1:
                tile = jnp.concatenate([tile[:, :ck - PAGE], tile[:, ck - PAGE:] + blast_ref[...]], axis=1)
            madd_s[j] = tile
        m_s[...] = jnp.full(m_s.shape, NEG, F32)
        l_s[...] = jnp.zeros(l_s.shape, F32)
        acc_s[...] = jnp.zeros(acc_s.shape, F32)

    qbd = (qbd_s[...] * (SM_SCALE * LOG2E)).astype(BF16)
    kt = jnp.concatenate([r[...].astype(BF16) for r in sk_refs], axis=1)
    vt = jnp.concatenate([r[...].astype(BF16) for r in sv_refs], axis=1)
    m, alpha, l, p = _softmax_rows(qbd, kt, madd_s[c], m_s[...], l_s[...])
    acc = alpha * acc_s[...] + _dot_nt(p, vt)
    m_s[...], l_s[...], acc_s[...] = m, l, acc

    @pl.when(c == nc - 1)
    def _():
        row_t = lax.broadcasted_iota(jnp.int32, (rows, 128), 0) % dt
        col = lax.broadcasted_iota(jnp.int32, (rows, 128), 1)
        new_add = jnp.where((col <= row_t) & (col < dt), bnew_ref[...], NEG)
        row_w = lax.broadcasted_iota(jnp.int32, (rows, WINDOW), 0) % dt
        col_w = lax.broadcasted_iota(jnp.int32, (rows, WINDOW), 1)
        win_add = jnp.where(col_w > row_w, bwin_ref[...], NEG)
        pad = jnp.zeros((128 - dt, KV_COLS), F32)

        def new_tile(src):
            return jnp.concatenate([kv_ref[:, src * KV_COLS:(src + 1) * KV_COLS], pad], axis=0).astype(BF16)

        def attend_nat(kn, vn, add, m, l, acc):
            s = _dot_nt(qbd, kn) + add
            m_new = jnp.maximum(m, jnp.max(s, axis=-1, keepdims=True))
            alpha = jnp.exp2(m - m_new)
            p = jnp.exp2(s - m_new)
            return m_new, alpha * l + jnp.sum(p, axis=-1, keepdims=True), alpha * acc + _dot(p.astype(BF16), vn)

        m2, l2, acc2 = attend_nat(new_tile(2), new_tile(3), new_add, m, l, acc)
        os_full = acc2 / l2
        init = (jnp.full((rows, 1), NEG, F32), jnp.zeros((rows, 1), F32), jnp.zeros((rows, KV_COLS), F32))
        carry = attend_nat(wink_ref[...].astype(BF16), winv_ref[...].astype(BF16), win_add, *init)
        m3, l3, acc3 = attend_nat(new_tile(4), new_tile(5), new_add, *carry)
        ow_full = acc3 / l3

        def heads(full):
            parts = []
            for g in range(KV_A):
                for r in range(REP_A):
                    r0 = (g * REP_A + r) * dt
                    parts.append(full[r0:r0 + dt, g * HEAD_DIM:(g + 1) * HEAD_DIM])
            return jnp.concatenate(parts, axis=1)

        sg = _sigmoid(gate_ref[...])
        ge = _dot_exact_rhs(sg, e2_ref[...])
        o = ge[:, 0:W_A] * heads(oc_s[...]) + ge[:, W_A:2 * W_A] * heads(os_full) + ge[:, 2 * W_A:] * heads(ow_full)
        o_ref[...] = o * _silu(za_ref[...])
        wk_o[...] = jnp.concatenate([wink_ref[dt:, :], kv_ref[:, 4 * KV_COLS:5 * KV_COLS]], axis=0)
        wv_o[...] = jnp.concatenate([winv_ref[dt:, :], kv_ref[:, 5 * KV_COLS:6 * KV_COLS]], axis=0)


def _attn_decode(layer, page_table, proj, kct, vct, pool_sk, pool_sv, win_k, win_v, tabs, e_sel, e2, pages, dt):
    db, n_pages = page_table.shape
    nps = n_pages * (PAGE // SEL_BLOCK)
    nc = n_pages // pages
    ck = pages * PAGE
    rows = KV_A * REP_A * dt
    bcd, blast, bnew, bwin = tabs

    def row_spec(width, col):
        return pl.BlockSpec((dt, width), lambda b, c, pt: (b, col // width))

    def page_spec(p):
        return pl.BlockSpec((None, None, KV_COLS, PAGE), lambda b, c, pt: (layer, pt[b, c * pages + p], 0, 0))

    cmp_spec = pl.BlockSpec((None, None, KV_COLS, 2 * nps), lambda b, c, pt: (layer, b, 0, 0))
    win_spec = pl.BlockSpec((None, None, WINDOW, KV_COLS), lambda b, c, pt: (layer, b, 0, 0))

    def const_spec(a):
        return pl.BlockSpec(a.shape, lambda b, c, pt: (0,) * a.ndim)

    in_specs = ([row_spec(W_A, COL_Q), row_spec(W_A, COL_ZA), row_spec(6 * KV_COLS, COL_KV), row_spec(128, COL_GATE)]
                + [cmp_spec] * 2 + [page_spec(p) for p in range(pages)] * 2 + [win_spec] * 2
                + [const_spec(a) for a in (bcd, blast, bnew, bwin, e_sel, e2)])
    out_specs = [pl.BlockSpec((dt, W_A), lambda b, c, pt: (b, 0)),
                 pl.BlockSpec((None, WINDOW, KV_COLS), lambda b, c, pt: (b, 0, 0)),
                 pl.BlockSpec((None, WINDOW, KV_COLS), lambda b, c, pt: (b, 0, 0))]
    out_shape = [jax.ShapeDtypeStruct((db * dt, W_A), F32),
                 jax.ShapeDtypeStruct((db, WINDOW, KV_COLS), F32),
                 jax.ShapeDtypeStruct((db, WINDOW, KV_COLS), F32)]
    scratch = [pltpu.VMEM((rows, KV_COLS), F32), pltpu.VMEM((rows, 1), F32), pltpu.VMEM((rows, 1), F32),
               pltpu.VMEM((rows, KV_COLS), F32), pltpu.VMEM((nc, rows, ck), F32), pltpu.VMEM((rows, KV_COLS), F32)]
    return pl.pallas_call(
        functools.partial(_attn_decode_kernel, pages=pages, nps=nps, nc=nc, dt=dt),
        grid_spec=pltpu.PrefetchScalarGridSpec(
            num_scalar_prefetch=1, grid=(db, nc), in_specs=in_specs, out_specs=out_specs, scratch_shapes=scratch),
        out_shape=out_shape,
        compiler_params=_cparams(("arbitrary", "arbitrary")),
        name="attn_decode",
    )(page_table, proj, proj, proj, proj, kct, vct, *([pool_sk] * pages), *([pool_sv] * pages),
      win_k, win_v, bcd, blast, bnew, bwin, e_sel, e2)


EXT_B0 = 32
EXT_C0 = 8
CONV_ROWS = 32


def _mixer_kernel(glu_ref, zb_ref, xc_ref, zc_ref, bufb_ref, bufc_ref, h0_ref,
                  cbw_ref, cbb_ref, gng_ref, gnb_ref, wpw_ref, gones_ref,
                  ccw_ref, ccb_ref, wa_ref, ba_ref, wx_ref, bx_ref, lam_ref,
                  bo_ref, co_ref, cbs_ref, ccs_ref, hs_ref,
                  extb, extc, hcar, cbuf, *, tt):
    j = pl.program_id(1)
    nb, nc = CONV_B - 1, CONV_C - 1

    @pl.when(j == 0)
    def _():
        extb[EXT_B0 - nb:EXT_B0, :] = bufb_ref[...]
        extc[EXT_C0 - nc:EXT_C0, :] = bufc_ref[...]
        hcar[...] = h0_ref[...]

    glu = glu_ref[...]
    extb[EXT_B0:EXT_B0 + tt, :] = glu[:, 0:W_B] * _sigmoid(glu[:, W_B:2 * W_B])
    step = min(CONV_ROWS, tt)
    for r0 in range(0, tt, step):
        acc = jnp.zeros((step, W_B), F32)
        for k in range(CONV_B):
            lo = EXT_B0 - nb + k + r0
            acc = acc + cbw_ref[k:k + 1, :] * extb[lo:lo + step, :]
        cbuf[r0:r0 + step, :] = acc + cbb_ref[...]
    cv = cbuf[...]
    gones = gones_ref[...]
    mu = _dot_exact_rhs(cv, gones) * (1.0 / GN_GROUP)
    d = cv - mu
    var = _dot_exact_rhs(d * d, gones) * (1.0 / GN_GROUP)
    cn = d * lax.rsqrt(var + EPS) * gng_ref[...] + gnb_ref[...]
    bo_ref[...] = _dot(_silu(cn).astype(BF16), wpw_ref[...]) * _silu(zb_ref[...])

    extc[EXT_C0:EXT_C0 + tt, :] = xc_ref[...]
    u = jnp.zeros((tt, W_C), F32)
    for k in range(CONV_C):
        lo = EXT_C0 - nc + k
        u = u + ccw_ref[k:k + 1, :] * extc[lo:lo + tt, :]
    u = u + ccb_ref[...]
    ub = u.astype(BF16)
    r = _sigmoid(_dot(ub, wa_ref[...]) + ba_ref[...])
    ig = _sigmoid(_dot(ub, wx_ref[...]) + bx_ref[...])
    nl = -lam_ref[...]
    softplus = jnp.maximum(nl, 0.0) + jnp.log1p(jnp.exp(-jnp.abs(nl)))
    log_a = -LRU_C * r * softplus
    a = jnp.exp(log_a)
    b = jnp.sqrt(-_expm1(2.0 * log_a)) * (ig * u)
    row = lax.broadcasted_iota(jnp.int32, (tt, W_C), 0)
    s = 1
    while s < tt:
        a_sh = jnp.where(row < s, 1.0, pltpu.roll(a, s, axis=0))
        b_sh = jnp.where(row < s, 0.0, pltpu.roll(b, s, axis=0))
        b = a * b_sh + b
        a = a * a_sh
        s *= 2
    hh = a * hcar[...] + b
    co_ref[...] = hh * _silu(zc_ref[...])
    hcar[...] = hh[tt - 1:tt, :]

    new_b = extb[EXT_B0 + tt - nb:EXT_B0 + tt, :]
    new_c = extc[EXT_C0 + tt - nc:EXT_C0 + tt, :]
    extb[EXT_B0 - nb:EXT_B0, :] = new_b
    extc[EXT_C0 - nc:EXT_C0, :] = new_c

    @pl.when(j == pl.num_programs(1) - 1)
    def _():
        cbs_ref[...] = new_b
        ccs_ref[...] = new_c
        hs_ref[...] = hh[tt - 1:tt, :]


def _mixers(proj, cols, bufb, bufc, h0, lw, gones, b, t, tt):
    nj = t // tt
    c_glu, c_zb, c_xc, c_zc = cols

    def row_spec(width, col):
        return pl.BlockSpec((tt, width), lambda bi, j: (bi * nj + j, col // width))

    def st_spec(n, w):
        return pl.BlockSpec((None, n, w), lambda bi, j: (bi, 0, 0))

    def const_spec(a):
        return pl.BlockSpec(a.shape, lambda bi, j: (0,) * a.ndim)

    consts = [lw["cbw"], lw["cbb"], lw["gng"], lw["gnb"], lw["wpw"], gones,
              lw["ccw"], lw["ccb"], lw["wa"], lw["ba"], lw["wx"], lw["bx"], lw["lam"]]
    in_specs = ([row_spec(2 * W_B, c_glu), row_spec(W_B, c_zb), row_spec(W_C, c_xc), row_spec(W_C, c_zc),
                 st_spec(CONV_B - 1, W_B), st_spec(CONV_C - 1, W_C), st_spec(1, W_C)]
                + [const_spec(a) for a in consts])
    out_specs = [pl.BlockSpec((tt, W_B), lambda bi, j: (bi * nj + j, 0)),
                 pl.BlockSpec((tt, W_C), lambda bi, j: (bi * nj + j, 0)),
                 st_spec(CONV_B - 1, W_B), st_spec(CONV_C - 1, W_C), st_spec(1, W_C)]
    out_shape = [jax.ShapeDtypeStruct((b * t, W_B), F32), jax.ShapeDtypeStruct((b * t, W_C), F32),
                 jax.ShapeDtypeStruct((b, CONV_B - 1, W_B), F32), jax.ShapeDtypeStruct((b, CONV_C - 1, W_C), F32),
                 jax.ShapeDtypeStruct((b, 1, W_C), F32)]
    scratch = [pltpu.VMEM((EXT_B0 + tt, W_B), F32), pltpu.VMEM((EXT_C0 + tt, W_C), F32),
               pltpu.VMEM((1, W_C), F32), pltpu.VMEM((tt, W_B), F32)]
    return pl.pallas_call(
        functools.partial(_mixer_kernel, tt=tt),
        grid=(b, nj),
        in_specs=in_specs, out_specs=out_specs, out_shape=out_shape, scratch_shapes=scratch,
        compiler_params=_cparams(("arbitrary", "arbitrary")),
        name="mixers",
    )(proj, proj, proj, proj, bufb, bufc, h0, *consts)


def _block_diag(w):
    nblk, c, d = w.shape
    eye = jnp.eye(nblk, dtype=w.dtype)
    return (eye[:, None, :, None] * w[:, :, None, :]).reshape(nblk * c, nblk * d)


def _row_tile(m, cap, step=8):
    t = min(m, cap)
    while m % t:
        t -= step
    return t


def kernel(x_prompt, x_sample, cache_cmp_k, cache_cmp_v, cache_sel_k, cache_sel_v, cache_win_k, cache_win_v,
           state_conv_b, state_conv_c, state_rglru, page_table, rel_bias, g_pre, g_post, w_in, w_out,
           w_cmp_k, w_cmp_v, conv_b_w, conv_b_b, gn_gain, gn_bias, w_pw_b, conv_c_w, conv_c_b,
           w_lru_a, b_lru_a, w_lru_x, b_lru_x, lru_lambda):
    depth = w_in.shape[0]
    b, t, _ = x_prompt.shape
    db, dt, _ = x_sample.shape
    n_pages = page_table.shape[1]
    past = n_pages * PAGE
    nps = past // SEL_BLOCK
    ns, nt = t // SEL_BLOCK, t // TQ
    assert t % TQ == 0 and TQ == TK and WINDOW == 2 * TK and TOP_K <= ns <= 128 and TOP_K < nps <= 128
    assert dt <= 8 and (past + dt) // L_CMP == past // L_CMP and cache_win_k.shape[2] == WINDOW
    pages = min(16, n_pages)
    assert n_pages % pages == 0
    nc = n_pages // pages
    rows = KV_A * REP_A * dt

    zpad = jnp.zeros((depth, D_MODEL, D_IN_PAD - D_IN), w_in.dtype)
    w_r = jnp.concatenate(
        [w_in[..., 0:512], w_in[..., 1304:1816], w_in[..., 1816:2328], w_in[..., 512:1280],
         w_in[..., 2328:3096], w_in[..., 1280:1304], zpad], axis=-1).astype(BF16)
    w_n = jnp.concatenate(
        [w_in[..., 1816:3096], w_in[..., 512:640], w_in[..., 768:896], w_in[..., 1024:1152]], axis=-1).astype(BF16)
    w_it = jnp.swapaxes(w_in, 1, 2)
    w_t = jnp.concatenate(
        [w_it[:, 0:512], w_it[:, 1304:1816], w_it[:, 512:1280], w_it[:, 1280:1304],
         jnp.zeros((depth, N_TR - 1816, D_MODEL), w_in.dtype)], axis=1).astype(BF16)
    w_o = w_out.astype(BF16)
    wck = jnp.tile(w_cmp_k, (1, 1, KV_A))
    wvt_p = jnp.tile(jnp.swapaxes(w_cmp_v, 1, 2), (1, KV_A, t // L_CMP))
    wkt_d = jnp.tile(jnp.swapaxes(w_cmp_k, 1, 2), (1, KV_A, PAGE // L_CMP))
    wvt_d = jnp.tile(jnp.swapaxes(w_cmp_v, 1, 2), (1, KV_A, PAGE // L_CMP))
    gones = jnp.asarray(_group_ones_np(), BF16)
    e2 = jnp.asarray(_gate_expand_np(), BF16)
    e2t = jnp.asarray(_gate_expand_np().T, BF16)
    ett = jnp.asarray(_expand_np(ns, 128).T, BF16)
    e_d = jnp.asarray(_expand_np(nps, nps), BF16)
    seg_p = jnp.asarray(_segment_np(t, ns), BF16)
    seg_d = jnp.asarray(_segment_np(pages * PAGE, pages * PAGE // SEL_BLOCK), BF16)

    kk, qq = np.arange(TK)[:, None], np.arange(TQ)[None, :]
    bk_near = np.concatenate([_bucket_np(d0 + qq - kk) for d0 in (0, TQ)], axis=0)
    bnear = _bias_lookup(rel_bias, bk_near, shift=True, scale=LOG2E).reshape(H_A, 2, TK, TQ)
    blk_eo = np.concatenate([2 * np.arange(ns), 2 * np.arange(ns) + 1])[:, None]
    bct = _bias_lookup(rel_bias, _bucket_np(np.arange(t)[None, :] - (blk_eo * L_CMP + L_CMP - 1)))
    qd = past + np.arange(dt)[:, None]
    blk_d = np.concatenate([2 * np.arange(nps), 2 * np.arange(nps) + 1])[None, :]
    bcd = _bias_lookup(rel_bias, _bucket_np(qd - (blk_d * L_CMP + L_CMP - 1))).reshape(rows, 2 * nps)
    blast = _bias_lookup(rel_bias, _bucket_np(qd - (past - PAGE + np.arange(PAGE))[None, :]), shift=True, scale=LOG2E)
    bnew = _bias_lookup(rel_bias, _bucket_np(np.arange(dt)[:, None] - np.arange(128)[None, :]), shift=True, scale=LOG2E)
    bwin = _bias_lookup(rel_bias, _bucket_np(WINDOW + np.arange(dt)[:, None] - np.arange(WINDOW)[None, :]),
                        shift=True, scale=LOG2E)
    dtabs = (bcd, blast.reshape(rows, PAGE), bnew.reshape(rows, 128), bwin.reshape(rows, WINDOW))

    pool = lambda a: jnp.transpose(a, (0, 1, 3, 4, 2)).reshape(a.shape[0], a.shape[1], KV_COLS, PAGE)
    kc_ch, vc_ch = _decode_compress(page_table, pool(cache_cmp_k), pool(cache_cmp_v), wkt_d, wvt_d, seg_d, pages)

    def eo_table(x):
        x = x.reshape(depth, db, nc, KV_COLS, 2, nps // nc).transpose(0, 1, 3, 4, 2, 5)
        return x.reshape(depth, db, KV_COLS, 2 * nps)

    kct, vct = eo_table(kc_ch), eo_table(vc_ch)
    pool_sk, pool_sv = pool(cache_sel_k), pool(cache_sel_v)
    win_k = cache_win_k.reshape(depth, db, WINDOW, KV_COLS)
    win_v = cache_win_v.reshape(depth, db, WINDOW, KV_COLS)

    hp = x_prompt.reshape(b * t, D_MODEL)
    hs = x_sample.reshape(db * dt, D_MODEL)
    zeros_b = jnp.zeros((b, CONV_B - 1, W_B), F32)
    zeros_c = jnp.zeros((b, CONV_C - 1, W_C), F32)
    zeros_h = jnp.zeros((b, 1, W_C), F32)
    tm_p, tm_s = _row_tile(t, 512, 128), _row_tile(db * dt, 512)
    tt_p = _row_tile(t, 256)
    nat_cols = (NAT_GLU, NAT_ZB, NAT_XC, NAT_ZC)
    dec_cols = (COL_GLU, COL_ZB, COL_XC, COL_ZC)

    p_states, s_states = [], []
    for l in range(depth):
        row = lambda a: a[l][None, :]
        lw = dict(cbw=conv_b_w[l], cbb=row(conv_b_b), gng=row(gn_gain), gnb=row(gn_bias), wpw=w_pw_b[l].astype(BF16),
                  ccw=conv_c_w[l], ccb=row(conv_c_b), wa=_block_diag(w_lru_a[l]).astype(BF16), ba=row(b_lru_a),
                  wx=_block_diag(w_lru_x[l]).astype(BF16), bx=row(b_lru_x), lam=row(lru_lambda))
        proj_n, proj_t = _project_prompt(hp, row(g_pre), w_n[l], w_t[l], b, t, tm_p)
        a_out = _attn_prompt(proj_n, proj_t, wck[l], wvt_p[l], seg_p, bnear, bct, ett, e2t, b, t)
        b_out, c_out, cb, cc, hc = _mixers(proj_n, nat_cols, zeros_b, zeros_c, zeros_h, lw, gones, b, t, tt_p)
        hp = _output(hp, a_out, b_out, c_out, w_o[l], row(g_post), tm_p)
        kvt = proj_t[:, TR_KV:TR_KV + 6 * KV_COLS, :].reshape(b, 6, KV_A, HEAD_DIM, t)
        p_states.append(tuple(kvt[:, k] for k in range(4))
                        + (kvt[:, 4, :, :, t - WINDOW:], kvt[:, 5, :, :, t - WINDOW:], cb, cc, hc[:, 0]))
        proj = _project(hs, row(g_pre), w_r[l], tm_s)
        a_out, wk_n, wv_n = _attn_decode(l, page_table, proj, kct, vct, pool_sk, pool_sv, win_k, win_v,
                                         dtabs, e_d, e2, pages, dt)
        b_out, c_out, cb, cc, hc = _mixers(proj, dec_cols, state_conv_b[l], state_conv_c[l],
                                           state_rglru[l][:, None, :], lw, gones, db, dt, dt)
        hs = _output(hs, a_out, b_out, c_out, w_o[l], row(g_post), tm_s)
        kv = proj[:, COL_KV:COL_KV + 6 * KV_COLS].reshape(db, dt, 6, KV_A, HEAD_DIM)
        s_states.append((kv[:, :, 0], kv[:, :, 1], kv[:, :, 2], kv[:, :, 3],
                         wk_n.reshape(db, WINDOW, KV_A, HEAD_DIM), wv_n.reshape(db, WINDOW, KV_A, HEAD_DIM),
                         cb, cc, hc[:, 0]))

    ck_p, cv_p, sk_p, sv_p, wk_p, wv_p, cb_p, cc_p, h_p = [jnp.stack(a) for a in zip(*p_states)]
    ck_p, cv_p, sk_p, sv_p, wk_p, wv_p = [jnp.transpose(a, (0, 1, 4, 2, 3)) for a in (ck_p, cv_p, sk_p, sv_p, wk_p, wv_p)]
    ck_s, cv_s, sk_s, sv_s, wk_s, wv_s, cb_s, cc_s, h_s = [jnp.stack(a) for a in zip(*s_states)]
    return (hp.reshape(b, t, D_MODEL), hs.reshape(db, dt, D_MODEL),
            ck_p, ck_s, cv_p, cv_s, sk_p, sk_s, sv_p, sv_s, wk_p, wk_s, wv_p, wv_s,
            cb_p, cb_s, cc_p, cc_s, h_p, h_s)
```

```python
import functools
import math

import numpy as np
import jax
import jax.numpy as jnp
from jax import lax
from jax.experimental import pallas as pl
from jax.experimental.pallas import tpu as pltpu

F32 = jnp.float32
BF16 = jnp.bfloat16

D_MODEL = 1024
HEAD_DIM = 64
W_A = D_MODEL // 2
W_B = D_MODEL // 4
W_C = D_MODEL // 4
H_A = W_A // HEAD_DIM
KV_A = 2
REP_A = H_A // KV_A
KV_COLS = KV_A * HEAD_DIM
L_CMP = 32
SEL_BLOCK = 64
TOP_K = 16
FORCE_SCORE = 1.0e4
WINDOW = 512
PAGE = 128
CONV_B = 31
CONV_C = 4
GN_GROUP = W_B // 4
LRU_C = 8.0
NUM_BUCKETS = 32
MAX_DISTANCE = 128
SM_SCALE = HEAD_DIM ** -0.5
LOG2E = 1.4426950408889634
EPS = 1e-6
NEG = -1e30

TQ = 256
TK = 256
D_IN = 3096
D_IN_PAD = 3200
COL_Q, COL_ZA, COL_GLU, COL_KV, COL_ZB, COL_XC, COL_ZC, COL_GATE = 0, 512, 1024, 1536, 2304, 2560, 2816, 3072
N_NAT = 1664
NAT_GLU, NAT_ZB, NAT_XC, NAT_ZC, NAT_KC, NAT_KS, NAT_KW = 0, 512, 768, 1024, 1280, 1408, 1536
N_TR = 1152
TR_Q, TR_ZA, TR_GATE = 0, 512, 1024
N_KV = 6
VMEM_LIMIT = 56 * 1024 * 1024


def _cparams(sem):
    return pltpu.CompilerParams(dimension_semantics=sem, vmem_limit_bytes=VMEM_LIMIT)


def _bucket_np(dist):
    n = np.maximum(dist, 0)
    max_exact = NUM_BUCKETS // 2
    nf = np.maximum(n, 1).astype(np.float32)
    large = max_exact + (np.log(nf / np.float32(max_exact)) / np.float32(math.log(MAX_DISTANCE / max_exact))
                         * np.float32(NUM_BUCKETS - max_exact)).astype(np.int32)
    return np.where(n < max_exact, n, np.minimum(large, NUM_BUCKETS - 1)).astype(np.int32)


def _expand_np(n_blocks, rows):
    e = np.zeros((rows, n_blocks * SEL_BLOCK), np.float32)
    for m in range(n_blocks):
        e[m, m * SEL_BLOCK:(m + 1) * SEL_BLOCK] = 1.0
    return e


def _gate_expand_np():
    e = np.zeros((128, 3 * W_A), np.float32)
    for br in range(3):
        for h in range(H_A):
            e[br * H_A + h, br * W_A + h * HEAD_DIM: br * W_A + (h + 1) * HEAD_DIM] = 1.0
    return e


def _group_ones_np():
    g = np.zeros((W_B, W_B), np.float32)
    for k in range(W_B // GN_GROUP):
        g[k * GN_GROUP:(k + 1) * GN_GROUP, k * GN_GROUP:(k + 1) * GN_GROUP] = 1.0
    return g


def _segment_np(n_rows, n_half):
    s = np.zeros((n_rows, 2 * n_half), np.float32)
    blk = np.arange(n_rows) // L_CMP
    s[np.arange(n_rows), np.where(blk % 2 == 0, blk // 2, n_half + blk // 2)] = 1.0
    return s


def _split2(a):
    hi = a.astype(BF16)
    return hi, (a - hi.astype(F32)).astype(BF16)


def _split3(a):
    hi = a.astype(BF16)
    r1 = a - hi.astype(F32)
    mid = r1.astype(BF16)
    return hi, mid, (r1 - mid.astype(F32)).astype(BF16)


def _dot(a, b):
    return jnp.dot(a, b, preferred_element_type=F32)


def _dot_nt(a, b):
    return lax.dot_general(a, b, (((1,), (1,)), ((), ())), preferred_element_type=F32)


def _dot_exact_rhs(a, b_bf16):
    hi, mid, lo = _split3(a)
    return _dot(hi, b_bf16) + _dot(mid, b_bf16) + _dot(lo, b_bf16)


def _dot_exact_lhs(a_bf16, b):
    hi, mid, lo = _split3(b)
    return _dot(a_bf16, hi) + _dot(a_bf16, mid) + _dot(a_bf16, lo)


def _dot_f32(a, b):
    ah, al = _split2(a)
    bh, bl = _split2(b)
    return _dot(ah, bh) + _dot(al, bh) + _dot(ah, bl)


def _sigmoid(x):
    return 1.0 / (1.0 + jnp.exp(-x))


def _silu(x):
    return x * _sigmoid(x)


def _expm1(x):
    u = jnp.exp(x)
    safe = jnp.where((u == 1.0) | (u == 0.0), 0.5, u)
    return jnp.where(u == 1.0, x, jnp.where(u == 0.0, -1.0, (safe - 1.0) * x / jnp.log(safe)))


def _rank_select(score, idx, n, axis):
    rank = jnp.zeros(score.shape, F32)
    for m in range(n):
        sm = score[m:m + 1, :] if axis == 0 else score[:, m:m + 1]
        beats = (sm > score) | ((sm == score) & (idx > m))
        rank = rank + jnp.where(beats, 1.0, 0.0)
    return rank


def _bias_kernel(rb_ref, bk_ref, o_ref, *, shift, scale):
    h = pl.program_id(0)
    bk = bk_ref[...]
    acc = jnp.zeros(bk.shape, F32)
    for b in range(NUM_BUCKETS):
        acc = jnp.where(bk == b, rb_ref[b, h], acc)
    if shift:
        acc = acc - rb_ref[NUM_BUCKETS - 1, h]
    o_ref[...] = acc * scale


def _bias_lookup(rel_bias, bucket, shift=False, scale=1.0):
    r, c = bucket.shape
    return pl.pallas_call(
        functools.partial(_bias_kernel, shift=shift, scale=scale),
        grid=(H_A,),
        in_specs=[pl.BlockSpec(memory_space=pltpu.SMEM),
                  pl.BlockSpec((r, c), lambda h: (0, 0))],
        out_specs=pl.BlockSpec((None, r, c), lambda h: (h, 0, 0)),
        out_shape=jax.ShapeDtypeStruct((H_A, r, c), F32),
        compiler_params=_cparams(("arbitrary",)),
        name="bias_lookup",
    )(rel_bias, jnp.asarray(bucket))


def _rms_bf16(x_ref, g_ref):
    x = x_ref[...]
    ms = jnp.mean(x * x, axis=-1, keepdims=True)
    return (x * lax.rsqrt(ms + EPS) * g_ref[...]).astype(BF16)


def _proj_kernel(x_ref, g_ref, w_ref, o_ref):
    u = _rms_bf16(x_ref, g_ref)
    step = 640
    for c in range(0, o_ref.shape[1], step):
        o_ref[:, c:c + step] = _dot(u, w_ref[:, c:c + step])


def _project(h, g, w, tm):
    m = h.shape[0]
    return pl.pallas_call(
        _proj_kernel,
        grid=(m // tm,),
        in_specs=[pl.BlockSpec((tm, D_MODEL), lambda i: (i, 0)),
                  pl.BlockSpec((1, D_MODEL), lambda i: (0, 0)),
                  pl.BlockSpec((D_MODEL, D_IN_PAD), lambda i: (0, 0))],
        out_specs=pl.BlockSpec((tm, D_IN_PAD), lambda i: (i, 0)),
        out_shape=jax.ShapeDtypeStruct((m, D_IN_PAD), F32),
        compiler_params=_cparams(("arbitrary",)),
        name="project",
    )(h, g, w)


def _proj2_kernel(x_ref, g_ref, wn_ref, wt_ref, wkv_ref, *refs):
    on_ref, ot_ref = refs[N_KV], refs[N_KV + 1]
    st_refs = refs[N_KV + 2:]
    u = _rms_bf16(x_ref, g_ref)
    for c0, c1 in ((0, 768), (768, N_NAT)):
        on_ref[:, c0:c1] = _dot(u, wn_ref[:, c0:c1])
    for c in range(0, N_TR, 384):
        ot_ref[c:c + 384, :] = _dot_nt(wt_ref[c:c + 384, :], u)
    for k in range(0, N_KV, 2):
        kv = _dot_nt(wkv_ref[k * KV_COLS:(k + 2) * KV_COLS, :], u)
        st_refs[k][...] = kv[0:KV_COLS]
        st_refs[k + 1][...] = kv[KV_COLS:2 * KV_COLS]


def _project_prompt(h, g, wn, wt, wkv, states, layer, b, t, tm):
    per = t // tm
    st_spec = pl.BlockSpec((None, None, KV_COLS, tm), lambda i: (layer, i // per, 0, i % per))
    outs = pl.pallas_call(
        _proj2_kernel,
        grid=(b * per,),
        in_specs=[pl.BlockSpec((tm, D_MODEL), lambda i: (i, 0)),
                  pl.BlockSpec((1, D_MODEL), lambda i: (0, 0)),
                  pl.BlockSpec((D_MODEL, N_NAT), lambda i: (0, 0)),
                  pl.BlockSpec((N_TR, D_MODEL), lambda i: (0, 0)),
                  pl.BlockSpec((N_KV * KV_COLS, D_MODEL), lambda i: (0, 0))]
                 + [pl.BlockSpec(memory_space=pl.ANY)] * N_KV,
        out_specs=[pl.BlockSpec((tm, N_NAT), lambda i: (i, 0)),
                   pl.BlockSpec((None, N_TR, tm), lambda i: (i // per, 0, i % per))] + [st_spec] * N_KV,
        out_shape=[jax.ShapeDtypeStruct((b * t, N_NAT), F32), jax.ShapeDtypeStruct((b, N_TR, t), F32)]
                  + [jax.ShapeDtypeStruct(a.shape, a.dtype) for a in states],
        input_output_aliases={5 + k: 2 + k for k in range(N_KV)},
        compiler_params=_cparams(("arbitrary",)),
        name="project_prompt",
    )(h, g, wn, wt, wkv, *states)
    return outs[0], outs[1], list(outs[2:])


def _out_kernel(h_ref, a_ref, b_ref, c_ref, w_ref, g_ref, o_ref):
    y = (_dot(a_ref[...].astype(BF16), w_ref[0:W_A, :])
         + _dot(b_ref[...].astype(BF16), w_ref[W_A:W_A + W_B, :])
         + _dot(c_ref[...].astype(BF16), w_ref[W_A + W_B:, :]))
    ms = jnp.mean(y * y, axis=-1, keepdims=True)
    o_ref[...] = h_ref[...] + y * lax.rsqrt(ms + EPS) * g_ref[...]


def _output(h, a, b, c, w, g, tm):
    m = h.shape[0]
    return pl.pallas_call(
        _out_kernel,
        grid=(m // tm,),
        in_specs=[pl.BlockSpec((tm, D_MODEL), lambda i: (i, 0)),
                  pl.BlockSpec((tm, W_A), lambda i: (i, 0)),
                  pl.BlockSpec((tm, W_B), lambda i: (i, 0)),
                  pl.BlockSpec((tm, W_C), lambda i: (i, 0)),
                  pl.BlockSpec((D_MODEL, D_MODEL), lambda i: (0, 0)),
                  pl.BlockSpec((1, D_MODEL), lambda i: (0, 0))],
        out_specs=pl.BlockSpec((tm, D_MODEL), lambda i: (i, 0)),
        out_shape=jax.ShapeDtypeStruct((m, D_MODEL), F32),
        compiler_params=_cparams(("arbitrary",)),
        name="output",
    )(h, a, b, c, w, g)


def _softmax_steps(scores, vts, states):
    stats = []
    for s, st in zip(scores, states):
        m_new = jnp.max(s, axis=0, keepdims=True)
        alpha = None
        if st is not None:
            m_new = jnp.maximum(st[0], m_new)
            alpha = jnp.exp2(st[0] - m_new)
        p = jnp.exp2(s - m_new)
        l_new = jnp.sum(p, axis=0, keepdims=True)
        if st is not None:
            l_new = alpha * st[1] + l_new
        stats.append((m_new, alpha, l_new, p.astype(BF16)))
    out = []
    for (m_new, alpha, l_new, p), vt, st in zip(stats, vts, states):
        pv = _dot(vt, p)
        out.append((m_new, l_new, pv if st is None else alpha * st[2] + pv))
    return out


def _attn_prompt_kernel(qt_ref, zat_ref, gt_ref, vct_ref, vst_ref, vwt_ref, kc_ref, ks_ref, kw_ref,
                        wk_ref, wvt_ref, seg_ref, bnear_ref, bct_ref, ett_ref, e2t_ref,
                        o_ref, kc_s, vct_s, ks_s, kw_s, vst_s, vwt_s, madd_s, *, ns, nt):
    i = pl.program_id(1)
    nsp = -(-ns // 8) * 8

    @pl.when(i == 0)
    def _():
        x3 = kc_ref[...].reshape(ns, SEL_BLOCK, KV_COLS)
        w = wk_ref[...]
        kc_s[0:ns, :] = jnp.sum(x3[:, :L_CMP, :] * w[None], axis=1)
        kc_s[ns:2 * ns, :] = jnp.sum(x3[:, L_CMP:, :] * w[None], axis=1)
        hi, mid = _split2(vct_ref[...] * wvt_ref[...])
        seg = seg_ref[...]
        vct_s[...] = _dot(hi, seg) + _dot(mid, seg)
        for g in range(KV_A):
            ks_s[g] = ks_ref[:, g * HEAD_DIM:(g + 1) * HEAD_DIM].astype(BF16)
            kw_s[g] = kw_ref[:, g * HEAD_DIM:(g + 1) * HEAD_DIM].astype(BF16)
        for c in range(nt):
            vst_s[c] = vst_ref[:, c * TK:(c + 1) * TK].astype(BF16)
            vwt_s[c] = vwt_ref[:, c * TK:(c + 1) * TK].astype(BF16)

    qt = qt_ref[...]
    q0 = i * TQ

    row_c = lax.broadcasted_iota(jnp.int32, (2 * ns, TQ), 0)
    qpos_c = q0 + lax.broadcasted_iota(jnp.int32, (2 * ns, TQ), 1)
    blk_c = jnp.where(row_c < ns, 2 * row_c, 2 * (row_c - ns) + 1)
    mask_c = qpos_c >= blk_c * L_CMP + (L_CMP - 1)
    blk_t = lax.broadcasted_iota(jnp.int32, (nsp, TQ), 0)
    cur_t = (q0 + lax.broadcasted_iota(jnp.int32, (nsp, TQ), 1)) // SEL_BLOCK
    forced = (blk_t == 0) | ((blk_t <= cur_t) & (blk_t > cur_t - 2))
    kq_gap = (lax.broadcasted_iota(jnp.int32, (TK, TQ), 0) - lax.broadcasted_iota(jnp.int32, (TK, TQ), 1))

    oc_parts = []
    for g in range(KV_A):
        kh, kl = _split2(kc_s[:, g * HEAD_DIM:(g + 1) * HEAD_DIM])
        vcg = vct_s[g * HEAD_DIM:(g + 1) * HEAD_DIM, :].astype(BF16)
        imp = jnp.zeros((2 * ns, TQ), F32)
        for r in range(REP_A):
            h = g * REP_A + r
            qh, ql = _split2(qt[h * HEAD_DIM:(h + 1) * HEAD_DIM, :])
            s = (_dot(kh, qh) + _dot(kl, qh) + _dot(kh, ql)) * SM_SCALE + bct_ref[h]
            s = jnp.where(mask_c, s, NEG)
            e = jnp.exp(s - jnp.max(s, axis=0, keepdims=True))
            p = e / jnp.sum(e, axis=0, keepdims=True)
            p = jnp.where(mask_c, p, 0.0)
            oc_parts.append(_dot(vcg, p.astype(BF16)))
            imp = imp + p
        simp = imp[0:ns, :] + imp[ns:2 * ns, :]
        if nsp > ns:
            simp = jnp.concatenate([simp, jnp.zeros((nsp - ns, TQ), F32)], axis=0)
        score = jnp.where(forced, FORCE_SCORE, simp)
        score = jnp.where(blk_t <= cur_t, score, NEG)
        rank = _rank_select(score, blk_t, ns, 0)
        sel = jnp.where((rank < TOP_K) & (blk_t <= cur_t), 1.0, 0.0)
        sel = jnp.concatenate([sel, jnp.zeros((128 - nsp, TQ), F32)], axis=0).astype(BF16)
        def expand(c, _, g=g, sel=sel):
            r0 = pl.multiple_of(c * TK, TK)
            mm = (_dot(ett_ref[pl.ds(r0, TK), :], sel) - 1.0) * (-NEG)
            madd_s[g, c] = jnp.where(kq_gap + r0 > q0, NEG, mm)
            return 0

        lax.fori_loop(0, i + 1, expand, 0)
    oc_t = jnp.concatenate(oc_parts, axis=0)

    ok1, ok2 = i >= 1, i >= 2
    c1, c2 = jnp.maximum(i - 1, 0), jnp.maximum(i - 2, 0)
    n_far = jnp.maximum(i - 1, 0)
    qs = [(qt[h * HEAD_DIM:(h + 1) * HEAD_DIM, :] * (SM_SCALE * LOG2E)).astype(BF16) for h in range(H_A)]
    group = [h // REP_A for h in range(H_A)]

    def kv_sel(g, c):
        r0 = pl.multiple_of(c * TK, TK)
        return ks_s[g, pl.ds(r0, TK), :], vst_s[c, g * HEAD_DIM:(g + 1) * HEAD_DIM, :]

    def kv_win(g, c):
        r0 = pl.multiple_of(c * TK, TK)
        return kw_s[g, pl.ds(r0, TK), :], vwt_s[c, g * HEAD_DIM:(g + 1) * HEAD_DIM, :]

    def far(c, carry):
        kv = [kv_sel(g, c) for g in range(KV_A)]
        ma = [madd_s[g, c] for g in range(KV_A)]
        scores = [_dot(kv[group[h]][0], qs[h]) + ma[group[h]] for h in range(H_A)]
        return tuple(_softmax_steps(scores, [kv[group[h]][1] for h in range(H_A)], carry))

    init = tuple((jnp.full((1, TQ), NEG, F32), jnp.zeros((1, TQ), F32), jnp.zeros((HEAD_DIM, TQ), F32))
                 for _ in range(H_A))
    sel_st = list(lax.fori_loop(0, n_far, far, init))
    win_st = [None] * H_A
    for c, ok, d_idx in ((c1, ok1, 1), (i, None, 0)):
        kvs = [kv_sel(g, c) for g in range(KV_A)]
        kvw = [kv_win(g, c) for g in range(KV_A)]
        kcat = [jnp.concatenate([kvs[g][0], kvw[g][0]], axis=0) for g in range(KV_A)]
        ma = [madd_s[g, c] for g in range(KV_A)]
        s2 = [_dot(kcat[group[h]], qs[h]) for h in range(H_A)]
        scores, vts, states = [], [], []
        for h in range(H_A):
            bias = bnear_ref[h, d_idx]
            if ok is None:
                a_sel, a_win = ma[group[h]] + bias, jnp.where(kq_gap <= 0, bias, NEG)
            else:
                a_sel, a_win = jnp.where(ok, ma[group[h]] + bias, NEG), jnp.where(ok, bias, NEG)
            scores += [s2[h][0:TK] + a_sel, s2[h][TK:2 * TK] + a_win]
            vts += [kvs[group[h]][1], kvw[group[h]][1]]
            states += [sel_st[h], win_st[h]]
        new = _softmax_steps(scores, vts, states)
        sel_st, win_st = new[0::2], new[1::2]
    kvw = [kv_win(g, c2) for g in range(KV_A)]
    far_add = jnp.where((kq_gap > 0) & ok2, 0.0, NEG)
    win_st = _softmax_steps([_dot(kvw[group[h]][0], qs[h]) + far_add for h in range(H_A)],
                            [kvw[group[h]][1] for h in range(H_A)], win_st)
    os_t = jnp.concatenate([acc / l for (_, l, acc) in sel_st], axis=0)
    ow_t = jnp.concatenate([acc / l for (_, l, acc) in win_st], axis=0)

    ge = _dot_exact_lhs(e2t_ref[...], _sigmoid(gt_ref[...]))
    o_t = ge[0:W_A] * oc_t + ge[W_A:2 * W_A] * os_t + ge[2 * W_A:3 * W_A] * ow_t
    o_ref[...] = (o_t * _silu(zat_ref[...])).T


def _attn_prompt(layer, proj_n, proj_t, states, wk, wvt, seg, bnear, bct, ett, e2t, b, t):
    ns, nt = t // SEL_BLOCK, t // TQ
    st_spec = pl.BlockSpec((None, None, KV_COLS, t), lambda bi, i: (layer, bi, 0, 0))

    def tr_spec(rows, row0, width):
        return pl.BlockSpec((None, rows, width), lambda bi, i: (bi, row0 // rows, i))

    def nat_spec(col0):
        return pl.BlockSpec((t, KV_COLS), lambda bi, i: (bi, col0 // KV_COLS))

    def const_spec(a):
        return pl.BlockSpec(a.shape, lambda bi, i: (0,) * a.ndim)

    in_specs = [tr_spec(W_A, TR_Q, TQ), tr_spec(W_A, TR_ZA, TQ), tr_spec(128, TR_GATE, TQ),
                st_spec, st_spec, st_spec,
                nat_spec(NAT_KC), nat_spec(NAT_KS), nat_spec(NAT_KW),
                const_spec(wk), const_spec(wvt), const_spec(seg), const_spec(bnear),
                pl.BlockSpec((H_A, 2 * ns, TQ), lambda bi, i: (0, 0, i)),
                const_spec(ett), const_spec(e2t)]
    scratch = [pltpu.VMEM((2 * ns, KV_COLS), F32), pltpu.VMEM((KV_COLS, 2 * ns), F32),
               pltpu.VMEM((KV_A, t, HEAD_DIM), BF16), pltpu.VMEM((KV_A, t, HEAD_DIM), BF16),
               pltpu.VMEM((nt, KV_COLS, TK), BF16), pltpu.VMEM((nt, KV_COLS, TK), BF16),
               pltpu.VMEM((KV_A, nt, TK, TQ), F32)]
    return pl.pallas_call(
        functools.partial(_attn_prompt_kernel, ns=ns, nt=nt),
        grid=(b, nt),
        in_specs=in_specs,
        out_specs=pl.BlockSpec((TQ, W_A), lambda bi, i: (bi * nt + i, 0)),
        out_shape=jax.ShapeDtypeStruct((b * t, W_A), F32),
        scratch_shapes=scratch,
        compiler_params=_cparams(("arbitrary", "arbitrary")),
        name="attn_prompt",
    )(proj_t, proj_t, proj_t, states[1], states[3], states[5], proj_n, proj_n, proj_n,
      wk, wvt, seg, bnear, bct, ett, e2t)


def _dcmp_kernel(pt_ref, *refs, pages):
    del pt_ref
    k_refs, v_refs = refs[0:pages], refs[pages:2 * pages]
    wkt_ref, wvt_ref, seg_ref = refs[2 * pages:2 * pages + 3]
    ko_ref, vo_ref = refs[2 * pages + 3:]
    for src, w_ref, o_ref in ((k_refs, wkt_ref, ko_ref), (v_refs, wvt_ref, vo_ref)):
        w = w_ref[...]
        parts = [_split2(src[p][...] * w) for p in range(pages)]
        x = jnp.concatenate([hi for hi, _ in parts] + [lo for _, lo in parts], axis=1)
        o_ref[...] = _dot(x, seg_ref[...])


def _decode_compress(page_table, pool_k, pool_v, wkt, wvt, seg, pages):
    depth, db, n_pages = pool_k.shape[0], page_table.shape[0], page_table.shape[1]
    nc = n_pages // pages
    ncol = seg.shape[1]

    def page_spec(p):
        return pl.BlockSpec((None, None, KV_COLS, PAGE), lambda l, b, c, pt: (l, pt[b, c * pages + p], 0, 0))

    w_spec = pl.BlockSpec((None, KV_COLS, PAGE), lambda l, b, c, pt: (l, 0, 0))
    s_spec = pl.BlockSpec(seg.shape, lambda l, b, c, pt: (0, 0))
    o_spec = pl.BlockSpec((None, None, None, KV_COLS, ncol), lambda l, b, c, pt: (l, b, c, 0, 0))
    o_shape = jax.ShapeDtypeStruct((depth, db, nc, KV_COLS, ncol), F32)
    return pl.pallas_call(
        functools.partial(_dcmp_kernel, pages=pages),
        grid_spec=pltpu.PrefetchScalarGridSpec(
            num_scalar_prefetch=1,
            grid=(depth, db, nc),
            in_specs=[page_spec(p) for p in range(pages)] * 2 + [w_spec, w_spec, s_spec],
            out_specs=[o_spec] * 2),
        out_shape=[o_shape] * 2,
        compiler_params=_cparams(("arbitrary", "arbitrary", "arbitrary")),
        name="decode_compress",
    )(page_table, *([pool_k] * pages), *([pool_v] * pages), wkt, wvt, seg)


def _softmax_rows(qbd, kt, add, m, l):
    s = _dot(qbd, kt) + add
    m_new = jnp.maximum(m, jnp.max(s, axis=-1, keepdims=True))
    alpha = jnp.exp2(m - m_new)
    p = jnp.exp2(s - m_new)
    return m_new, alpha, alpha * l + jnp.sum(p, axis=-1, keepdims=True), p.astype(BF16)


def _attn_decode_kernel(pt_ref, *refs, pages, nps, nc, dt):
    del pt_ref
    (q_ref, za_ref, kv_ref, gate_ref, kct_ref, vct_ref) = refs[0:6]
    sk_refs, sv_refs = refs[6:6 + pages], refs[6 + pages:6 + 2 * pages]
    (wink_ref, winv_ref, bcd_ref, blast_ref, bnew_ref, bwin_ref, e_ref, e2_ref,
     o_ref, wk_o, wv_o, qbd_s, m_s, l_s, acc_s, madd_s, oc_s) = refs[6 + 2 * pages:]
    c = pl.program_id(1)
    ck = pages * PAGE
    rows = KV_A * REP_A * dt
    zpad = jnp.zeros((dt, HEAD_DIM), F32)

    @pl.when(c == 0)
    def _():
        q = q_ref[...]
        blocks = []
        for g in range(KV_A):
            for r in range(REP_A):
                h = g * REP_A + r
                piece = q[:, h * HEAD_DIM:(h + 1) * HEAD_DIM]
                blocks.append(jnp.concatenate([piece, zpad] if g == 0 else [zpad, piece], axis=1))
        qbd = jnp.concatenate(blocks, axis=0)
        qbd_s[...] = qbd
        s = _dot_f32(qbd, kct_ref[...]) * SM_SCALE + bcd_ref[...]
        e = jnp.exp(s - jnp.max(s, axis=-1, keepdims=True))
        p = e / jnp.sum(e, axis=-1, keepdims=True)
        oc_s[...] = _dot_nt(p.astype(BF16), vct_ref[...].astype(BF16))
        scores = []
        for g in range(KV_A):
            pe = [p[(g * REP_A + r) * dt:(g * REP_A + r + 1) * dt, 0:nps] for r in range(REP_A)]
            po = [p[(g * REP_A + r) * dt:(g * REP_A + r + 1) * dt, nps:2 * nps] for r in range(REP_A)]
            scores.append((pe[0] + pe[1] + pe[2] + pe[3]) + (po[0] + po[1] + po[2] + po[3]))
        simp = jnp.concatenate(scores, axis=0)
        blk = lax.broadcasted_iota(jnp.int32, simp.shape, 1)
        score = jnp.where((blk == 0) | (blk >= nps - 1), FORCE_SCORE, simp)
        rank = _rank_select(score, blk, nps, 1)
        sel = jnp.where(rank < TOP_K - 1, 1.0, 0.0).astype(BF16)
        madd = (_dot(sel, e_ref[...]) - 1.0) * (-NEG)
        madd = jnp.concatenate([madd[0:dt]] * REP_A + [madd[dt:2 * dt]] * REP_A, axis=0)
        for j in range(nc):
            tile = madd[:, j * ck:(j + 1) * ck]
            if j == nc - 1:
                tile = jnp.concatenate([tile[:, :ck - PAGE], tile[:, ck - PAGE:] + blast_ref[...]], axis=1)
            madd_s[j] = tile
        m_s[...] = jnp.full(m_s.shape, NEG, F32)
        l_s[...] = jnp.zeros(l_s.shape, F32)
        acc_s[...] = jnp.zeros(acc_s.shape, F32)

    qbd = (qbd_s[...] * (SM_SCALE * LOG2E)).astype(BF16)
    kt = jnp.concatenate([r[...].astype(BF16) for r in sk_refs], axis=1)
    vt = jnp.concatenate([r[...].astype(BF16) for r in sv_refs], axis=1)
    m, alpha, l, p = _softmax_rows(qbd, kt, madd_s[c], m_s[...], l_s[...])
    acc = alpha * acc_s[...] + _dot_nt(p, vt)
    m_s[...], l_s[...], acc_s[...] = m, l, acc

    @pl.when(c == nc - 1)
    def _():
        row_t = lax.broadcasted_iota(jnp.int32, (rows, 128), 0) % dt
        col = lax.broadcasted_iota(jnp.int32, (rows, 128), 1)
        new_add = jnp.where((col <= row_t) & (col < dt), bnew_ref[...], NEG)
        row_w = lax.broadcasted_iota(jnp.int32, (rows, WINDOW), 0) % dt
        col_w = lax.broadcasted_iota(jnp.int32, (rows, WINDOW), 1)
        win_add = jnp.where(col_w > row_w, bwin_ref[...], NEG)
        pad = jnp.zeros((128 - dt, KV_COLS), F32)

        def new_tile(src):
            return jnp.concatenate([kv_ref[:, src * KV_COLS:(src + 1) * KV_COLS], pad], axis=0).astype(BF16)

        def attend_nat(kn, vn, add, m, l, acc):
            s = _dot_nt(qbd, kn) + add
            m_new = jnp.maximum(m, jnp.max(s, axis=-1, keepdims=True))
            alpha = jnp.exp2(m - m_new)
            p = jnp.exp2(s - m_new)
            return m_new, alpha * l + jnp.sum(p, axis=-1, keepdims=True), alpha * acc + _dot(p.astype(BF16), vn)

        m2, l2, acc2 = attend_nat(new_tile(2), new_tile(3), new_add, m, l, acc)
        os_full = acc2 / l2
        init = (jnp.full((rows, 1), NEG, F32), jnp.zeros((rows, 1), F32), jnp.zeros((rows, KV_COLS), F32))
        carry = attend_nat(wink_ref[...].astype(BF16), winv_ref[...].astype(BF16), win_add, *init)
        m3, l3, acc3 = attend_nat(new_tile(4), new_tile(5), new_add, *carry)
        ow_full = acc3 / l3

        def heads(full):
            parts = []
            for g in range(KV_A):
                for r in range(REP_A):
                    r0 = (g * REP_A + r) * dt
                    parts.append(full[r0:r0 + dt, g * HEAD_DIM:(g + 1) * HEAD_DIM])
            return jnp.concatenate(parts, axis=1)

        sg = _sigmoid(gate_ref[...])
        ge = _dot_exact_rhs(sg, e2_ref[...])
        o = ge[:, 0:W_A] * heads(oc_s[...]) + ge[:, W_A:2 * W_A] * heads(os_full) + ge[:, 2 * W_A:] * heads(ow_full)
        o_ref[...] = o * _silu(za_ref[...])
        wk_o[...] = jnp.concatenate([wink_ref[dt:, :], kv_ref[:, 4 * KV_COLS:5 * KV_COLS]], axis=0)
        wv_o[...] = jnp.concatenate([winv_ref[dt:, :], kv_ref[:, 5 * KV_COLS:6 * KV_COLS]], axis=0)


def _attn_decode(layer, page_table, proj, kct, vct, pool_sk, pool_sv, win_k, win_v, tabs, e_sel, e2, pages, dt):
    db, n_pages = page_table.shape
    nps = n_pages * (PAGE // SEL_BLOCK)
    nc = n_pages // pages
    ck = pages * PAGE
    rows = KV_A * REP_A * dt
    bcd, blast, bnew, bwin = tabs

    def row_spec(width, col):
        return pl.BlockSpec((dt, width), lambda b, c, pt: (b, col // width))

    def page_spec(p):
        return pl.BlockSpec((None, None, KV_COLS, PAGE), lambda b, c, pt: (layer, pt[b, c * pages + p], 0, 0))

    cmp_spec = pl.BlockSpec((None, None, KV_COLS, 2 * nps), lambda b, c, pt: (layer, b, 0, 0))
    win_spec = pl.BlockSpec((None, None, WINDOW, KV_COLS), lambda b, c, pt: (layer, b, 0, 0))

    def const_spec(a):
        return pl.BlockSpec(a.shape, lambda b, c, pt: (0,) * a.ndim)

    in_specs = ([row_spec(W_A, COL_Q), row_spec(W_A, COL_ZA), row_spec(6 * KV_COLS, COL_KV), row_spec(128, COL_GATE)]
                + [cmp_spec] * 2 + [page_spec(p) for p in range(pages)] * 2 + [win_spec] * 2
                + [const_spec(a) for a in (bcd, blast, bnew, bwin, e_sel, e2)])
    out_specs = [pl.BlockSpec((dt, W_A), lambda b, c, pt: (b, 0)),
                 pl.BlockSpec((None, WINDOW, KV_COLS), lambda b, c, pt: (b, 0, 0)),
                 pl.BlockSpec((None, WINDOW, KV_COLS), lambda b, c, pt: (b, 0, 0))]
    out_shape = [jax.ShapeDtypeStruct((db * dt, W_A), F32),
                 jax.ShapeDtypeStruct((db, WINDOW, KV_COLS), F32),
                 jax.ShapeDtypeStruct((db, WINDOW, KV_COLS), F32)]
    scratch = [pltpu.VMEM((rows, KV_COLS), F32), pltpu.VMEM((rows, 1), F32), pltpu.VMEM((rows, 1), F32),
               pltpu.VMEM((rows, KV_COLS), F32), pltpu.VMEM((nc, rows, ck), F32), pltpu.VMEM((rows, KV_COLS), F32)]
    return pl.pallas_call(
        functools.partial(_attn_decode_kernel, pages=pages, nps=nps, nc=nc, dt=dt),
        grid_spec=pltpu.PrefetchScalarGridSpec(
            num_scalar_prefetch=1, grid=(db, nc), in_specs=in_specs, out_specs=out_specs, scratch_shapes=scratch),
        out_shape=out_shape,
        compiler_params=_cparams(("arbitrary", "arbitrary")),
        name="attn_decode",
    )(page_table, proj, proj, proj, proj, kct, vct, *([pool_sk] * pages), *([pool_sv] * pages),
      win_k, win_v, bcd, blast, bnew, bwin, e_sel, e2)


EXT_B0 = 32
EXT_C0 = 8
CONV_ROWS = 32
SH_PAD = (CONV_B - 1) // 8 * 8


def _mixer_kernel(glu_ref, zb_ref, xc_ref, zc_ref, bufb_ref, bufc_ref, h0_ref,
                  cbw_ref, cbb_ref, gng_ref, gnb_ref, wpw_ref, gones_ref,
                  ccw_ref, ccb_ref, wa_ref, ba_ref, wx_ref, bx_ref, lam_ref,
                  bo_ref, co_ref, cbs_ref, ccs_ref, hs_ref,
                  extb, extc, hcar, cbuf, shb, *, tt):
    j = pl.program_id(1)
    nb, nc = CONV_B - 1, CONV_C - 1

    @pl.when(j == 0)
    def _():
        extb[EXT_B0 - nb:EXT_B0, :] = bufb_ref[...]
        extc[EXT_C0 - nc:EXT_C0, :] = bufc_ref[...]
        hcar[...] = h0_ref[...]

    glu = glu_ref[...]
    extb[EXT_B0:EXT_B0 + tt, :] = glu[:, 0:W_B] * _sigmoid(glu[:, W_B:2 * W_B])
    for r in range(8):
        n_r = tt + (CONV_B - 1 - r) // 8 * 8
        shb[r, 0:n_r, :] = extb[EXT_B0 - nb + r:EXT_B0 - nb + r + n_r, :]
    step = min(CONV_ROWS, tt)
    for r0 in range(0, tt, step):
        acc = jnp.zeros((step, W_B), F32)
        for k in range(CONV_B):
            lo = (k // 8) * 8 + r0
            acc = acc + cbw_ref[k:k + 1, :] * shb[k % 8, lo:lo + step, :]
        cbuf[r0:r0 + step, :] = acc + cbb_ref[...]
    cv = cbuf[...]
    gones = gones_ref[...]
    mu = _dot_exact_rhs(cv, gones) * (1.0 / GN_GROUP)
    d = cv - mu
    var = _dot_exact_rhs(d * d, gones) * (1.0 / GN_GROUP)
    cn = d * lax.rsqrt(var + EPS) * gng_ref[...] + gnb_ref[...]
    bo_ref[...] = _dot(_silu(cn).astype(BF16), wpw_ref[...]) * _silu(zb_ref[...])

    extc[EXT_C0:EXT_C0 + tt, :] = xc_ref[...]
    u = jnp.zeros((tt, W_C), F32)
    for k in range(CONV_C):
        lo = EXT_C0 - nc + k
        u = u + ccw_ref[k:k + 1, :] * extc[lo:lo + tt, :]
    u = u + ccb_ref[...]
    ub = u.astype(BF16)
    r = _sigmoid(_dot(ub, wa_ref[...]) + ba_ref[...])
    ig = _sigmoid(_dot(ub, wx_ref[...]) + bx_ref[...])
    nl = -lam_ref[...]
    softplus = jnp.maximum(nl, 0.0) + jnp.log1p(jnp.exp(-jnp.abs(nl)))
    log_a = -LRU_C * r * softplus
    a = jnp.exp(log_a)
    b = jnp.sqrt(-_expm1(2.0 * log_a)) * (ig * u)
    row = lax.broadcasted_iota(jnp.int32, (tt, W_C), 0)
    s = 1
    while s < tt:
        a_sh = jnp.where(row < s, 1.0, pltpu.roll(a, s, axis=0))
        b_sh = jnp.where(row < s, 0.0, pltpu.roll(b, s, axis=0))
        b = a * b_sh + b
        a = a * a_sh
        s *= 2
    hh = a * hcar[...] + b
    co_ref[...] = hh * _silu(zc_ref[...])
    hcar[...] = hh[tt - 1:tt, :]

    new_b = extb[EXT_B0 + tt - nb:EXT_B0 + tt, :]
    new_c = extc[EXT_C0 + tt - nc:EXT_C0 + tt, :]
    extb[EXT_B0 - nb:EXT_B0, :] = new_b
    extc[EXT_C0 - nc:EXT_C0, :] = new_c

    @pl.when(j == pl.num_programs(1) - 1)
    def _():
        cbs_ref[...] = new_b
        ccs_ref[...] = new_c
        hs_ref[...] = hh[tt - 1:tt, :]


def _mixers(proj, cols, bufb, bufc, h0, lw, gones, b, t, tt):
    nj = t // tt
    c_glu, c_zb, c_xc, c_zc = cols

    def row_spec(width, col):
        return pl.BlockSpec((tt, width), lambda bi, j: (bi * nj + j, col // width))

    def st_spec(n, w):
        return pl.BlockSpec((None, n, w), lambda bi, j: (bi, 0, 0))

    def const_spec(a):
        return pl.BlockSpec(a.shape, lambda bi, j: (0,) * a.ndim)

    consts = [lw["cbw"], lw["cbb"], lw["gng"], lw["gnb"], lw["wpw"], gones,
              lw["ccw"], lw["ccb"], lw["wa"], lw["ba"], lw["wx"], lw["bx"], lw["lam"]]
    in_specs = ([row_spec(2 * W_B, c_glu), row_spec(W_B, c_zb), row_spec(W_C, c_xc), row_spec(W_C, c_zc),
                 st_spec(CONV_B - 1, W_B), st_spec(CONV_C - 1, W_C), st_spec(1, W_C)]
                + [const_spec(a) for a in consts])
    out_specs = [pl.BlockSpec((tt, W_B), lambda bi, j: (bi * nj + j, 0)),
                 pl.BlockSpec((tt, W_C), lambda bi, j: (bi * nj + j, 0)),
                 st_spec(CONV_B - 1, W_B), st_spec(CONV_C - 1, W_C), st_spec(1, W_C)]
    out_shape = [jax.ShapeDtypeStruct((b * t, W_B), F32), jax.ShapeDtypeStruct((b * t, W_C), F32),
                 jax.ShapeDtypeStruct((b, CONV_B - 1, W_B), F32), jax.ShapeDtypeStruct((b, CONV_C - 1, W_C), F32),
                 jax.ShapeDtypeStruct((b, 1, W_C), F32)]
    scratch = [pltpu.VMEM((EXT_B0 + tt, W_B), F32), pltpu.VMEM((EXT_C0 + tt, W_C), F32),
               pltpu.VMEM((1, W_C), F32), pltpu.VMEM((tt, W_B), F32), pltpu.VMEM((8, tt + SH_PAD, W_B), F32)]
    return pl.pallas_call(
        functools.partial(_mixer_kernel, tt=tt),
        grid=(b, nj),
        in_specs=in_specs, out_specs=out_specs, out_shape=out_shape, scratch_shapes=scratch,
        compiler_params=_cparams(("arbitrary", "arbitrary")),
        name="mixers",
    )(proj, proj, proj, proj, bufb, bufc, h0, *consts)


def _block_diag(w):
    nblk, c, d = w.shape
    eye = jnp.eye(nblk, dtype=w.dtype)
    return (eye[:, None, :, None] * w[:, :, None, :]).reshape(nblk * c, nblk * d)


def _row_tile(m, cap, step=8):
    t = min(m, cap)
    while m % t:
        t -= step
    return t


def kernel(x_prompt, x_sample, cache_cmp_k, cache_cmp_v, cache_sel_k, cache_sel_v, cache_win_k, cache_win_v,
           state_conv_b, state_conv_c, state_rglru, page_table, rel_bias, g_pre, g_post, w_in, w_out,
           w_cmp_k, w_cmp_v, conv_b_w, conv_b_b, gn_gain, gn_bias, w_pw_b, conv_c_w, conv_c_b,
           w_lru_a, b_lru_a, w_lru_x, b_lru_x, lru_lambda):
    depth = w_in.shape[0]
    b, t, _ = x_prompt.shape
    db, dt, _ = x_sample.shape
    n_pages = page_table.shape[1]
    past = n_pages * PAGE
    nps = past // SEL_BLOCK
    ns, nt = t // SEL_BLOCK, t // TQ
    assert t % TQ == 0 and TQ == TK and WINDOW == 2 * TK and TOP_K <= ns <= 128 and TOP_K < nps <= 128
    assert dt <= 8 and (past + dt) // L_CMP == past // L_CMP and cache_win_k.shape[2] == WINDOW
    pages = min(16, n_pages)
    assert n_pages % pages == 0
    nc = n_pages // pages
    rows = KV_A * REP_A * dt

    zpad = jnp.zeros((depth, D_MODEL, D_IN_PAD - D_IN), w_in.dtype)
    w_r = jnp.concatenate(
        [w_in[..., 0:512], w_in[..., 1304:1816], w_in[..., 1816:2328], w_in[..., 512:1280],
         w_in[..., 2328:3096], w_in[..., 1280:1304], zpad], axis=-1).astype(BF16)
    w_n = jnp.concatenate(
        [w_in[..., 1816:3096], w_in[..., 512:640], w_in[..., 768:896], w_in[..., 1024:1152]], axis=-1).astype(BF16)
    w_it = jnp.swapaxes(w_in, 1, 2)
    w_t = jnp.concatenate(
        [w_it[:, 0:512], w_it[:, 1304:1816], w_it[:, 1280:1304],
         jnp.zeros((depth, N_TR - 1048, D_MODEL), w_in.dtype)], axis=1).astype(BF16)
    w_kv = w_it[:, 512:1280].astype(BF16)
    w_o = w_out.astype(BF16)
    wck = jnp.tile(w_cmp_k, (1, 1, KV_A))
    wvt_p = jnp.tile(jnp.swapaxes(w_cmp_v, 1, 2), (1, KV_A, t // L_CMP))
    wkt_d = jnp.tile(jnp.swapaxes(w_cmp_k, 1, 2), (1, KV_A, PAGE // L_CMP))
    wvt_d = jnp.tile(jnp.swapaxes(w_cmp_v, 1, 2), (1, KV_A, PAGE // L_CMP))
    gones = jnp.asarray(_group_ones_np(), BF16)
    e2 = jnp.asarray(_gate_expand_np(), BF16)
    e2t = jnp.asarray(_gate_expand_np().T, BF16)
    ett = jnp.asarray(_expand_np(ns, 128).T, BF16)
    e_d = jnp.asarray(_expand_np(nps, nps), BF16)
    seg_p = jnp.asarray(_segment_np(t, ns), BF16)
    seg_d = jnp.asarray(np.tile(_segment_np(pages * PAGE, pages * PAGE // SEL_BLOCK), (2, 1)), BF16)

    kk, qq = np.arange(TK)[:, None], np.arange(TQ)[None, :]
    bk_near = np.concatenate([_bucket_np(d0 + qq - kk) for d0 in (0, TQ)], axis=0)
    bnear = _bias_lookup(rel_bias, bk_near, shift=True, scale=LOG2E).reshape(H_A, 2, TK, TQ)
    blk_eo = np.concatenate([2 * np.arange(ns), 2 * np.arange(ns) + 1])[:, None]
    bct = _bias_lookup(rel_bias, _bucket_np(np.arange(t)[None, :] - (blk_eo * L_CMP + L_CMP - 1)))
    qd = past + np.arange(dt)[:, None]
    blk_d = np.concatenate([2 * np.arange(nps), 2 * np.arange(nps) + 1])[None, :]
    bcd = _bias_lookup(rel_bias, _bucket_np(qd - (blk_d * L_CMP + L_CMP - 1))).reshape(rows, 2 * nps)
    blast = _bias_lookup(rel_bias, _bucket_np(qd - (past - PAGE + np.arange(PAGE))[None, :]), shift=True, scale=LOG2E)
    bnew = _bias_lookup(rel_bias, _bucket_np(np.arange(dt)[:, None] - np.arange(128)[None, :]), shift=True, scale=LOG2E)
    bwin = _bias_lookup(rel_bias, _bucket_np(WINDOW + np.arange(dt)[:, None] - np.arange(WINDOW)[None, :]),
                        shift=True, scale=LOG2E)
    dtabs = (bcd, blast.reshape(rows, PAGE), bnew.reshape(rows, 128), bwin.reshape(rows, WINDOW))

    pool = lambda a: jnp.transpose(a, (0, 1, 3, 4, 2)).reshape(a.shape[0], a.shape[1], KV_COLS, PAGE)
    kc_ch, vc_ch = _decode_compress(page_table, pool(cache_cmp_k), pool(cache_cmp_v), wkt_d, wvt_d, seg_d, pages)

    def eo_table(x):
        x = x.reshape(depth, db, nc, KV_COLS, 2, nps // nc).transpose(0, 1, 3, 4, 2, 5)
        return x.reshape(depth, db, KV_COLS, 2 * nps)

    kct, vct = eo_table(kc_ch), eo_table(vc_ch)
    pool_sk, pool_sv = pool(cache_sel_k), pool(cache_sel_v)
    win_k = cache_win_k.reshape(depth, db, WINDOW, KV_COLS)
    win_v = cache_win_v.reshape(depth, db, WINDOW, KV_COLS)

    hp = x_prompt.reshape(b * t, D_MODEL)
    hs = x_sample.reshape(db * dt, D_MODEL)
    zeros_b = jnp.zeros((b, CONV_B - 1, W_B), F32)
    zeros_c = jnp.zeros((b, CONV_C - 1, W_C), F32)
    zeros_h = jnp.zeros((b, 1, W_C), F32)
    tm_p, tm_s = _row_tile(t, 512, 128), _row_tile(db * dt, 512)
    tt_p = _row_tile(t, 256)
    nat_cols = (NAT_GLU, NAT_ZB, NAT_XC, NAT_ZC)
    dec_cols = (COL_GLU, COL_ZB, COL_XC, COL_ZC)

    kv_states = [jnp.zeros((depth, b, KV_COLS, t), F32) for _ in range(N_KV)]
    p_states, s_states = [], []
    for l in range(depth):
        row = lambda a: a[l][None, :]
        lw = dict(cbw=conv_b_w[l], cbb=row(conv_b_b), gng=row(gn_gain), gnb=row(gn_bias), wpw=w_pw_b[l].astype(BF16),
                  ccw=conv_c_w[l], ccb=row(conv_c_b), wa=_block_diag(w_lru_a[l]).astype(BF16), ba=row(b_lru_a),
                  wx=_block_diag(w_lru_x[l]).astype(BF16), bx=row(b_lru_x), lam=row(lru_lambda))
        proj_n, proj_t, kv_states = _project_prompt(hp, row(g_pre), w_n[l], w_t[l], w_kv[l], kv_states, l, b, t, tm_p)
        a_out = _attn_prompt(l, proj_n, proj_t, kv_states, wck[l], wvt_p[l], seg_p, bnear, bct, ett, e2t, b, t)
        b_out, c_out, cb, cc, hc = _mixers(proj_n, nat_cols, zeros_b, zeros_c, zeros_h, lw, gones, b, t, tt_p)
        hp = _output(hp, a_out, b_out, c_out, w_o[l], row(g_post), tm_p)
        p_states.append((cb, cc, hc[:, 0]))
        proj = _project(hs, row(g_pre), w_r[l], tm_s)
        a_out, wk_n, wv_n = _attn_decode(l, page_table, proj, kct, vct, pool_sk, pool_sv, win_k, win_v,
                                         dtabs, e_d, e2, pages, dt)
        b_out, c_out, cb, cc, hc = _mixers(proj, dec_cols, state_conv_b[l], state_conv_c[l],
                                           state_rglru[l][:, None, :], lw, gones, db, dt, dt)
        hs = _output(hs, a_out, b_out, c_out, w_o[l], row(g_post), tm_s)
        kv = proj[:, COL_KV:COL_KV + 6 * KV_COLS].reshape(db, dt, 6, KV_A, HEAD_DIM)
        s_states.append((kv[:, :, 0], kv[:, :, 1], kv[:, :, 2], kv[:, :, 3],
                         wk_n.reshape(db, WINDOW, KV_A, HEAD_DIM), wv_n.reshape(db, WINDOW, KV_A, HEAD_DIM),
                         cb, cc, hc[:, 0]))

    cb_p, cc_p, h_p = [jnp.stack(a) for a in zip(*p_states)]
    kv_states = kv_states[:4] + [a[..., t - WINDOW:] for a in kv_states[4:]]
    ck_p, cv_p, sk_p, sv_p, wk_p, wv_p = [
        jnp.transpose(a.reshape(depth, b, KV_A, HEAD_DIM, a.shape[-1]), (0, 1, 4, 2, 3)) for a in kv_states]
    ck_s, cv_s, sk_s, sv_s, wk_s, wv_s, cb_s, cc_s, h_s = [jnp.stack(a) for a in zip(*s_states)]
    return (hp.reshape(b, t, D_MODEL), hs.reshape(db, dt, D_MODEL),
            ck_p, ck_s, cv_p, cv_s, sk_p, sk_s, sv_p, sv_s, wk_p, wk_s, wv_p, wv_s,
            cb_p, cb_s, cc_p, cc_s, h_p, h_s)
```

```python
import functools
import math

import numpy as np
import jax
import jax.numpy as jnp
from jax import lax
from jax.experimental import pallas as pl
from jax.experimental.pallas import tpu as pltpu

F32 = jnp.float32
BF16 = jnp.bfloat16

D_MODEL = 1024
HEAD_DIM = 64
W_A = D_MODEL // 2
W_B = D_MODEL // 4
W_C = D_MODEL // 4
H_A = W_A // HEAD_DIM
KV_A = 2
REP_A = H_A // KV_A
KV_COLS = KV_A * HEAD_DIM
L_CMP = 32
SEL_BLOCK = 64
TOP_K = 16
FORCE_SCORE = 1.0e4
WINDOW = 512
PAGE = 128
CONV_B = 31
CONV_C = 4
GN_GROUP = W_B // 4
LRU_C = 8.0
NUM_BUCKETS = 32
MAX_DISTANCE = 128
SM_SCALE = HEAD_DIM ** -0.5
LOG2E = 1.4426950408889634
EPS = 1e-6
NEG = -1e30

TQ = 256
TK = 256
D_IN = 3096
D_IN_PAD = 3200
COL_Q, COL_ZA, COL_GLU, COL_KV, COL_ZB, COL_XC, COL_ZC, COL_GATE = 0, 512, 1024, 1536, 2304, 2560, 2816, 3072
N_NAT = 1664
NAT_GLU, NAT_ZB, NAT_XC, NAT_ZC, NAT_KC, NAT_KS, NAT_KW = 0, 512, 768, 1024, 1280, 1408, 1536
N_TR = 1152
TR_Q, TR_ZA, TR_GATE = 0, 512, 1024
N_KV = 6
VMEM_LIMIT = 56 * 1024 * 1024


def _cparams(sem):
    return pltpu.CompilerParams(dimension_semantics=sem, vmem_limit_bytes=VMEM_LIMIT)


def _bucket_np(dist):
    n = np.maximum(dist, 0)
    max_exact = NUM_BUCKETS // 2
    nf = np.maximum(n, 1).astype(np.float32)
    large = max_exact + (np.log(nf / np.float32(max_exact)) / np.float32(math.log(MAX_DISTANCE / max_exact))
                         * np.float32(NUM_BUCKETS - max_exact)).astype(np.int32)
    return np.where(n < max_exact, n, np.minimum(large, NUM_BUCKETS - 1)).astype(np.int32)


def _expand_np(n_blocks, rows):
    e = np.zeros((rows, n_blocks * SEL_BLOCK), np.float32)
    for m in range(n_blocks):
        e[m, m * SEL_BLOCK:(m + 1) * SEL_BLOCK] = 1.0
    return e


def _gate_expand_np():
    e = np.zeros((128, 3 * W_A), np.float32)
    for br in range(3):
        for h in range(H_A):
            e[br * H_A + h, br * W_A + h * HEAD_DIM: br * W_A + (h + 1) * HEAD_DIM] = 1.0
    return e


def _group_ones_np():
    g = np.zeros((W_B, W_B), np.float32)
    for k in range(W_B // GN_GROUP):
        g[k * GN_GROUP:(k + 1) * GN_GROUP, k * GN_GROUP:(k + 1) * GN_GROUP] = 1.0
    return g


def _segment_np(n_rows, n_half):
    s = np.zeros((n_rows, 2 * n_half), np.float32)
    blk = np.arange(n_rows) // L_CMP
    s[np.arange(n_rows), np.where(blk % 2 == 0, blk // 2, n_half + blk // 2)] = 1.0
    return s


def _split2(a):
    hi = a.astype(BF16)
    return hi, (a - hi.astype(F32)).astype(BF16)


def _split3(a):
    hi = a.astype(BF16)
    r1 = a - hi.astype(F32)
    mid = r1.astype(BF16)
    return hi, mid, (r1 - mid.astype(F32)).astype(BF16)


def _dot(a, b):
    return jnp.dot(a, b, preferred_element_type=F32)


def _dot_nt(a, b):
    return lax.dot_general(a, b, (((1,), (1,)), ((), ())), preferred_element_type=F32)


def _dot_exact_rhs(a, b_bf16):
    hi, mid, lo = _split3(a)
    return _dot(hi, b_bf16) + _dot(mid, b_bf16) + _dot(lo, b_bf16)


def _dot_exact_lhs(a_bf16, b):
    hi, mid, lo = _split3(b)
    return _dot(a_bf16, hi) + _dot(a_bf16, mid) + _dot(a_bf16, lo)


def _dot_f32(a, b):
    ah, al = _split2(a)
    bh, bl = _split2(b)
    return _dot(ah, bh) + _dot(al, bh) + _dot(ah, bl)


def _sigmoid(x):
    return 1.0 / (1.0 + jnp.exp(-x))


def _silu(x):
    return x * _sigmoid(x)


def _expm1(x):
    u = jnp.exp(x)
    safe = jnp.where((u == 1.0) | (u == 0.0), 0.5, u)
    return jnp.where(u == 1.0, x, jnp.where(u == 0.0, -1.0, (safe - 1.0) * x / jnp.log(safe)))


def _rank_select(score, idx, n, axis):
    rank = jnp.zeros(score.shape, F32)
    for m in range(n):
        sm = score[m:m + 1, :] if axis == 0 else score[:, m:m + 1]
        beats = (sm > score) | ((sm == score) & (idx > m))
        rank = rank + jnp.where(beats, 1.0, 0.0)
    return rank


def _bias_kernel(rb_ref, bk_ref, o_ref, *, shift, scale):
    h = pl.program_id(0)
    bk = bk_ref[...]
    acc = jnp.zeros(bk.shape, F32)
    for b in range(NUM_BUCKETS):
        acc = jnp.where(bk == b, rb_ref[b, h], acc)
    if shift:
        acc = acc - rb_ref[NUM_BUCKETS - 1, h]
    o_ref[...] = acc * scale


def _bias_lookup(rel_bias, bucket, shift=False, scale=1.0):
    r, c = bucket.shape
    return pl.pallas_call(
        functools.partial(_bias_kernel, shift=shift, scale=scale),
        grid=(H_A,),
        in_specs=[pl.BlockSpec(memory_space=pltpu.SMEM),
                  pl.BlockSpec((r, c), lambda h: (0, 0))],
        out_specs=pl.BlockSpec((None, r, c), lambda h: (h, 0, 0)),
        out_shape=jax.ShapeDtypeStruct((H_A, r, c), F32),
        compiler_params=_cparams(("arbitrary",)),
        name="bias_lookup",
    )(rel_bias, jnp.asarray(bucket))


def _rms_bf16(x_ref, g_ref):
    x = x_ref[...]
    ms = jnp.mean(x * x, axis=-1, keepdims=True)
    return (x * lax.rsqrt(ms + EPS) * g_ref[...]).astype(BF16)


def _proj_kernel(x_ref, g_ref, w_ref, o_ref):
    u = _rms_bf16(x_ref, g_ref)
    step = 640
    for c in range(0, o_ref.shape[1], step):
        o_ref[:, c:c + step] = _dot(u, w_ref[:, c:c + step])


def _project(h, g, w, tm):
    m = h.shape[0]
    return pl.pallas_call(
        _proj_kernel,
        grid=(m // tm,),
        in_specs=[pl.BlockSpec((tm, D_MODEL), lambda i: (i, 0)),
                  pl.BlockSpec((1, D_MODEL), lambda i: (0, 0)),
                  pl.BlockSpec((D_MODEL, D_IN_PAD), lambda i: (0, 0))],
        out_specs=pl.BlockSpec((tm, D_IN_PAD), lambda i: (i, 0)),
        out_shape=jax.ShapeDtypeStruct((m, D_IN_PAD), F32),
        compiler_params=_cparams(("arbitrary",)),
        name="project",
    )(h, g, w)


def _proj2_kernel(x_ref, g_ref, wn_ref, wt_ref, wkv_ref, *refs):
    on_ref, ot_ref = refs[N_KV], refs[N_KV + 1]
    st_refs = refs[N_KV + 2:]
    u = _rms_bf16(x_ref, g_ref)
    for c0, c1 in ((0, 768), (768, N_NAT)):
        on_ref[:, c0:c1] = _dot(u, wn_ref[:, c0:c1])
    for c in range(0, N_TR, 384):
        ot_ref[c:c + 384, :] = _dot_nt(wt_ref[c:c + 384, :], u)
    for k in range(0, N_KV, 2):
        kv = _dot_nt(wkv_ref[k * KV_COLS:(k + 2) * KV_COLS, :], u)
        st_refs[k][...] = kv[0:KV_COLS]
        st_refs[k + 1][...] = kv[KV_COLS:2 * KV_COLS]


def _project_prompt(h, g, wn, wt, wkv, states, layer, b, t, tm):
    per = t // tm
    st_spec = pl.BlockSpec((None, None, KV_COLS, tm), lambda i: (layer, i // per, 0, i % per))
    outs = pl.pallas_call(
        _proj2_kernel,
        grid=(b * per,),
        in_specs=[pl.BlockSpec((tm, D_MODEL), lambda i: (i, 0)),
                  pl.BlockSpec((1, D_MODEL), lambda i: (0, 0)),
                  pl.BlockSpec((D_MODEL, N_NAT), lambda i: (0, 0)),
                  pl.BlockSpec((N_TR, D_MODEL), lambda i: (0, 0)),
                  pl.BlockSpec((N_KV * KV_COLS, D_MODEL), lambda i: (0, 0))]
                 + [pl.BlockSpec(memory_space=pl.ANY)] * N_KV,
        out_specs=[pl.BlockSpec((tm, N_NAT), lambda i: (i, 0)),
                   pl.BlockSpec((None, N_TR, tm), lambda i: (i // per, 0, i % per))] + [st_spec] * N_KV,
        out_shape=[jax.ShapeDtypeStruct((b * t, N_NAT), F32), jax.ShapeDtypeStruct((b, N_TR, t), F32)]
                  + [jax.ShapeDtypeStruct(a.shape, a.dtype) for a in states],
        input_output_aliases={5 + k: 2 + k for k in range(N_KV)},
        compiler_params=_cparams(("arbitrary",)),
        name="project_prompt",
    )(h, g, wn, wt, wkv, *states)
    return outs[0], outs[1], list(outs[2:])


def _out_kernel(h_ref, a_ref, b_ref, c_ref, w_ref, g_ref, o_ref):
    y = (_dot(a_ref[...].astype(BF16), w_ref[0:W_A, :])
         + _dot(b_ref[...].astype(BF16), w_ref[W_A:W_A + W_B, :])
         + _dot(c_ref[...].astype(BF16), w_ref[W_A + W_B:, :]))
    ms = jnp.mean(y * y, axis=-1, keepdims=True)
    o_ref[...] = h_ref[...] + y * lax.rsqrt(ms + EPS) * g_ref[...]


def _output(h, a, b, c, w, g, tm):
    m = h.shape[0]
    return pl.pallas_call(
        _out_kernel,
        grid=(m // tm,),
        in_specs=[pl.BlockSpec((tm, D_MODEL), lambda i: (i, 0)),
                  pl.BlockSpec((tm, W_A), lambda i: (i, 0)),
                  pl.BlockSpec((tm, W_B), lambda i: (i, 0)),
                  pl.BlockSpec((tm, W_C), lambda i: (i, 0)),
                  pl.BlockSpec((D_MODEL, D_MODEL), lambda i: (0, 0)),
                  pl.BlockSpec((1, D_MODEL), lambda i: (0, 0))],
        out_specs=pl.BlockSpec((tm, D_MODEL), lambda i: (i, 0)),
        out_shape=jax.ShapeDtypeStruct((m, D_MODEL), F32),
        compiler_params=_cparams(("arbitrary",)),
        name="output",
    )(h, a, b, c, w, g)


def _softmax_steps(scores, vts, states):
    stats = []
    for s, st in zip(scores, states):
        m_new = jnp.max(s, axis=0, keepdims=True)
        alpha = None
        if st is not None:
            m_new = jnp.maximum(st[0], m_new)
            alpha = jnp.exp2(st[0] - m_new)
        p = jnp.exp2(s - m_new)
        l_new = jnp.sum(p, axis=0, keepdims=True)
        if st is not None:
            l_new = alpha * st[1] + l_new
        stats.append((m_new, alpha, l_new, p.astype(BF16)))
    out = []
    for (m_new, alpha, l_new, p), vt, st in zip(stats, vts, states):
        pv = _dot(vt, p)
        out.append((m_new, l_new, pv if st is None else alpha * st[2] + pv))
    return out


def _attn_prompt_kernel(qt_ref, zat_ref, gt_ref, vct_ref, vst_ref, vwt_ref, kc_ref, ks_ref, kw_ref,
                        wk_ref, wvt_ref, seg_ref, bnear_ref, bct_ref, ett_ref, e2t_ref,
                        o_ref, kc_s, vct_s, ks_s, kw_s, vst_s, vwt_s, *, ns, nt):
    i = pl.program_id(1)
    nsp = -(-ns // 8) * 8

    @pl.when(i == 0)
    def _():
        x3 = kc_ref[...].reshape(ns, SEL_BLOCK, KV_COLS)
        w = wk_ref[...]
        kc_s[0:ns, :] = jnp.sum(x3[:, :L_CMP, :] * w[None], axis=1)
        kc_s[ns:2 * ns, :] = jnp.sum(x3[:, L_CMP:, :] * w[None], axis=1)
        hi, mid = _split2(vct_ref[...] * wvt_ref[...])
        seg = seg_ref[...]
        vct_s[...] = _dot(hi, seg) + _dot(mid, seg)
        for g in range(KV_A):
            ks_s[g] = jnp.concatenate([ks_ref[:, g * HEAD_DIM:(g + 1) * HEAD_DIM].astype(BF16), ett_ref[...]], axis=1)
            kw_s[g] = jnp.concatenate([kw_ref[:, g * HEAD_DIM:(g + 1) * HEAD_DIM].astype(BF16),
                                       jnp.zeros(ett_ref.shape, BF16)], axis=1)
        for c in range(nt):
            vst_s[c] = vst_ref[:, c * TK:(c + 1) * TK].astype(BF16)
            vwt_s[c] = vwt_ref[:, c * TK:(c + 1) * TK].astype(BF16)

    qt = qt_ref[...]
    q0 = i * TQ

    row_c = lax.broadcasted_iota(jnp.int32, (2 * ns, TQ), 0)
    qpos_c = q0 + lax.broadcasted_iota(jnp.int32, (2 * ns, TQ), 1)
    blk_c = jnp.where(row_c < ns, 2 * row_c, 2 * (row_c - ns) + 1)
    mask_c = qpos_c >= blk_c * L_CMP + (L_CMP - 1)
    blk_t = lax.broadcasted_iota(jnp.int32, (nsp, TQ), 0)
    cur_t = (q0 + lax.broadcasted_iota(jnp.int32, (nsp, TQ), 1)) // SEL_BLOCK
    forced = (blk_t == 0) | ((blk_t <= cur_t) & (blk_t > cur_t - 2))
    kq_gap = (lax.broadcasted_iota(jnp.int32, (TK, TQ), 0) - lax.broadcasted_iota(jnp.int32, (TK, TQ), 1))
    sel_neg = []

    oc_parts = []
    for g in range(KV_A):
        kh, kl = _split2(kc_s[:, g * HEAD_DIM:(g + 1) * HEAD_DIM])
        vcg = vct_s[g * HEAD_DIM:(g + 1) * HEAD_DIM, :].astype(BF16)
        imp = jnp.zeros((2 * ns, TQ), F32)
        for r in range(REP_A):
            h = g * REP_A + r
            qh, ql = _split2(qt[h * HEAD_DIM:(h + 1) * HEAD_DIM, :])
            s = (_dot(kh, qh) + _dot(kl, qh) + _dot(kh, ql)) * SM_SCALE + bct_ref[h]
            s = jnp.where(mask_c, s, NEG)
            e = jnp.exp(s - jnp.max(s, axis=0, keepdims=True))
            p = e / jnp.sum(e, axis=0, keepdims=True)
            p = jnp.where(mask_c, p, 0.0)
            oc_parts.append(_dot(vcg, p.astype(BF16)))
            imp = imp + p
        simp = imp[0:ns, :] + imp[ns:2 * ns, :]
        if nsp > ns:
            simp = jnp.concatenate([simp, jnp.zeros((nsp - ns, TQ), F32)], axis=0)
        score = jnp.where(forced, FORCE_SCORE, simp)
        score = jnp.where(blk_t <= cur_t, score, NEG)
        rank = _rank_select(score, blk_t, ns, 0)
        sel = jnp.where((rank < TOP_K) & (blk_t <= cur_t), 1.0, 0.0)
        sel = jnp.concatenate([sel, jnp.zeros((128 - nsp, TQ), F32)], axis=0)
        sel_neg.append(((sel - 1.0) * (-NEG)).astype(BF16))
    oc_t = jnp.concatenate(oc_parts, axis=0)

    ok1, ok2 = i >= 1, i >= 2
    c1, c2 = jnp.maximum(i - 1, 0), jnp.maximum(i - 2, 0)
    n_far = jnp.maximum(i - 1, 0)
    group = [h // REP_A for h in range(H_A)]
    qs = [jnp.concatenate([(qt[h * HEAD_DIM:(h + 1) * HEAD_DIM, :] * (SM_SCALE * LOG2E)).astype(BF16),
                           sel_neg[group[h]]], axis=0) for h in range(H_A)]

    def kv_sel(g, c):
        r0 = pl.multiple_of(c * TK, TK)
        return ks_s[g, pl.ds(r0, TK), :], vst_s[c, g * HEAD_DIM:(g + 1) * HEAD_DIM, :]

    def kv_win(g, c):
        r0 = pl.multiple_of(c * TK, TK)
        return kw_s[g, pl.ds(r0, TK), :], vwt_s[c, g * HEAD_DIM:(g + 1) * HEAD_DIM, :]

    def far(c, carry):
        kv = [kv_sel(g, c) for g in range(KV_A)]
        scores = [_dot(kv[group[h]][0], qs[h]) for h in range(H_A)]
        return tuple(_softmax_steps(scores, [kv[group[h]][1] for h in range(H_A)], carry))

    init = tuple((jnp.full((1, TQ), NEG, F32), jnp.zeros((1, TQ), F32), jnp.zeros((HEAD_DIM, TQ), F32))
                 for _ in range(H_A))
    sel_st = list(lax.fori_loop(0, n_far, far, init))
    win_st = [None] * H_A
    for c, ok, d_idx in ((c1, ok1, 1), (i, None, 0)):
        kvs = [kv_sel(g, c) for g in range(KV_A)]
        kvw = [kv_win(g, c) for g in range(KV_A)]
        kcat = [jnp.concatenate([kvs[g][0], kvw[g][0]], axis=0) for g in range(KV_A)]
        s2 = [_dot(kcat[group[h]], qs[h]) for h in range(H_A)]
        scores, vts, states = [], [], []
        for h in range(H_A):
            bias = bnear_ref[h, d_idx] if ok is None else jnp.where(ok, bnear_ref[h, d_idx], NEG)
            scores += [s2[h][0:TK] + bias, s2[h][TK:2 * TK] + bias]
            vts += [kvs[group[h]][1], kvw[group[h]][1]]
            states += [sel_st[h], win_st[h]]
        new = _softmax_steps(scores, vts, states)
        sel_st, win_st = new[0::2], new[1::2]
    kvw = [kv_win(g, c2) for g in range(KV_A)]
    far_add = jnp.where((kq_gap > 0) & ok2, 0.0, NEG)
    win_st = _softmax_steps([_dot(kvw[group[h]][0], qs[h]) + far_add for h in range(H_A)],
                            [kvw[group[h]][1] for h in range(H_A)], win_st)
    os_t = jnp.concatenate([acc / l for (_, l, acc) in sel_st], axis=0)
    ow_t = jnp.concatenate([acc / l for (_, l, acc) in win_st], axis=0)

    ge = _dot_exact_lhs(e2t_ref[...], _sigmoid(gt_ref[...]))
    o_t = ge[0:W_A] * oc_t + ge[W_A:2 * W_A] * os_t + ge[2 * W_A:3 * W_A] * ow_t
    o_ref[...] = (o_t * _silu(zat_ref[...])).T


def _attn_prompt(layer, proj_n, proj_t, states, wk, wvt, seg, bnear, bct, ett, e2t, b, t):
    ns, nt = t // SEL_BLOCK, t // TQ
    st_spec = pl.BlockSpec((None, None, KV_COLS, t), lambda bi, i: (layer, bi, 0, 0))

    def tr_spec(rows, row0, width):
        return pl.BlockSpec((None, rows, width), lambda bi, i: (bi, row0 // rows, i))

    def nat_spec(col0):
        return pl.BlockSpec((t, KV_COLS), lambda bi, i: (bi, col0 // KV_COLS))

    def const_spec(a):
        return pl.BlockSpec(a.shape, lambda bi, i: (0,) * a.ndim)

    in_specs = [tr_spec(W_A, TR_Q, TQ), tr_spec(W_A, TR_ZA, TQ), tr_spec(128, TR_GATE, TQ),
                st_spec, st_spec, st_spec,
                nat_spec(NAT_KC), nat_spec(NAT_KS), nat_spec(NAT_KW),
                const_spec(wk), const_spec(wvt), const_spec(seg), const_spec(bnear),
                pl.BlockSpec((H_A, 2 * ns, TQ), lambda bi, i: (0, 0, i)),
                const_spec(ett), const_spec(e2t)]
    scratch = [pltpu.VMEM((2 * ns, KV_COLS), F32), pltpu.VMEM((KV_COLS, 2 * ns), F32),
               pltpu.VMEM((KV_A, t, HEAD_DIM + 128), BF16), pltpu.VMEM((KV_A, t, HEAD_DIM + 128), BF16),
               pltpu.VMEM((nt, KV_COLS, TK), BF16), pltpu.VMEM((nt, KV_COLS, TK), BF16)]
    return pl.pallas_call(
        functools.partial(_attn_prompt_kernel, ns=ns, nt=nt),
        grid=(b, nt),
        in_specs=in_specs,
        out_specs=pl.BlockSpec((TQ, W_A), lambda bi, i: (bi * nt + i, 0)),
        out_shape=jax.ShapeDtypeStruct((b * t, W_A), F32),
        scratch_shapes=scratch,
        compiler_params=_cparams(("arbitrary", "arbitrary")),
        name="attn_prompt",
    )(proj_t, proj_t, proj_t, states[1], states[3], states[5], proj_n, proj_n, proj_n,
      wk, wvt, seg, bnear, bct, ett, e2t)


def _dcmp_kernel(pt_ref, *refs, pages):
    del pt_ref
    k_refs, v_refs = refs[0:pages], refs[pages:2 * pages]
    wkt_ref, wvt_ref, seg_ref = refs[2 * pages:2 * pages + 3]
    ko_ref, vo_ref = refs[2 * pages + 3:]
    for src, w_ref, o_ref in ((k_refs, wkt_ref, ko_ref), (v_refs, wvt_ref, vo_ref)):
        for l in range(o_ref.shape[0]):
            w = w_ref[l]
            parts = [_split2(src[p][l] * w) for p in range(pages)]
            x = jnp.concatenate([hi for hi, _ in parts] + [lo for _, lo in parts], axis=1)
            o_ref[l] = _dot(x, seg_ref[...])


def _decode_compress(page_table, pool_k, pool_v, wkt, wvt, seg, pages):
    depth, db, n_pages = pool_k.shape[0], page_table.shape[0], page_table.shape[1]
    nc = n_pages // pages
    ncol = seg.shape[1]

    def page_spec(p):
        return pl.BlockSpec((depth, None, KV_COLS, PAGE), lambda b, c, pt: (0, pt[b, c * pages + p], 0, 0))

    w_spec = pl.BlockSpec((depth, KV_COLS, PAGE), lambda b, c, pt: (0, 0, 0))
    s_spec = pl.BlockSpec(seg.shape, lambda b, c, pt: (0, 0))
    o_spec = pl.BlockSpec((depth, None, None, KV_COLS, ncol), lambda b, c, pt: (0, b, c, 0, 0))
    o_shape = jax.ShapeDtypeStruct((depth, db, nc, KV_COLS, ncol), F32)
    return pl.pallas_call(
        functools.partial(_dcmp_kernel, pages=pages),
        grid_spec=pltpu.PrefetchScalarGridSpec(
            num_scalar_prefetch=1,
            grid=(db, nc),
            in_specs=[page_spec(p) for p in range(pages)] * 2 + [w_spec, w_spec, s_spec],
            out_specs=[o_spec] * 2),
        out_shape=[o_shape] * 2,
        compiler_params=_cparams(("arbitrary", "arbitrary")),
        name="decode_compress",
    )(page_table, *([pool_k] * pages), *([pool_v] * pages), wkt, wvt, seg)


def _softmax_rows(qbd, kt, add, m, l):
    s = _dot(qbd, kt) + add
    m_new = jnp.maximum(m, jnp.max(s, axis=-1, keepdims=True))
    alpha = jnp.exp2(m - m_new)
    p = jnp.exp2(s - m_new)
    return m_new, alpha, alpha * l + jnp.sum(p, axis=-1, keepdims=True), p.astype(BF16)


def _attn_decode_kernel(pt_ref, *refs, pages, nps, nc, dt):
    del pt_ref
    (q_ref, za_ref, kv_ref, gate_ref, kct_ref, vct_ref) = refs[0:6]
    sk_refs, sv_refs = refs[6:6 + pages], refs[6 + pages:6 + 2 * pages]
    (wink_ref, winv_ref, bcd_ref, blast_ref, bnew_ref, bwin_ref, e_ref, e2_ref,
     o_ref, wk_o, wv_o, qbd_s, m_s, l_s, acc_s, madd_s, oc_s) = refs[6 + 2 * pages:]
    c = pl.program_id(1)
    ck = pages * PAGE
    rows = KV_A * REP_A * dt
    zpad = jnp.zeros((dt, HEAD_DIM), F32)

    @pl.when(c == 0)
    def _():
        q = q_ref[...]
        blocks = []
        for g in range(KV_A):
            for r in range(REP_A):
                h = g * REP_A + r
                piece = q[:, h * HEAD_DIM:(h + 1) * HEAD_DIM]
                blocks.append(jnp.concatenate([piece, zpad] if g == 0 else [zpad, piece], axis=1))
        qbd = jnp.concatenate(blocks, axis=0)
        qbd_s[...] = qbd
        s = _dot_f32(qbd, kct_ref[...]) * SM_SCALE + bcd_ref[...]
        e = jnp.exp(s - jnp.max(s, axis=-1, keepdims=True))
        p = e / jnp.sum(e, axis=-1, keepdims=True)
        oc_s[...] = _dot_nt(p.astype(BF16), vct_ref[...].astype(BF16))
        scores = []
        for g in range(KV_A):
            pe = [p[(g * REP_A + r) * dt:(g * REP_A + r + 1) * dt, 0:nps] for r in range(REP_A)]
            po = [p[(g * REP_A + r) * dt:(g * REP_A + r + 1) * dt, nps:2 * nps] for r in range(REP_A)]
            scores.append((pe[0] + pe[1] + pe[2] + pe[3]) + (po[0] + po[1] + po[2] + po[3]))
        simp = jnp.concatenate(scores, axis=0)
        blk = lax.broadcasted_iota(jnp.int32, simp.shape, 1)
        score = jnp.where((blk == 0) | (blk >= nps - 1), FORCE_SCORE, simp)
        rank = _rank_select(score, blk, nps, 1)
        sel = jnp.where(rank < TOP_K - 1, 1.0, 0.0).astype(BF16)
        madd = (_dot(sel, e_ref[...]) - 1.0) * (-NEG)
        madd = jnp.concatenate([madd[0:dt]] * REP_A + [madd[dt:2 * dt]] * REP_A, axis=0)
        for j in range(nc):
            tile = madd[:, j * ck:(j + 1) * ck]
            if j == nc - 1:
                tile = jnp.concatenate([tile[:, :ck - PAGE], tile[:, ck - PAGE:] + blast_ref[...]], axis=1)
            madd_s[j] = tile
        m_s[...] = jnp.full(m_s.shape, NEG, F32)
        l_s[...] = jnp.zeros(l_s.shape, F32)
        acc_s[...] = jnp.zeros(acc_s.shape, F32)

    qbd = (qbd_s[...] * (SM_SCALE * LOG2E)).astype(BF16)
    kt = jnp.concatenate([r[...].astype(BF16) for r in sk_refs], axis=1)
    vt = jnp.concatenate([r[...].astype(BF16) for r in sv_refs], axis=1)
    m, alpha, l, p = _softmax_rows(qbd, kt, madd_s[c], m_s[...], l_s[...])
    acc = alpha * acc_s[...] + _dot_nt(p, vt)
    m_s[...], l_s[...], acc_s[...] = m, l, acc

    @pl.when(c == nc - 1)
    def _():
        row_t = lax.broadcasted_iota(jnp.int32, (rows, 128), 0) % dt
        col = lax.broadcasted_iota(jnp.int32, (rows, 128), 1)
        new_add = jnp.where((col <= row_t) & (col < dt), bnew_ref[...], NEG)
        row_w = lax.broadcasted_iota(jnp.int32, (rows, WINDOW), 0) % dt
        col_w = lax.broadcasted_iota(jnp.int32, (rows, WINDOW), 1)
        win_add = jnp.where(col_w > row_w, bwin_ref[...], NEG)
        pad = jnp.zeros((128 - dt, KV_COLS), F32)

        def new_tile(src):
            return jnp.concatenate([kv_ref[:, src * KV_COLS:(src + 1) * KV_COLS], pad], axis=0).astype(BF16)

        def attend_nat(kn, vn, add, m, l, acc):
            s = _dot_nt(qbd, kn) + add
            m_new = jnp.maximum(m, jnp.max(s, axis=-1, keepdims=True))
            alpha = jnp.exp2(m - m_new)
            p = jnp.exp2(s - m_new)
            return m_new, alpha * l + jnp.sum(p, axis=-1, keepdims=True), alpha * acc + _dot(p.astype(BF16), vn)

        m2, l2, acc2 = attend_nat(new_tile(2), new_tile(3), new_add, m, l, acc)
        os_full = acc2 / l2
        init = (jnp.full((rows, 1), NEG, F32), jnp.zeros((rows, 1), F32), jnp.zeros((rows, KV_COLS), F32))
        carry = attend_nat(wink_ref[...].astype(BF16), winv_ref[...].astype(BF16), win_add, *init)
        m3, l3, acc3 = attend_nat(new_tile(4), new_tile(5), new_add, *carry)
        ow_full = acc3 / l3

        def heads(full):
            parts = []
            for g in range(KV_A):
                for r in range(REP_A):
                    r0 = (g * REP_A + r) * dt
                    parts.append(full[r0:r0 + dt, g * HEAD_DIM:(g + 1) * HEAD_DIM])
            return jnp.concatenate(parts, axis=1)

        sg = _sigmoid(gate_ref[...])
        ge = _dot_exact_rhs(sg, e2_ref[...])
        o = ge[:, 0:W_A] * heads(oc_s[...]) + ge[:, W_A:2 * W_A] * heads(os_full) + ge[:, 2 * W_A:] * heads(ow_full)
        o_ref[...] = o * _silu(za_ref[...])
        wk_o[...] = jnp.concatenate([wink_ref[dt:, :], kv_ref[:, 4 * KV_COLS:5 * KV_COLS]], axis=0)
        wv_o[...] = jnp.concatenate([winv_ref[dt:, :], kv_ref[:, 5 * KV_COLS:6 * KV_COLS]], axis=0)


def _attn_decode(layer, page_table, proj, kct, vct, pool_sk, pool_sv, win_k, win_v, tabs, e_sel, e2, pages, dt):
    db, n_pages = page_table.shape
    nps = n_pages * (PAGE // SEL_BLOCK)
    nc = n_pages // pages
    ck = pages * PAGE
    rows = KV_A * REP_A * dt
    bcd, blast, bnew, bwin = tabs

    def row_spec(width, col):
        return pl.BlockSpec((dt, width), lambda b, c, pt: (b, col // width))

    def page_spec(p):
        return pl.BlockSpec((None, None, KV_COLS, PAGE), lambda b, c, pt: (layer, pt[b, c * pages + p], 0, 0))

    cmp_spec = pl.BlockSpec((None, None, KV_COLS, 2 * nps), lambda b, c, pt: (layer, b, 0, 0))
    win_spec = pl.BlockSpec((None, None, WINDOW, KV_COLS), lambda b, c, pt: (layer, b, 0, 0))

    def const_spec(a):
        return pl.BlockSpec(a.shape, lambda b, c, pt: (0,) * a.ndim)

    in_specs = ([row_spec(W_A, COL_Q), row_spec(W_A, COL_ZA), row_spec(6 * KV_COLS, COL_KV), row_spec(128, COL_GATE)]
                + [cmp_spec] * 2 + [page_spec(p) for p in range(pages)] * 2 + [win_spec] * 2
                + [const_spec(a) for a in (bcd, blast, bnew, bwin, e_sel, e2)])
    out_specs = [pl.BlockSpec((dt, W_A), lambda b, c, pt: (b, 0)),
                 pl.BlockSpec((None, WINDOW, KV_COLS), lambda b, c, pt: (b, 0, 0)),
                 pl.BlockSpec((None, WINDOW, KV_COLS), lambda b, c, pt: (b, 0, 0))]
    out_shape = [jax.ShapeDtypeStruct((db * dt, W_A), F32),
                 jax.ShapeDtypeStruct((db, WINDOW, KV_COLS), F32),
                 jax.ShapeDtypeStruct((db, WINDOW, KV_COLS), F32)]
    scratch = [pltpu.VMEM((rows, KV_COLS), F32), pltpu.VMEM((rows, 1), F32), pltpu.VMEM((rows, 1), F32),
               pltpu.VMEM((rows, KV_COLS), F32), pltpu.VMEM((nc, rows, ck), F32), pltpu.VMEM((rows, KV_COLS), F32)]
    return pl.pallas_call(
        functools.partial(_attn_decode_kernel, pages=pages, nps=nps, nc=nc, dt=dt),
        grid_spec=pltpu.PrefetchScalarGridSpec(
            num_scalar_prefetch=1, grid=(db, nc), in_specs=in_specs, out_specs=out_specs, scratch_shapes=scratch),
        out_shape=out_shape,
        compiler_params=_cparams(("arbitrary", "arbitrary")),
        name="attn_decode",
    )(page_table, proj, proj, proj, proj, kct, vct, *([pool_sk] * pages), *([pool_sv] * pages),
      win_k, win_v, bcd, blast, bnew, bwin, e_sel, e2)


EXT_B0 = 32
EXT_C0 = 8
CONV_ROWS = 32
SH_PAD = (CONV_B - 1) // 8 * 8


def _mixer_kernel(glu_ref, zb_ref, xc_ref, zc_ref, bufb_ref, bufc_ref, h0_ref,
                  cbw_ref, cbb_ref, gng_ref, gnb_ref, wpw_ref, gones_ref,
                  ccw_ref, ccb_ref, wa_ref, ba_ref, wx_ref, bx_ref, lam_ref,
                  bo_ref, co_ref, cbs_ref, ccs_ref, hs_ref,
                  extb, extc, hcar, cbuf, shb, *, tt):
    j = pl.program_id(1)
    nb, nc = CONV_B - 1, CONV_C - 1

    @pl.when(j == 0)
    def _():
        extb[EXT_B0 - nb:EXT_B0, :] = bufb_ref[...]
        extc[EXT_C0 - nc:EXT_C0, :] = bufc_ref[...]
        hcar[...] = h0_ref[...]

    glu = glu_ref[...]
    extb[EXT_B0:EXT_B0 + tt, :] = glu[:, 0:W_B] * _sigmoid(glu[:, W_B:2 * W_B])
    for r in range(8):
        n_r = tt + (CONV_B - 1 - r) // 8 * 8
        shb[r, 0:n_r, :] = extb[EXT_B0 - nb + r:EXT_B0 - nb + r + n_r, :]
    step = min(CONV_ROWS, tt)
    for r0 in range(0, tt, step):
        acc = jnp.zeros((step, W_B), F32)
        for k in range(CONV_B):
            lo = (k // 8) * 8 + r0
            acc = acc + cbw_ref[k:k + 1, :] * shb[k % 8, lo:lo + step, :]
        cbuf[r0:r0 + step, :] = acc + cbb_ref[...]
    cv = cbuf[...]
    gones = gones_ref[...]
    mu = _dot_exact_rhs(cv, gones) * (1.0 / GN_GROUP)
    d = cv - mu
    var = _dot_exact_rhs(d * d, gones) * (1.0 / GN_GROUP)
    cn = d * lax.rsqrt(var + EPS) * gng_ref[...] + gnb_ref[...]
    bo_ref[...] = _dot(_silu(cn).astype(BF16), wpw_ref[...]) * _silu(zb_ref[...])

    extc[EXT_C0:EXT_C0 + tt, :] = xc_ref[...]
    u = jnp.zeros((tt, W_C), F32)
    for k in range(CONV_C):
        lo = EXT_C0 - nc + k
        u = u + ccw_ref[k:k + 1, :] * extc[lo:lo + tt, :]
    u = u + ccb_ref[...]
    ub = u.astype(BF16)
    r = _sigmoid(_dot(ub, wa_ref[...]) + ba_ref[...])
    ig = _sigmoid(_dot(ub, wx_ref[...]) + bx_ref[...])
    nl = -lam_ref[...]
    softplus = jnp.maximum(nl, 0.0) + jnp.log1p(jnp.exp(-jnp.abs(nl)))
    log_a = -LRU_C * r * softplus
    a = jnp.exp(log_a)
    b = jnp.sqrt(-_expm1(2.0 * log_a)) * (ig * u)
    row = lax.broadcasted_iota(jnp.int32, (tt, W_C), 0)
    s = 1
    while s < tt:
        a_sh = jnp.where(row < s, 1.0, pltpu.roll(a, s, axis=0))
        b_sh = jnp.where(row < s, 0.0, pltpu.roll(b, s, axis=0))
        b = a * b_sh + b
        a = a * a_sh
        s *= 2
    hh = a * hcar[...] + b
    co_ref[...] = hh * _silu(zc_ref[...])
    hcar[...] = hh[tt - 1:tt, :]

    new_b = extb[EXT_B0 + tt - nb:EXT_B0 + tt, :]
    new_c = extc[EXT_C0 + tt - nc:EXT_C0 + tt, :]
    extb[EXT_B0 - nb:EXT_B0, :] = new_b
    extc[EXT_C0 - nc:EXT_C0, :] = new_c

    @pl.when(j == pl.num_programs(1) - 1)
    def _():
        cbs_ref[...] = new_b
        ccs_ref[...] = new_c
        hs_ref[...] = hh[tt - 1:tt, :]


def _mixers(proj, cols, bufb, bufc, h0, lw, gones, b, t, tt):
    nj = t // tt
    c_glu, c_zb, c_xc, c_zc = cols

    def row_spec(width, col):
        return pl.BlockSpec((tt, width), lambda bi, j: (bi * nj + j, col // width))

    def st_spec(n, w):
        return pl.BlockSpec((None, n, w), lambda bi, j: (bi, 0, 0))

    def const_spec(a):
        return pl.BlockSpec(a.shape, lambda bi, j: (0,) * a.ndim)

    consts = [lw["cbw"], lw["cbb"], lw["gng"], lw["gnb"], lw["wpw"], gones,
              lw["ccw"], lw["ccb"], lw["wa"], lw["ba"], lw["wx"], lw["bx"], lw["lam"]]
    in_specs = ([row_spec(2 * W_B, c_glu), row_spec(W_B, c_zb), row_spec(W_C, c_xc), row_spec(W_C, c_zc),
                 st_spec(CONV_B - 1, W_B), st_spec(CONV_C - 1, W_C), st_spec(1, W_C)]
                + [const_spec(a) for a in consts])
    out_specs = [pl.BlockSpec((tt, W_B), lambda bi, j: (bi * nj + j, 0)),
                 pl.BlockSpec((tt, W_C), lambda bi, j: (bi * nj + j, 0)),
                 st_spec(CONV_B - 1, W_B), st_spec(CONV_C - 1, W_C), st_spec(1, W_C)]
    out_shape = [jax.ShapeDtypeStruct((b * t, W_B), F32), jax.ShapeDtypeStruct((b * t, W_C), F32),
                 jax.ShapeDtypeStruct((b, CONV_B - 1, W_B), F32), jax.ShapeDtypeStruct((b, CONV_C - 1, W_C), F32),
                 jax.ShapeDtypeStruct((b, 1, W_C), F32)]
    scratch = [pltpu.VMEM((EXT_B0 + tt, W_B), F32), pltpu.VMEM((EXT_C0 + tt, W_C), F32),
               pltpu.VMEM((1, W_C), F32), pltpu.VMEM((tt, W_B), F32), pltpu.VMEM((8, tt + SH_PAD, W_B), F32)]
    return pl.pallas_call(
        functools.partial(_mixer_kernel, tt=tt),
        grid=(b, nj),
        in_specs=in_specs, out_specs=out_specs, out_shape=out_shape, scratch_shapes=scratch,
        compiler_params=_cparams(("arbitrary", "arbitrary")),
        name="mixers",
    )(proj, proj, proj, proj, bufb, bufc, h0, *consts)


def _block_diag(w):
    nblk, c, d = w.shape
    eye = jnp.eye(nblk, dtype=w.dtype)
    return (eye[:, None, :, None] * w[:, :, None, :]).reshape(nblk * c, nblk * d)


def _row_tile(m, cap, step=8):
    t = min(m, cap)
    while m % t:
        t -= step
    return t


def kernel(x_prompt, x_sample, cache_cmp_k, cache_cmp_v, cache_sel_k, cache_sel_v, cache_win_k, cache_win_v,
           state_conv_b, state_conv_c, state_rglru, page_table, rel_bias, g_pre, g_post, w_in, w_out,
           w_cmp_k, w_cmp_v, conv_b_w, conv_b_b, gn_gain, gn_bias, w_pw_b, conv_c_w, conv_c_b,
           w_lru_a, b_lru_a, w_lru_x, b_lru_x, lru_lambda):
    depth = w_in.shape[0]
    b, t, _ = x_prompt.shape
    db, dt, _ = x_sample.shape
    n_pages = page_table.shape[1]
    past = n_pages * PAGE
    nps = past // SEL_BLOCK
    ns, nt = t // SEL_BLOCK, t // TQ
    assert t % TQ == 0 and TQ == TK and WINDOW == 2 * TK and TOP_K <= ns <= 128 and TOP_K < nps <= 128
    assert dt <= 8 and (past + dt) // L_CMP == past // L_CMP and cache_win_k.shape[2] == WINDOW
    pages = min(32, n_pages)
    pages_c = min(16, n_pages)
    assert n_pages % pages == 0 and n_pages % pages_c == 0
    nc = n_pages // pages_c
    rows = KV_A * REP_A * dt

    zpad = jnp.zeros((depth, D_MODEL, D_IN_PAD - D_IN), w_in.dtype)
    w_r = jnp.concatenate(
        [w_in[..., 0:512], w_in[..., 1304:1816], w_in[..., 1816:2328], w_in[..., 512:1280],
         w_in[..., 2328:3096], w_in[..., 1280:1304], zpad], axis=-1).astype(BF16)
    w_n = jnp.concatenate(
        [w_in[..., 1816:3096], w_in[..., 512:640], w_in[..., 768:896], w_in[..., 1024:1152]], axis=-1).astype(BF16)
    w_it = jnp.swapaxes(w_in, 1, 2)
    w_t = jnp.concatenate(
        [w_it[:, 0:512], w_it[:, 1304:1816], w_it[:, 1280:1304],
         jnp.zeros((depth, N_TR - 1048, D_MODEL), w_in.dtype)], axis=1).astype(BF16)
    w_kv = w_it[:, 512:1280].astype(BF16)
    w_o = w_out.astype(BF16)
    wck = jnp.tile(w_cmp_k, (1, 1, KV_A))
    wvt_p = jnp.tile(jnp.swapaxes(w_cmp_v, 1, 2), (1, KV_A, t // L_CMP))
    wkt_d = jnp.tile(jnp.swapaxes(w_cmp_k, 1, 2), (1, KV_A, PAGE // L_CMP))
    wvt_d = jnp.tile(jnp.swapaxes(w_cmp_v, 1, 2), (1, KV_A, PAGE // L_CMP))
    gones = jnp.asarray(_group_ones_np(), BF16)
    e2 = jnp.asarray(_gate_expand_np(), BF16)
    e2t = jnp.asarray(_gate_expand_np().T, BF16)
    ett = jnp.asarray(_expand_np(ns, 128).T, BF16)
    e_d = jnp.asarray(_expand_np(nps, nps), BF16)
    seg_p = jnp.asarray(_segment_np(t, ns), BF16)
    seg_d = jnp.asarray(np.tile(_segment_np(pages_c * PAGE, pages_c * PAGE // SEL_BLOCK), (2, 1)), BF16)

    kk, qq = np.arange(TK)[:, None], np.arange(TQ)[None, :]
    bk_near = np.concatenate([_bucket_np(d0 + qq - kk) for d0 in (0, TQ)], axis=0)
    bnear = _bias_lookup(rel_bias, bk_near, shift=True, scale=LOG2E).reshape(H_A, 2, TK, TQ)
    bnear = jnp.where(jnp.asarray((kk > qq)[None, None] & (np.arange(2) == 0)[None, :, None, None]), NEG, bnear)
    blk_eo = np.concatenate([2 * np.arange(ns), 2 * np.arange(ns) + 1])[:, None]
    bct = _bias_lookup(rel_bias, _bucket_np(np.arange(t)[None, :] - (blk_eo * L_CMP + L_CMP - 1)))
    qd = past + np.arange(dt)[:, None]
    blk_d = np.concatenate([2 * np.arange(nps), 2 * np.arange(nps) + 1])[None, :]
    bcd = _bias_lookup(rel_bias, _bucket_np(qd - (blk_d * L_CMP + L_CMP - 1))).reshape(rows, 2 * nps)
    blast = _bias_lookup(rel_bias, _bucket_np(qd - (past - PAGE + np.arange(PAGE))[None, :]), shift=True, scale=LOG2E)
    bnew = _bias_lookup(rel_bias, _bucket_np(np.arange(dt)[:, None] - np.arange(128)[None, :]), shift=True, scale=LOG2E)
    bwin = _bias_lookup(rel_bias, _bucket_np(WINDOW + np.arange(dt)[:, None] - np.arange(WINDOW)[None, :]),
                        shift=True, scale=LOG2E)
    dtabs = (bcd, blast.reshape(rows, PAGE), bnew.reshape(rows, 128), bwin.reshape(rows, WINDOW))

    pool = lambda a: jnp.transpose(a, (0, 1, 3, 4, 2)).reshape(a.shape[0], a.shape[1], KV_COLS, PAGE)
    kc_ch, vc_ch = _decode_compress(page_table, pool(cache_cmp_k), pool(cache_cmp_v), wkt_d, wvt_d, seg_d, pages_c)

    def eo_table(x):
        x = x.reshape(depth, db, nc, KV_COLS, 2, nps // nc).transpose(0, 1, 3, 4, 2, 5)
        return x.reshape(depth, db, KV_COLS, 2 * nps)

    kct, vct = eo_table(kc_ch), eo_table(vc_ch)
    pool_sk, pool_sv = pool(cache_sel_k), pool(cache_sel_v)
    win_k = cache_win_k.reshape(depth, db, WINDOW, KV_COLS)
    win_v = cache_win_v.reshape(depth, db, WINDOW, KV_COLS)

    hp = x_prompt.reshape(b * t, D_MODEL)
    hs = x_sample.reshape(db * dt, D_MODEL)
    zeros_b = jnp.zeros((b, CONV_B - 1, W_B), F32)
    zeros_c = jnp.zeros((b, CONV_C - 1, W_C), F32)
    zeros_h = jnp.zeros((b, 1, W_C), F32)
    tm_p, tm_s = _row_tile(t, 512, 128), _row_tile(db * dt, 512)
    tt_p = _row_tile(t, 256)
    nat_cols = (NAT_GLU, NAT_ZB, NAT_XC, NAT_ZC)
    dec_cols = (COL_GLU, COL_ZB, COL_XC, COL_ZC)

    kv_states = [jnp.zeros((depth, b, KV_COLS, t), F32) for _ in range(N_KV)]
    p_states, s_states = [], []
    for l in range(depth):
        row = lambda a: a[l][None, :]
        lw = dict(cbw=conv_b_w[l], cbb=row(conv_b_b), gng=row(gn_gain), gnb=row(gn_bias), wpw=w_pw_b[l].astype(BF16),
                  ccw=conv_c_w[l], ccb=row(conv_c_b), wa=_block_diag(w_lru_a[l]).astype(BF16), ba=row(b_lru_a),
                  wx=_block_diag(w_lru_x[l]).astype(BF16), bx=row(b_lru_x), lam=row(lru_lambda))
        proj_n, proj_t, kv_states = _project_prompt(hp, row(g_pre), w_n[l], w_t[l], w_kv[l], kv_states, l, b, t, tm_p)
        a_out = _attn_prompt(l, proj_n, proj_t, kv_states, wck[l], wvt_p[l], seg_p, bnear, bct, ett, e2t, b, t)
        b_out, c_out, cb, cc, hc = _mixers(proj_n, nat_cols, zeros_b, zeros_c, zeros_h, lw, gones, b, t, tt_p)
        hp = _output(hp, a_out, b_out, c_out, w_o[l], row(g_post), tm_p)
        p_states.append((cb, cc, hc[:, 0]))
        proj = _project(hs, row(g_pre), w_r[l], tm_s)
        a_out, wk_n, wv_n = _attn_decode(l, page_table, proj, kct, vct, pool_sk, pool_sv, win_k, win_v,
                                         dtabs, e_d, e2, pages, dt)
        b_out, c_out, cb, cc, hc = _mixers(proj, dec_cols, state_conv_b[l], state_conv_c[l],
                                           state_rglru[l][:, None, :], lw, gones, db, dt, dt)
        hs = _output(hs, a_out, b_out, c_out, w_o[l], row(g_post), tm_s)
        kv = proj[:, COL_KV:COL_KV + 6 * KV_COLS].reshape(db, dt, 6, KV_A, HEAD_DIM)
        s_states.append((kv[:, :, 0], kv[:, :, 1], kv[:, :, 2], kv[:, :, 3],
                         wk_n.reshape(db, WINDOW, KV_A, HEAD_DIM), wv_n.reshape(db, WINDOW, KV_A, HEAD_DIM),
                         cb, cc, hc[:, 0]))

    cb_p, cc_p, h_p = [jnp.stack(a) for a in zip(*p_states)]
    kv_states = kv_states[:4] + [a[..., t - WINDOW:] for a in kv_states[4:]]
    ck_p, cv_p, sk_p, sv_p, wk_p, wv_p = [
        jnp.transpose(a.reshape(depth, b, KV_A, HEAD_DIM, a.shape[-1]), (0, 1, 4, 2, 3)) for a in kv_states]
    ck_s, cv_s, sk_s, sv_s, wk_s, wv_s, cb_s, cc_s, h_s = [jnp.stack(a) for a in zip(*s_states)]
    return (hp.reshape(b, t, D_MODEL), hs.reshape(db, dt, D_MODEL),
            ck_p, ck_s, cv_p, cv_s, sk_p, sk_s, sv_p, sv_s, wk_p, wk_s, wv_p, wv_s,
            cb_p, cb_s, cc_p, cc_s, h_p, h_s)
```

```python
import functools
import math

import numpy as np
import jax
import jax.numpy as jnp
from jax import lax
from jax.experimental import pallas as pl
from jax.experimental.pallas import tpu as pltpu

F32 = jnp.float32
BF16 = jnp.bfloat16

D_MODEL = 1024
HEAD_DIM = 64
W_A = D_MODEL // 2
W_B = D_MODEL // 4
W_C = D_MODEL // 4
H_A = W_A // HEAD_DIM
KV_A = 2
REP_A = H_A // KV_A
KV_COLS = KV_A * HEAD_DIM
L_CMP = 32
SEL_BLOCK = 64
TOP_K = 16
FORCE_SCORE = 1.0e4
WINDOW = 512
PAGE = 128
CONV_B = 31
CONV_C = 4
GN_GROUP = W_B // 4
LRU_C = 8.0
NUM_BUCKETS = 32
MAX_DISTANCE = 128
SM_SCALE = HEAD_DIM ** -0.5
LOG2E = 1.4426950408889634
EPS = 1e-6
NEG = -1e30

TQ = 256
TK = 256
V_ROWS = HEAD_DIM + 16
D_IN = 3096
D_IN_PAD = 3200
COL_Q, COL_ZA, COL_GLU, COL_KV, COL_ZB, COL_XC, COL_ZC, COL_GATE = 0, 512, 1024, 1536, 2304, 2560, 2816, 3072
N_NAT = 1664
NAT_GLU, NAT_ZB, NAT_XC, NAT_ZC, NAT_KC, NAT_KS, NAT_KW = 0, 512, 768, 1024, 1280, 1408, 1536
N_TR = 1152
TR_Q, TR_ZA, TR_GATE = 0, 512, 1024
N_KV = 6
VMEM_LIMIT = 56 * 1024 * 1024


def _cparams(sem):
    return pltpu.CompilerParams(dimension_semantics=sem, vmem_limit_bytes=VMEM_LIMIT)


def _bucket_np(dist):
    n = np.maximum(dist, 0)
    max_exact = NUM_BUCKETS // 2
    nf = np.maximum(n, 1).astype(np.float32)
    large = max_exact + (np.log(nf / np.float32(max_exact)) / np.float32(math.log(MAX_DISTANCE / max_exact))
                         * np.float32(NUM_BUCKETS - max_exact)).astype(np.int32)
    return np.where(n < max_exact, n, np.minimum(large, NUM_BUCKETS - 1)).astype(np.int32)


def _expand_np(n_blocks, rows):
    e = np.zeros((rows, n_blocks * SEL_BLOCK), np.float32)
    for m in range(n_blocks):
        e[m, m * SEL_BLOCK:(m + 1) * SEL_BLOCK] = 1.0
    return e


def _gate_expand_np():
    e = np.zeros((128, 3 * W_A), np.float32)
    for br in range(3):
        for h in range(H_A):
            e[br * H_A + h, br * W_A + h * HEAD_DIM: br * W_A + (h + 1) * HEAD_DIM] = 1.0
    return e


def _group_ones_np():
    g = np.zeros((W_B, W_B), np.float32)
    for k in range(W_B // GN_GROUP):
        g[k * GN_GROUP:(k + 1) * GN_GROUP, k * GN_GROUP:(k + 1) * GN_GROUP] = 1.0
    return g


def _segment_np(n_rows, n_half):
    s = np.zeros((n_rows, 2 * n_half), np.float32)
    blk = np.arange(n_rows) // L_CMP
    s[np.arange(n_rows), np.where(blk % 2 == 0, blk // 2, n_half + blk // 2)] = 1.0
    return s


def _split2(a):
    hi = a.astype(BF16)
    return hi, (a - hi.astype(F32)).astype(BF16)


def _split3(a):
    hi = a.astype(BF16)
    r1 = a - hi.astype(F32)
    mid = r1.astype(BF16)
    return hi, mid, (r1 - mid.astype(F32)).astype(BF16)


def _dot(a, b):
    return jnp.dot(a, b, preferred_element_type=F32)


def _dot_nt(a, b):
    return lax.dot_general(a, b, (((1,), (1,)), ((), ())), preferred_element_type=F32)


def _dot_exact_rhs(a, b_bf16):
    hi, mid, lo = _split3(a)
    return _dot(hi, b_bf16) + _dot(mid, b_bf16) + _dot(lo, b_bf16)


def _dot_exact_lhs(a_bf16, b):
    hi, mid, lo = _split3(b)
    return _dot(a_bf16, hi) + _dot(a_bf16, mid) + _dot(a_bf16, lo)


def _dot_f32(a, b):
    ah, al = _split2(a)
    bh, bl = _split2(b)
    return _dot(ah, bh) + _dot(al, bh) + _dot(ah, bl)


def _sigmoid(x):
    return 1.0 / (1.0 + jnp.exp(-x))


def _silu(x):
    return x * _sigmoid(x)


def _expm1(x):
    u = jnp.exp(x)
    safe = jnp.where((u == 1.0) | (u == 0.0), 0.5, u)
    return jnp.where(u == 1.0, x, jnp.where(u == 0.0, -1.0, (safe - 1.0) * x / jnp.log(safe)))


def _rank_select(score, idx, n, axis):
    rank = jnp.zeros(score.shape, F32)
    for m in range(n):
        sm = score[m:m + 1, :] if axis == 0 else score[:, m:m + 1]
        beats = (sm > score) | ((sm == score) & (idx > m))
        rank = rank + jnp.where(beats, 1.0, 0.0)
    return rank


def _bias_kernel(rb_ref, bk_ref, o_ref, *, shift, scale):
    h = pl.program_id(0)
    bk = bk_ref[...]
    acc = jnp.zeros(bk.shape, F32)
    for b in range(NUM_BUCKETS):
        acc = jnp.where(bk == b, rb_ref[b, h], acc)
    if shift:
        acc = acc - rb_ref[NUM_BUCKETS - 1, h]
    o_ref[...] = acc * scale


def _bias_lookup(rel_bias, bucket, shift=False, scale=1.0):
    r, c = bucket.shape
    return pl.pallas_call(
        functools.partial(_bias_kernel, shift=shift, scale=scale),
        grid=(H_A,),
        in_specs=[pl.BlockSpec(memory_space=pltpu.SMEM),
                  pl.BlockSpec((r, c), lambda h: (0, 0))],
        out_specs=pl.BlockSpec((None, r, c), lambda h: (h, 0, 0)),
        out_shape=jax.ShapeDtypeStruct((H_A, r, c), F32),
        compiler_params=_cparams(("arbitrary",)),
        name="bias_lookup",
    )(rel_bias, jnp.asarray(bucket))


def _rms_bf16(x_ref, g_ref):
    x = x_ref[...]
    ms = jnp.mean(x * x, axis=-1, keepdims=True)
    return (x * lax.rsqrt(ms + EPS) * g_ref[...]).astype(BF16)


def _proj_kernel(x_ref, g_ref, w_ref, o_ref):
    u = _rms_bf16(x_ref, g_ref)
    step = 640
    for c in range(0, o_ref.shape[1], step):
        o_ref[:, c:c + step] = _dot(u, w_ref[:, c:c + step])


def _project(h, g, w, tm):
    m = h.shape[0]
    return pl.pallas_call(
        _proj_kernel,
        grid=(m // tm,),
        in_specs=[pl.BlockSpec((tm, D_MODEL), lambda i: (i, 0)),
                  pl.BlockSpec((1, D_MODEL), lambda i: (0, 0)),
                  pl.BlockSpec((D_MODEL, D_IN_PAD), lambda i: (0, 0))],
        out_specs=pl.BlockSpec((tm, D_IN_PAD), lambda i: (i, 0)),
        out_shape=jax.ShapeDtypeStruct((m, D_IN_PAD), F32),
        compiler_params=_cparams(("arbitrary",)),
        name="project",
    )(h, g, w)


def _proj2_kernel(x_ref, g_ref, wn_ref, wt_ref, wkv_ref, *refs):
    on_ref, ot_ref = refs[N_KV], refs[N_KV + 1]
    st_refs = refs[N_KV + 2:]
    u = _rms_bf16(x_ref, g_ref)
    for c0, c1 in ((0, 768), (768, N_NAT)):
        on_ref[:, c0:c1] = _dot(u, wn_ref[:, c0:c1])
    for c in range(0, N_TR, 384):
        ot_ref[c:c + 384, :] = _dot_nt(wt_ref[c:c + 384, :], u)
    for k in range(0, N_KV, 2):
        kv = _dot_nt(wkv_ref[k * KV_COLS:(k + 2) * KV_COLS, :], u)
        st_refs[k][...] = kv[0:KV_COLS]
        st_refs[k + 1][...] = kv[KV_COLS:2 * KV_COLS]


def _project_prompt(h, g, wn, wt, wkv, states, layer, b, t, tm):
    per = t // tm
    st_spec = pl.BlockSpec((None, None, KV_COLS, tm), lambda i: (layer, i // per, 0, i % per))
    outs = pl.pallas_call(
        _proj2_kernel,
        grid=(b * per,),
        in_specs=[pl.BlockSpec((tm, D_MODEL), lambda i: (i, 0)),
                  pl.BlockSpec((1, D_MODEL), lambda i: (0, 0)),
                  pl.BlockSpec((D_MODEL, N_NAT), lambda i: (0, 0)),
                  pl.BlockSpec((N_TR, D_MODEL), lambda i: (0, 0)),
                  pl.BlockSpec((N_KV * KV_COLS, D_MODEL), lambda i: (0, 0))]
                 + [pl.BlockSpec(memory_space=pl.ANY)] * N_KV,
        out_specs=[pl.BlockSpec((tm, N_NAT), lambda i: (i, 0)),
                   pl.BlockSpec((None, N_TR, tm), lambda i: (i // per, 0, i % per))] + [st_spec] * N_KV,
        out_shape=[jax.ShapeDtypeStruct((b * t, N_NAT), F32), jax.ShapeDtypeStruct((b, N_TR, t), F32)]
                  + [jax.ShapeDtypeStruct(a.shape, a.dtype) for a in states],
        input_output_aliases={5 + k: 2 + k for k in range(N_KV)},
        compiler_params=_cparams(("arbitrary",)),
        name="project_prompt",
    )(h, g, wn, wt, wkv, *states)
    return outs[0], outs[1], list(outs[2:])


def _out_kernel(h_ref, a_ref, b_ref, c_ref, w_ref, g_ref, o_ref):
    y = (_dot(a_ref[...].astype(BF16), w_ref[0:W_A, :])
         + _dot(b_ref[...].astype(BF16), w_ref[W_A:W_A + W_B, :])
         + _dot(c_ref[...].astype(BF16), w_ref[W_A + W_B:, :]))
    ms = jnp.mean(y * y, axis=-1, keepdims=True)
    o_ref[...] = h_ref[...] + y * lax.rsqrt(ms + EPS) * g_ref[...]


def _output(h, a, b, c, w, g, tm):
    m = h.shape[0]
    return pl.pallas_call(
        _out_kernel,
        grid=(m // tm,),
        in_specs=[pl.BlockSpec((tm, D_MODEL), lambda i: (i, 0)),
                  pl.BlockSpec((tm, W_A), lambda i: (i, 0)),
                  pl.BlockSpec((tm, W_B), lambda i: (i, 0)),
                  pl.BlockSpec((tm, W_C), lambda i: (i, 0)),
                  pl.BlockSpec((D_MODEL, D_MODEL), lambda i: (0, 0)),
                  pl.BlockSpec((1, D_MODEL), lambda i: (0, 0))],
        out_specs=pl.BlockSpec((tm, D_MODEL), lambda i: (i, 0)),
        out_shape=jax.ShapeDtypeStruct((m, D_MODEL), F32),
        compiler_params=_cparams(("arbitrary",)),
        name="output",
    )(h, a, b, c, w, g)


def _softmax_steps(scores, vts, states):
    stats = []
    for s, st in zip(scores, states):
        m_new = jnp.max(s, axis=0, keepdims=True)
        alpha = None
        if st is not None:
            m_new = jnp.maximum(st[0], m_new)
            alpha = jnp.exp2(st[0] - m_new)
        stats.append((m_new, alpha, jnp.exp2(s - m_new).astype(BF16)))
    out = []
    for (m_new, alpha, p), vt, st in zip(stats, vts, states):
        pv = _dot(vt, p)
        out.append((m_new, pv if st is None else alpha * st[1] + pv))
    return out


def _softmax_finish(states):
    return jnp.concatenate([acc[0:HEAD_DIM] / acc[HEAD_DIM:HEAD_DIM + 1] for (_, acc) in states], axis=0)


def _attn_prompt_kernel(qt_ref, zat_ref, gt_ref, vct_ref, vst_ref, vwt_ref, kc_ref, ks_ref, kw_ref,
                        wk_ref, wvt_ref, seg_ref, bnear_ref, bct_ref, ett_ref, e2t_ref,
                        o_ref, kc_s, vct_s, ks_s, kw_s, vst_s, vwt_s, *, ns, nt):
    i = pl.program_id(1)
    nsp = -(-ns // 8) * 8

    @pl.when(i == 0)
    def _():
        x3 = kc_ref[...].reshape(ns, SEL_BLOCK, KV_COLS)
        w = wk_ref[...]
        kc_s[0:ns, :] = jnp.sum(x3[:, :L_CMP, :] * w[None], axis=1)
        kc_s[ns:2 * ns, :] = jnp.sum(x3[:, L_CMP:, :] * w[None], axis=1)
        hi, mid = _split2(vct_ref[...] * wvt_ref[...])
        seg = seg_ref[...]
        vct_s[...] = _dot(hi, seg) + _dot(mid, seg)
        for g in range(KV_A):
            ks_s[g] = jnp.concatenate([ks_ref[:, g * HEAD_DIM:(g + 1) * HEAD_DIM].astype(BF16), ett_ref[...]], axis=1)
            kw_s[g] = jnp.concatenate([kw_ref[:, g * HEAD_DIM:(g + 1) * HEAD_DIM].astype(BF16),
                                       jnp.zeros(ett_ref.shape, BF16)], axis=1)
        ones_row = jnp.where(lax.broadcasted_iota(jnp.int32, (V_ROWS - HEAD_DIM, TK), 0) == 0, 1.0, 0.0).astype(BF16)
        for c in range(nt):
            for g in range(KV_A):
                gs = slice(g * HEAD_DIM, (g + 1) * HEAD_DIM)
                vst_s[c, g] = jnp.concatenate([vst_ref[gs, c * TK:(c + 1) * TK].astype(BF16), ones_row], axis=0)
                vwt_s[c, g] = jnp.concatenate([vwt_ref[gs, c * TK:(c + 1) * TK].astype(BF16), ones_row], axis=0)

    qt = qt_ref[...]
    q0 = i * TQ

    row_c = lax.broadcasted_iota(jnp.int32, (2 * ns, TQ), 0)
    qpos_c = q0 + lax.broadcasted_iota(jnp.int32, (2 * ns, TQ), 1)
    blk_c = jnp.where(row_c < ns, 2 * row_c, 2 * (row_c - ns) + 1)
    mask_c = qpos_c >= blk_c * L_CMP + (L_CMP - 1)
    blk_t = lax.broadcasted_iota(jnp.int32, (nsp, TQ), 0)
    cur_t = (q0 + lax.broadcasted_iota(jnp.int32, (nsp, TQ), 1)) // SEL_BLOCK
    forced = (blk_t == 0) | ((blk_t <= cur_t) & (blk_t > cur_t - 2))
    kq_gap = (lax.broadcasted_iota(jnp.int32, (TK, TQ), 0) - lax.broadcasted_iota(jnp.int32, (TK, TQ), 1))
    sel_neg = []

    oc_parts = []
    for g in range(KV_A):
        kh, kl = _split2(kc_s[:, g * HEAD_DIM:(g + 1) * HEAD_DIM])
        vcg = vct_s[g * HEAD_DIM:(g + 1) * HEAD_DIM, :].astype(BF16)
        imp = jnp.zeros((2 * ns, TQ), F32)
        for r in range(REP_A):
            h = g * REP_A + r
            qh, ql = _split2(qt[h * HEAD_DIM:(h + 1) * HEAD_DIM, :])
            s = (_dot(kh, qh) + _dot(kl, qh) + _dot(kh, ql)) * SM_SCALE + bct_ref[h]
            s = jnp.where(mask_c, s, NEG)
            e = jnp.exp(s - jnp.max(s, axis=0, keepdims=True))
            p = e / jnp.sum(e, axis=0, keepdims=True)
            p = jnp.where(mask_c, p, 0.0)
            oc_parts.append(_dot(vcg, p.astype(BF16)))
            imp = imp + p
        simp = imp[0:ns, :] + imp[ns:2 * ns, :]
        if nsp > ns:
            simp = jnp.concatenate([simp, jnp.zeros((nsp - ns, TQ), F32)], axis=0)
        score = jnp.where(forced, FORCE_SCORE, simp)
        score = jnp.where(blk_t <= cur_t, score, NEG)
        rank = _rank_select(score, blk_t, ns, 0)
        sel = jnp.where((rank < TOP_K) & (blk_t <= cur_t), 1.0, 0.0)
        sel = jnp.concatenate([sel, jnp.zeros((128 - nsp, TQ), F32)], axis=0)
        sel_neg.append(((sel - 1.0) * (-NEG)).astype(BF16))
    oc_t = jnp.concatenate(oc_parts, axis=0)

    ok1, ok2 = i >= 1, i >= 2
    c1, c2 = jnp.maximum(i - 1, 0), jnp.maximum(i - 2, 0)
    n_far = jnp.maximum(i - 1, 0)
    group = [h // REP_A for h in range(H_A)]
    qs = [jnp.concatenate([(qt[h * HEAD_DIM:(h + 1) * HEAD_DIM, :] * (SM_SCALE * LOG2E)).astype(BF16),
                           sel_neg[group[h]]], axis=0) for h in range(H_A)]

    def kv_sel(g, c):
        r0 = pl.multiple_of(c * TK, TK)
        return ks_s[g, pl.ds(r0, TK), :], vst_s[c, g]

    def kv_win(g, c):
        r0 = pl.multiple_of(c * TK, TK)
        return kw_s[g, pl.ds(r0, TK), :], vwt_s[c, g]

    def far(c, carry):
        kv = [kv_sel(g, c) for g in range(KV_A)]
        scores = [_dot(kv[group[h]][0], qs[h]) for h in range(H_A)]
        return tuple(_softmax_steps(scores, [kv[group[h]][1] for h in range(H_A)], carry))

    init = tuple((jnp.full((1, TQ), NEG, F32), jnp.zeros((V_ROWS, TQ), F32)) for _ in range(H_A))
    sel_st = list(lax.fori_loop(0, n_far, far, init))
    win_st = [None] * H_A
    for c, ok, d_idx in ((c1, ok1, 1), (i, None, 0)):
        kvs = [kv_sel(g, c) for g in range(KV_A)]
        kvw = [kv_win(g, c) for g in range(KV_A)]
        kcat = [jnp.concatenate([kvs[g][0], kvw[g][0]], axis=0) for g in range(KV_A)]
        s2 = [_dot(kcat[group[h]], qs[h]) for h in range(H_A)]
        scores, vts, states = [], [], []
        for h in range(H_A):
            bias = bnear_ref[h, d_idx] if ok is None else jnp.where(ok, bnear_ref[h, d_idx], NEG)
            scores += [s2[h][0:TK] + bias, s2[h][TK:2 * TK] + bias]
            vts += [kvs[group[h]][1], kvw[group[h]][1]]
            states += [sel_st[h], win_st[h]]
        new = _softmax_steps(scores, vts, states)
        sel_st, win_st = new[0::2], new[1::2]
    kvw = [kv_win(g, c2) for g in range(KV_A)]
    far_add = jnp.where((kq_gap > 0) & ok2, 0.0, NEG)
    win_st = _softmax_steps([_dot(kvw[group[h]][0], qs[h]) + far_add for h in range(H_A)],
                            [kvw[group[h]][1] for h in range(H_A)], win_st)
    os_t = _softmax_finish(sel_st)
    ow_t = _softmax_finish(win_st)

    ge = _dot_exact_lhs(e2t_ref[...], _sigmoid(gt_ref[...]))
    o_t = ge[0:W_A] * oc_t + ge[W_A:2 * W_A] * os_t + ge[2 * W_A:3 * W_A] * ow_t
    o_ref[...] = (o_t * _silu(zat_ref[...])).T


def _attn_prompt(layer, proj_n, proj_t, states, wk, wvt, seg, bnear, bct, ett, e2t, b, t):
    ns, nt = t // SEL_BLOCK, t // TQ
    st_spec = pl.BlockSpec((None, None, KV_COLS, t), lambda bi, i: (layer, bi, 0, 0))

    def tr_spec(rows, row0, width):
        return pl.BlockSpec((None, rows, width), lambda bi, i: (bi, row0 // rows, i))

    def nat_spec(col0):
        return pl.BlockSpec((t, KV_COLS), lambda bi, i: (bi, col0 // KV_COLS))

    def const_spec(a):
        return pl.BlockSpec(a.shape, lambda bi, i: (0,) * a.ndim)

    in_specs = [tr_spec(W_A, TR_Q, TQ), tr_spec(W_A, TR_ZA, TQ), tr_spec(128, TR_GATE, TQ),
                st_spec, st_spec, st_spec,
                nat_spec(NAT_KC), nat_spec(NAT_KS), nat_spec(NAT_KW),
                const_spec(wk), const_spec(wvt), const_spec(seg), const_spec(bnear),
                pl.BlockSpec((H_A, 2 * ns, TQ), lambda bi, i: (0, 0, i)),
                const_spec(ett), const_spec(e2t)]
    scratch = [pltpu.VMEM((2 * ns, KV_COLS), F32), pltpu.VMEM((KV_COLS, 2 * ns), F32),
               pltpu.VMEM((KV_A, t, HEAD_DIM + 128), BF16), pltpu.VMEM((KV_A, t, HEAD_DIM + 128), BF16),
               pltpu.VMEM((nt, KV_A, V_ROWS, TK), BF16), pltpu.VMEM((nt, KV_A, V_ROWS, TK), BF16)]
    return pl.pallas_call(
        functools.partial(_attn_prompt_kernel, ns=ns, nt=nt),
        grid=(b, nt),
        in_specs=in_specs,
        out_specs=pl.BlockSpec((TQ, W_A), lambda bi, i: (bi * nt + i, 0)),
        out_shape=jax.ShapeDtypeStruct((b * t, W_A), F32),
        scratch_shapes=scratch,
        compiler_params=_cparams(("arbitrary", "arbitrary")),
        name="attn_prompt",
    )(proj_t, proj_t, proj_t, states[1], states[3], states[5], proj_n, proj_n, proj_n,
      wk, wvt, seg, bnear, bct, ett, e2t)


def _dcmp_kernel(pt_ref, *refs, pages):
    del pt_ref
    k_refs, v_refs = refs[0:pages], refs[pages:2 * pages]
    wkt_ref, wvt_ref, seg_ref = refs[2 * pages:2 * pages + 3]
    ko_ref, vo_ref = refs[2 * pages + 3:]
    for src, w_ref, o_ref in ((k_refs, wkt_ref, ko_ref), (v_refs, wvt_ref, vo_ref)):
        for l in range(o_ref.shape[0]):
            w = w_ref[l]
            parts = [_split2(src[p][l] * w) for p in range(pages)]
            x = jnp.concatenate([hi for hi, _ in parts] + [lo for _, lo in parts], axis=1)
            o_ref[l] = _dot(x, seg_ref[...])


def _decode_compress(page_table, pool_k, pool_v, wkt, wvt, seg, pages):
    depth, db, n_pages = pool_k.shape[0], page_table.shape[0], page_table.shape[1]
    nc = n_pages // pages
    ncol = seg.shape[1]

    def page_spec(p):
        return pl.BlockSpec((depth, None, KV_COLS, PAGE), lambda b, c, pt: (0, pt[b, c * pages + p], 0, 0))

    w_spec = pl.BlockSpec((depth, KV_COLS, PAGE), lambda b, c, pt: (0, 0, 0))
    s_spec = pl.BlockSpec(seg.shape, lambda b, c, pt: (0, 0))
    o_spec = pl.BlockSpec((depth, None, None, KV_COLS, ncol), lambda b, c, pt: (0, b, c, 0, 0))
    o_shape = jax.ShapeDtypeStruct((depth, db, nc, KV_COLS, ncol), F32)
    return pl.pallas_call(
        functools.partial(_dcmp_kernel, pages=pages),
        grid_spec=pltpu.PrefetchScalarGridSpec(
            num_scalar_prefetch=1,
            grid=(db, nc),
            in_specs=[page_spec(p) for p in range(pages)] * 2 + [w_spec, w_spec, s_spec],
            out_specs=[o_spec] * 2),
        out_shape=[o_shape] * 2,
        compiler_params=_cparams(("arbitrary", "arbitrary")),
        name="decode_compress",
    )(page_table, *([pool_k] * pages), *([pool_v] * pages), wkt, wvt, seg)


def _softmax_rows(qbd, kt, add, m, l):
    s = _dot(qbd, kt) + add
    m_new = jnp.maximum(m, jnp.max(s, axis=-1, keepdims=True))
    alpha = jnp.exp2(m - m_new)
    p = jnp.exp2(s - m_new)
    return m_new, alpha, alpha * l + jnp.sum(p, axis=-1, keepdims=True), p.astype(BF16)


def _attn_decode_kernel(pt_ref, *refs, pages, nps, nc, dt):
    del pt_ref
    (q_ref, za_ref, kv_ref, gate_ref, kct_ref, vct_ref) = refs[0:6]
    sk_refs, sv_refs = refs[6:6 + pages], refs[6 + pages:6 + 2 * pages]
    (wink_ref, winv_ref, bcd_ref, blast_ref, bnew_ref, bwin_ref, e_ref, e2_ref,
     o_ref, wk_o, wv_o, qbd_s, m_s, l_s, acc_s, madd_s, oc_s) = refs[6 + 2 * pages:]
    c = pl.program_id(1)
    ck = pages * PAGE
    rows = KV_A * REP_A * dt
    zpad = jnp.zeros((dt, HEAD_DIM), F32)

    @pl.when(c == 0)
    def _():
        q = q_ref[...]
        blocks = []
        for g in range(KV_A):
            for r in range(REP_A):
                h = g * REP_A + r
                piece = q[:, h * HEAD_DIM:(h + 1) * HEAD_DIM]
                blocks.append(jnp.concatenate([piece, zpad] if g == 0 else [zpad, piece], axis=1))
        qbd = jnp.concatenate(blocks, axis=0)
        qbd_s[...] = qbd
        s = _dot_f32(qbd, kct_ref[...]) * SM_SCALE + bcd_ref[...]
        e = jnp.exp(s - jnp.max(s, axis=-1, keepdims=True))
        p = e / jnp.sum(e, axis=-1, keepdims=True)
        oc_s[...] = _dot_nt(p.astype(BF16), vct_ref[...].astype(BF16))
        scores = []
        for g in range(KV_A):
            pe = [p[(g * REP_A + r) * dt:(g * REP_A + r + 1) * dt, 0:nps] for r in range(REP_A)]
            po = [p[(g * REP_A + r) * dt:(g * REP_A + r + 1) * dt, nps:2 * nps] for r in range(REP_A)]
            scores.append((pe[0] + pe[1] + pe[2] + pe[3]) + (po[0] + po[1] + po[2] + po[3]))
        simp = jnp.concatenate(scores, axis=0)
        blk = lax.broadcasted_iota(jnp.int32, simp.shape, 1)
        score = jnp.where((blk == 0) | (blk >= nps - 1), FORCE_SCORE, simp)
        rank = _rank_select(score, blk, nps, 1)
        sel = jnp.where(rank < TOP_K - 1, 1.0, 0.0).astype(BF16)
        madd = (_dot(sel, e_ref[...]) - 1.0) * (-NEG)
        madd = jnp.concatenate([madd[0:dt]] * REP_A + [madd[dt:2 * dt]] * REP_A, axis=0)
        for j in range(nc):
            tile = madd[:, j * ck:(j + 1) * ck]
            if j == nc - 1:
                tile = jnp.concatenate([tile[:, :ck - PAGE], tile[:, ck - PAGE:] + blast_ref[...]], axis=1)
            madd_s[j] = tile
        m_s[...] = jnp.full(m_s.shape, NEG, F32)
        l_s[...] = jnp.zeros(l_s.shape, F32)
        acc_s[...] = jnp.zeros(acc_s.shape, F32)

    qbd = (qbd_s[...] * (SM_SCALE * LOG2E)).astype(BF16)
    kt = jnp.concatenate([r[...].astype(BF16) for r in sk_refs], axis=1)
    vt = jnp.concatenate([r[...].astype(BF16) for r in sv_refs], axis=1)
    m, alpha, l, p = _softmax_rows(qbd, kt, madd_s[c], m_s[...], l_s[...])
    acc = alpha * acc_s[...] + _dot_nt(p, vt)
    m_s[...], l_s[...], acc_s[...] = m, l, acc

    @pl.when(c == nc - 1)
    def _():
        row_t = lax.broadcasted_iota(jnp.int32, (rows, 128), 0) % dt
        col = lax.broadcasted_iota(jnp.int32, (rows, 128), 1)
        new_add = jnp.where((col <= row_t) & (col < dt), bnew_ref[...], NEG)
        row_w = lax.broadcasted_iota(jnp.int32, (rows, WINDOW), 0) % dt
        col_w = lax.broadcasted_iota(jnp.int32, (rows, WINDOW), 1)
        win_add = jnp.where(col_w > row_w, bwin_ref[...], NEG)
        pad = jnp.zeros((128 - dt, KV_COLS), F32)

        def new_tile(src):
            return jnp.concatenate([kv_ref[:, src * KV_COLS:(src + 1) * KV_COLS], pad], axis=0).astype(BF16)

        def attend_nat(kn, vn, add, m, l, acc):
            s = _dot_nt(qbd, kn) + add
            m_new = jnp.maximum(m, jnp.max(s, axis=-1, keepdims=True))
            alpha = jnp.exp2(m - m_new)
            p = jnp.exp2(s - m_new)
            return m_new, alpha * l + jnp.sum(p, axis=-1, keepdims=True), alpha * acc + _dot(p.astype(BF16), vn)

        m2, l2, acc2 = attend_nat(new_tile(2), new_tile(3), new_add, m, l, acc)
        os_full = acc2 / l2
        init = (jnp.full((rows, 1), NEG, F32), jnp.zeros((rows, 1), F32), jnp.zeros((rows, KV_COLS), F32))
        m_w, _, l_w, p_w = _softmax_rows(qbd, wink_ref[...].astype(BF16), win_add, init[0], init[1])
        carry = (m_w, l_w, _dot_nt(p_w, winv_ref[...].astype(BF16)))
        m3, l3, acc3 = attend_nat(new_tile(4), new_tile(5), new_add, *carry)
        ow_full = acc3 / l3

        def heads(full):
            parts = []
            for g in range(KV_A):
                for r in range(REP_A):
                    r0 = (g * REP_A + r) * dt
                    parts.append(full[r0:r0 + dt, g * HEAD_DIM:(g + 1) * HEAD_DIM])
            return jnp.concatenate(parts, axis=1)

        sg = _sigmoid(gate_ref[...])
        ge = _dot_exact_rhs(sg, e2_ref[...])
        o = ge[:, 0:W_A] * heads(oc_s[...]) + ge[:, W_A:2 * W_A] * heads(os_full) + ge[:, 2 * W_A:] * heads(ow_full)
        o_ref[...] = o * _silu(za_ref[...])
        for src, w_ref, wo_ref in ((4, wink_ref, wk_o), (5, winv_ref, wv_o)):
            new_t = jnp.concatenate([kv_ref[:, src * KV_COLS:(src + 1) * KV_COLS], pad], axis=0).T
            wo_ref[...] = jnp.concatenate([w_ref[:, dt:], new_t[:, 0:dt]], axis=1)


def _attn_decode(layer, page_table, proj, kct, vct, pool_sk, pool_sv, win_k, win_v, tabs, e_sel, e2, pages, dt):
    db, n_pages = page_table.shape
    nps = n_pages * (PAGE // SEL_BLOCK)
    nc = n_pages // pages
    ck = pages * PAGE
    rows = KV_A * REP_A * dt
    bcd, blast, bnew, bwin = tabs

    def row_spec(width, col):
        return pl.BlockSpec((dt, width), lambda b, c, pt: (b, col // width))

    def page_spec(p):
        return pl.BlockSpec((None, None, KV_COLS, PAGE), lambda b, c, pt: (layer, pt[b, c * pages + p], 0, 0))

    cmp_spec = pl.BlockSpec((None, None, KV_COLS, 2 * nps), lambda b, c, pt: (layer, b, 0, 0))
    win_spec = pl.BlockSpec((None, None, KV_COLS, WINDOW), lambda b, c, pt: (layer, b, 0, 0))

    def const_spec(a):
        return pl.BlockSpec(a.shape, lambda b, c, pt: (0,) * a.ndim)

    in_specs = ([row_spec(W_A, COL_Q), row_spec(W_A, COL_ZA), row_spec(6 * KV_COLS, COL_KV), row_spec(128, COL_GATE)]
                + [cmp_spec] * 2 + [page_spec(p) for p in range(pages)] * 2 + [win_spec] * 2
                + [const_spec(a) for a in (bcd, blast, bnew, bwin, e_sel, e2)])
    out_specs = [pl.BlockSpec((dt, W_A), lambda b, c, pt: (b, 0)),
                 pl.BlockSpec((None, KV_COLS, WINDOW), lambda b, c, pt: (b, 0, 0)),
                 pl.BlockSpec((None, KV_COLS, WINDOW), lambda b, c, pt: (b, 0, 0))]
    out_shape = [jax.ShapeDtypeStruct((db * dt, W_A), F32),
                 jax.ShapeDtypeStruct((db, KV_COLS, WINDOW), F32),
                 jax.ShapeDtypeStruct((db, KV_COLS, WINDOW), F32)]
    scratch = [pltpu.VMEM((rows, KV_COLS), F32), pltpu.VMEM((rows, 1), F32), pltpu.VMEM((rows, 1), F32),
               pltpu.VMEM((rows, KV_COLS), F32), pltpu.VMEM((nc, rows, ck), F32), pltpu.VMEM((rows, KV_COLS), F32)]
    return pl.pallas_call(
        functools.partial(_attn_decode_kernel, pages=pages, nps=nps, nc=nc, dt=dt),
        grid_spec=pltpu.PrefetchScalarGridSpec(
            num_scalar_prefetch=1, grid=(db, nc), in_specs=in_specs, out_specs=out_specs, scratch_shapes=scratch),
        out_shape=out_shape,
        compiler_params=_cparams(("arbitrary", "arbitrary")),
        name="attn_decode",
    )(page_table, proj, proj, proj, proj, kct, vct, *([pool_sk] * pages), *([pool_sv] * pages),
      win_k, win_v, bcd, blast, bnew, bwin, e_sel, e2)


EXT_B0 = 32
EXT_C0 = 8
CONV_ROWS = 32
SH_PAD = (CONV_B - 1) // 8 * 8


def _mixer_kernel(glu_ref, zb_ref, xc_ref, zc_ref, bufb_ref, bufc_ref, h0_ref,
                  cbw_ref, cbb_ref, gng_ref, gnb_ref, wpw_ref, gones_ref,
                  ccw_ref, ccb_ref, wa_ref, ba_ref, wx_ref, bx_ref, lam_ref,
                  bo_ref, co_ref, cbs_ref, ccs_ref, hs_ref,
                  extb, extc, hcar, cbuf, shb, *, tt):
    j = pl.program_id(1)
    nb, nc = CONV_B - 1, CONV_C - 1

    @pl.when(j == 0)
    def _():
        extb[EXT_B0 - nb:EXT_B0, :] = bufb_ref[...]
        extc[EXT_C0 - nc:EXT_C0, :] = bufc_ref[...]
        hcar[...] = h0_ref[...]

    glu = glu_ref[...]
    extb[EXT_B0:EXT_B0 + tt, :] = glu[:, 0:W_B] * _sigmoid(glu[:, W_B:2 * W_B])
    for r in range(8):
        n_r = tt + (CONV_B - 1 - r) // 8 * 8
        shb[r, 0:n_r, :] = extb[EXT_B0 - nb + r:EXT_B0 - nb + r + n_r, :]
    step = min(CONV_ROWS, tt)
    for r0 in range(0, tt, step):
        acc = jnp.zeros((step, W_B), F32)
        for k in range(CONV_B):
            lo = (k // 8) * 8 + r0
            acc = acc + cbw_ref[k:k + 1, :] * shb[k % 8, lo:lo + step, :]
        cbuf[r0:r0 + step, :] = acc + cbb_ref[...]
    cv = cbuf[...]
    gones = gones_ref[...]
    mu = _dot_exact_rhs(cv, gones) * (1.0 / GN_GROUP)
    d = cv - mu
    var = _dot_exact_rhs(d * d, gones) * (1.0 / GN_GROUP)
    cn = d * lax.rsqrt(var + EPS) * gng_ref[...] + gnb_ref[...]
    bo_ref[...] = _dot(_silu(cn).astype(BF16), wpw_ref[...]) * _silu(zb_ref[...])

    extc[EXT_C0:EXT_C0 + tt, :] = xc_ref[...]
    u = jnp.zeros((tt, W_C), F32)
    for k in range(CONV_C):
        lo = EXT_C0 - nc + k
        u = u + ccw_ref[k:k + 1, :] * extc[lo:lo + tt, :]
    u = u + ccb_ref[...]
    ub = u.astype(BF16)
    r = _sigmoid(_dot(ub, wa_ref[...]) + ba_ref[...])
    ig = _sigmoid(_dot(ub, wx_ref[...]) + bx_ref[...])
    nl = -lam_ref[...]
    softplus = jnp.maximum(nl, 0.0) + jnp.log1p(jnp.exp(-jnp.abs(nl)))
    log_a = -LRU_C * r * softplus
    a = jnp.exp(log_a)
    b = jnp.sqrt(-_expm1(2.0 * log_a)) * (ig * u)
    row = lax.broadcasted_iota(jnp.int32, (tt, W_C), 0)
    s = 1
    while s < tt:
        a_sh = jnp.where(row < s, 1.0, pltpu.roll(a, s, axis=0))
        b_sh = jnp.where(row < s, 0.0, pltpu.roll(b, s, axis=0))
        b = a * b_sh + b
        a = a * a_sh
        s *= 2
    hh = a * hcar[...] + b
    co_ref[...] = hh * _silu(zc_ref[...])
    hcar[...] = hh[tt - 1:tt, :]

    new_b = extb[EXT_B0 + tt - nb:EXT_B0 + tt, :]
    new_c = extc[EXT_C0 + tt - nc:EXT_C0 + tt, :]
    extb[EXT_B0 - nb:EXT_B0, :] = new_b
    extc[EXT_C0 - nc:EXT_C0, :] = new_c

    @pl.when(j == pl.num_programs(1) - 1)
    def _():
        cbs_ref[...] = new_b
        ccs_ref[...] = new_c
        hs_ref[...] = hh[tt - 1:tt, :]


def _mixers(proj, cols, bufb, bufc, h0, lw, gones, b, t, tt):
    nj = t // tt
    c_glu, c_zb, c_xc, c_zc = cols

    def row_spec(width, col):
        return pl.BlockSpec((tt, width), lambda bi, j: (bi * nj + j, col // width))

    def st_spec(n, w):
        return pl.BlockSpec((None, n, w), lambda bi, j: (bi, 0, 0))

    def const_spec(a):
        return pl.BlockSpec(a.shape, lambda bi, j: (0,) * a.ndim)

    consts = [lw["cbw"], lw["cbb"], lw["gng"], lw["gnb"], lw["wpw"], gones,
              lw["ccw"], lw["ccb"], lw["wa"], lw["ba"], lw["wx"], lw["bx"], lw["lam"]]
    in_specs = ([row_spec(2 * W_B, c_glu), row_spec(W_B, c_zb), row_spec(W_C, c_xc), row_spec(W_C, c_zc),
                 st_spec(CONV_B - 1, W_B), st_spec(CONV_C - 1, W_C), st_spec(1, W_C)]
                + [const_spec(a) for a in consts])
    out_specs = [pl.BlockSpec((tt, W_B), lambda bi, j: (bi * nj + j, 0)),
                 pl.BlockSpec((tt, W_C), lambda bi, j: (bi * nj + j, 0)),
                 st_spec(CONV_B - 1, W_B), st_spec(CONV_C - 1, W_C), st_spec(1, W_C)]
    out_shape = [jax.ShapeDtypeStruct((b * t, W_B), F32), jax.ShapeDtypeStruct((b * t, W_C), F32),
                 jax.ShapeDtypeStruct((b, CONV_B - 1, W_B), F32), jax.ShapeDtypeStruct((b, CONV_C - 1, W_C), F32),
                 jax.ShapeDtypeStruct((b, 1, W_C), F32)]
    scratch = [pltpu.VMEM((EXT_B0 + tt, W_B), F32), pltpu.VMEM((EXT_C0 + tt, W_C), F32),
               pltpu.VMEM((1, W_C), F32), pltpu.VMEM((tt, W_B), F32), pltpu.VMEM((8, tt + SH_PAD, W_B), F32)]
    return pl.pallas_call(
        functools.partial(_mixer_kernel, tt=tt),
        grid=(b, nj),
        in_specs=in_specs, out_specs=out_specs, out_shape=out_shape, scratch_shapes=scratch,
        compiler_params=_cparams(("arbitrary", "arbitrary")),
        name="mixers",
    )(proj, proj, proj, proj, bufb, bufc, h0, *consts)


def _block_diag(w):
    nblk, c, d = w.shape
    eye = jnp.eye(nblk, dtype=w.dtype)
    return (eye[:, None, :, None] * w[:, :, None, :]).reshape(nblk * c, nblk * d)


def _row_tile(m, cap, step=8):
    t = min(m, cap)
    while m % t:
        t -= step
    return t


def kernel(x_prompt, x_sample, cache_cmp_k, cache_cmp_v, cache_sel_k, cache_sel_v, cache_win_k, cache_win_v,
           state_conv_b, state_conv_c, state_rglru, page_table, rel_bias, g_pre, g_post, w_in, w_out,
           w_cmp_k, w_cmp_v, conv_b_w, conv_b_b, gn_gain, gn_bias, w_pw_b, conv_c_w, conv_c_b,
           w_lru_a, b_lru_a, w_lru_x, b_lru_x, lru_lambda):
    depth = w_in.shape[0]
    b, t, _ = x_prompt.shape
    db, dt, _ = x_sample.shape
    n_pages = page_table.shape[1]
    past = n_pages * PAGE
    nps = past // SEL_BLOCK
    ns, nt = t // SEL_BLOCK, t // TQ
    assert t % TQ == 0 and TQ == TK and WINDOW == 2 * TK and TOP_K <= ns <= 128 and TOP_K < nps <= 128
    assert dt <= 8 and (past + dt) // L_CMP == past // L_CMP and cache_win_k.shape[2] == WINDOW
    pages = min(32, n_pages)
    pages_c = min(16, n_pages)
    assert n_pages % pages == 0 and n_pages % pages_c == 0
    nc = n_pages // pages_c
    rows = KV_A * REP_A * dt

    zpad = jnp.zeros((depth, D_MODEL, D_IN_PAD - D_IN), w_in.dtype)
    w_r = jnp.concatenate(
        [w_in[..., 0:512], w_in[..., 1304:1816], w_in[..., 1816:2328], w_in[..., 512:1280],
         w_in[..., 2328:3096], w_in[..., 1280:1304], zpad], axis=-1).astype(BF16)
    w_n = jnp.concatenate(
        [w_in[..., 1816:3096], w_in[..., 512:640], w_in[..., 768:896], w_in[..., 1024:1152]], axis=-1).astype(BF16)
    w_it = jnp.swapaxes(w_in, 1, 2)
    w_t = jnp.concatenate(
        [w_it[:, 0:512], w_it[:, 1304:1816], w_it[:, 1280:1304],
         jnp.zeros((depth, N_TR - 1048, D_MODEL), w_in.dtype)], axis=1).astype(BF16)
    w_kv = w_it[:, 512:1280].astype(BF16)
    w_o = w_out.astype(BF16)
    wck = jnp.tile(w_cmp_k, (1, 1, KV_A))
    wvt_p = jnp.tile(jnp.swapaxes(w_cmp_v, 1, 2), (1, KV_A, t // L_CMP))
    wkt_d = jnp.tile(jnp.swapaxes(w_cmp_k, 1, 2), (1, KV_A, PAGE // L_CMP))
    wvt_d = jnp.tile(jnp.swapaxes(w_cmp_v, 1, 2), (1, KV_A, PAGE // L_CMP))
    gones = jnp.asarray(_group_ones_np(), BF16)
    e2 = jnp.asarray(_gate_expand_np(), BF16)
    e2t = jnp.asarray(_gate_expand_np().T, BF16)
    ett = jnp.asarray(_expand_np(ns, 128).T, BF16)
    e_d = jnp.asarray(_expand_np(nps, nps), BF16)
    seg_p = jnp.asarray(_segment_np(t, ns), BF16)
    seg_d = jnp.asarray(np.tile(_segment_np(pages_c * PAGE, pages_c * PAGE // SEL_BLOCK), (2, 1)), BF16)

    kk, qq = np.arange(TK)[:, None], np.arange(TQ)[None, :]
    bk_near = np.concatenate([_bucket_np(d0 + qq - kk) for d0 in (0, TQ)], axis=0)
    bnear = _bias_lookup(rel_bias, bk_near, shift=True, scale=LOG2E).reshape(H_A, 2, TK, TQ)
    bnear = jnp.where(jnp.asarray((kk > qq)[None, None] & (np.arange(2) == 0)[None, :, None, None]), NEG, bnear)
    blk_eo = np.concatenate([2 * np.arange(ns), 2 * np.arange(ns) + 1])[:, None]
    bct = _bias_lookup(rel_bias, _bucket_np(np.arange(t)[None, :] - (blk_eo * L_CMP + L_CMP - 1)))
    qd = past + np.arange(dt)[:, None]
    blk_d = np.concatenate([2 * np.arange(nps), 2 * np.arange(nps) + 1])[None, :]
    bcd = _bias_lookup(rel_bias, _bucket_np(qd - (blk_d * L_CMP + L_CMP - 1))).reshape(rows, 2 * nps)
    blast = _bias_lookup(rel_bias, _bucket_np(qd - (past - PAGE + np.arange(PAGE))[None, :]), shift=True, scale=LOG2E)
    bnew = _bias_lookup(rel_bias, _bucket_np(np.arange(dt)[:, None] - np.arange(128)[None, :]), shift=True, scale=LOG2E)
    bwin = _bias_lookup(rel_bias, _bucket_np(WINDOW + np.arange(dt)[:, None] - np.arange(WINDOW)[None, :]),
                        shift=True, scale=LOG2E)
    dtabs = (bcd, blast.reshape(rows, PAGE), bnew.reshape(rows, 128), bwin.reshape(rows, WINDOW))

    pool = lambda a: jnp.transpose(a, (0, 1, 3, 4, 2)).reshape(a.shape[0], a.shape[1], KV_COLS, PAGE)
    kc_ch, vc_ch = _decode_compress(page_table, pool(cache_cmp_k), pool(cache_cmp_v), wkt_d, wvt_d, seg_d, pages_c)

    def eo_table(x):
        x = x.reshape(depth, db, nc, KV_COLS, 2, nps // nc).transpose(0, 1, 3, 4, 2, 5)
        return x.reshape(depth, db, KV_COLS, 2 * nps)

    kct, vct = eo_table(kc_ch), eo_table(vc_ch)
    pool_sk, pool_sv = pool(cache_sel_k), pool(cache_sel_v)
    win_k = jnp.transpose(cache_win_k, (0, 1, 3, 4, 2)).reshape(depth, db, KV_COLS, WINDOW)
    win_v = jnp.transpose(cache_win_v, (0, 1, 3, 4, 2)).reshape(depth, db, KV_COLS, WINDOW)

    hp = x_prompt.reshape(b * t, D_MODEL)
    hs = x_sample.reshape(db * dt, D_MODEL)
    zeros_b = jnp.zeros((b, CONV_B - 1, W_B), F32)
    zeros_c = jnp.zeros((b, CONV_C - 1, W_C), F32)
    zeros_h = jnp.zeros((b, 1, W_C), F32)
    tm_p, tm_s = _row_tile(t, 512, 128), _row_tile(db * dt, 512)
    tt_p = _row_tile(t, 256)
    nat_cols = (NAT_GLU, NAT_ZB, NAT_XC, NAT_ZC)
    dec_cols = (COL_GLU, COL_ZB, COL_XC, COL_ZC)

    kv_states = [jnp.zeros((depth, b, KV_COLS, t), F32) for _ in range(N_KV)]
    p_states, s_states = [], []
    for l in range(depth):
        row = lambda a: a[l][None, :]
        lw = dict(cbw=conv_b_w[l], cbb=row(conv_b_b), gng=row(gn_gain), gnb=row(gn_bias), wpw=w_pw_b[l].astype(BF16),
                  ccw=conv_c_w[l], ccb=row(conv_c_b), wa=_block_diag(w_lru_a[l]).astype(BF16), ba=row(b_lru_a),
                  wx=_block_diag(w_lru_x[l]).astype(BF16), bx=row(b_lru_x), lam=row(lru_lambda))
        proj_n, proj_t, kv_states = _project_prompt(hp, row(g_pre), w_n[l], w_t[l], w_kv[l], kv_states, l, b, t, tm_p)
        a_out = _attn_prompt(l, proj_n, proj_t, kv_states, wck[l], wvt_p[l], seg_p, bnear, bct, ett, e2t, b, t)
        b_out, c_out, cb, cc, hc = _mixers(proj_n, nat_cols, zeros_b, zeros_c, zeros_h, lw, gones, b, t, tt_p)
        hp = _output(hp, a_out, b_out, c_out, w_o[l], row(g_post), tm_p)
        p_states.append((cb, cc, hc[:, 0]))
        proj = _project(hs, row(g_pre), w_r[l], tm_s)
        a_out, wk_n, wv_n = _attn_decode(l, page_table, proj, kct, vct, pool_sk, pool_sv, win_k, win_v,
                                         dtabs, e_d, e2, pages, dt)
        b_out, c_out, cb, cc, hc = _mixers(proj, dec_cols, state_conv_b[l], state_conv_c[l],
                                           state_rglru[l][:, None, :], lw, gones, db, dt, dt)
        hs = _output(hs, a_out, b_out, c_out, w_o[l], row(g_post), tm_s)
        kv = proj[:, COL_KV:COL_KV + 6 * KV_COLS].reshape(db, dt, 6, KV_A, HEAD_DIM)
        s_states.append((kv[:, :, 0], kv[:, :, 1], kv[:, :, 2], kv[:, :, 3],
                         wk_n, wv_n, cb, cc, hc[:, 0]))

    cb_p, cc_p, h_p = [jnp.stack(a) for a in zip(*p_states)]
    kv_states = kv_states[:4] + [a[..., t - WINDOW:] for a in kv_states[4:]]
    ck_p, cv_p, sk_p, sv_p, wk_p, wv_p = [
        jnp.transpose(a.reshape(depth, b, KV_A, HEAD_DIM, a.shape[-1]), (0, 1, 4, 2, 3)) for a in kv_states]
    ck_s, cv_s, sk_s, sv_s, wk_s, wv_s, cb_s, cc_s, h_s = [jnp.stack(a) for a in zip(*s_states)]
    wk_s, wv_s = [jnp.transpose(a.reshape(depth, db, KV_A, HEAD_DIM, WINDOW), (0, 1, 4, 2, 3)) for a in (wk_s, wv_s)]
    return (hp.reshape(b, t, D_MODEL), hs.reshape(db, dt, D_MODEL),
            ck_p, ck_s, cv_p, cv_s, sk_p, sk_s, sv_p, sv_s, wk_p, wk_s, wv_p, wv_s,
            cb_p, cb_s, cc_p, cc_s, h_p, h_s)
```

```python
import functools
import math

import numpy as np
import jax
import jax.numpy as jnp
from jax import lax
from jax.experimental import pallas as pl
from jax.experimental.pallas import tpu as pltpu

F32 = jnp.float32
BF16 = jnp.bfloat16

D_MODEL = 1024
HEAD_DIM = 64
W_A = D_MODEL // 2
W_B = D_MODEL // 4
W_C = D_MODEL // 4
H_A = W_A // HEAD_DIM
KV_A = 2
REP_A = H_A // KV_A
KV_COLS = KV_A * HEAD_DIM
L_CMP = 32
SEL_BLOCK = 64
TOP_K = 16
FORCE_SCORE = 1.0e4
WINDOW = 512
PAGE = 128
CONV_B = 31
CONV_C = 4
GN_GROUP = W_B // 4
LRU_C = 8.0
NUM_BUCKETS = 32
MAX_DISTANCE = 128
SM_SCALE = HEAD_DIM ** -0.5
LOG2E = 1.4426950408889634
EPS = 1e-6
NEG = -1e30

TQ = 256
TK = 256
V_ROWS = HEAD_DIM + 16
D_IN = 3096
D_IN_PAD = 3200
COL_Q, COL_ZA, COL_GLU, COL_KV, COL_ZB, COL_XC, COL_ZC, COL_GATE = 0, 512, 1024, 1536, 2304, 2560, 2816, 3072
N_NAT = 1280
NAT_GLU, NAT_ZB, NAT_XC, NAT_ZC = 0, 512, 768, 1024
N_TR = 1152
TR_Q, TR_ZA, TR_GATE = 0, 512, 1024
N_KV = 6
VMEM_LIMIT = 56 * 1024 * 1024


def _cparams(sem):
    return pltpu.CompilerParams(dimension_semantics=sem, vmem_limit_bytes=VMEM_LIMIT)


def _bucket_np(dist):
    n = np.maximum(dist, 0)
    max_exact = NUM_BUCKETS // 2
    nf = np.maximum(n, 1).astype(np.float32)
    large = max_exact + (np.log(nf / np.float32(max_exact)) / np.float32(math.log(MAX_DISTANCE / max_exact))
                         * np.float32(NUM_BUCKETS - max_exact)).astype(np.int32)
    return np.where(n < max_exact, n, np.minimum(large, NUM_BUCKETS - 1)).astype(np.int32)


def _expand_np(n_blocks, rows):
    e = np.zeros((rows, n_blocks * SEL_BLOCK), np.float32)
    for m in range(n_blocks):
        e[m, m * SEL_BLOCK:(m + 1) * SEL_BLOCK] = 1.0
    return e


def _gate_expand_np():
    e = np.zeros((128, 3 * W_A), np.float32)
    for br in range(3):
        for h in range(H_A):
            e[br * H_A + h, br * W_A + h * HEAD_DIM: br * W_A + (h + 1) * HEAD_DIM] = 1.0
    return e


def _group_ones_np():
    g = np.zeros((W_B, W_B), np.float32)
    for k in range(W_B // GN_GROUP):
        g[k * GN_GROUP:(k + 1) * GN_GROUP, k * GN_GROUP:(k + 1) * GN_GROUP] = 1.0
    return g


def _segment_np(n_rows, n_half):
    s = np.zeros((n_rows, 2 * n_half), np.float32)
    blk = np.arange(n_rows) // L_CMP
    s[np.arange(n_rows), np.where(blk % 2 == 0, blk // 2, n_half + blk // 2)] = 1.0
    return s


def _split2(a):
    hi = a.astype(BF16)
    return hi, (a - hi.astype(F32)).astype(BF16)


def _split3(a):
    hi = a.astype(BF16)
    r1 = a - hi.astype(F32)
    mid = r1.astype(BF16)
    return hi, mid, (r1 - mid.astype(F32)).astype(BF16)


def _dot(a, b):
    return jnp.dot(a, b, preferred_element_type=F32)


def _dot_nt(a, b):
    return lax.dot_general(a, b, (((1,), (1,)), ((), ())), preferred_element_type=F32)


def _dot_exact_rhs(a, b_bf16):
    hi, mid, lo = _split3(a)
    return _dot(hi, b_bf16) + _dot(mid, b_bf16) + _dot(lo, b_bf16)


def _dot_exact_lhs(a_bf16, b):
    hi, mid, lo = _split3(b)
    return _dot(a_bf16, hi) + _dot(a_bf16, mid) + _dot(a_bf16, lo)


def _dot_f32(a, b):
    ah, al = _split2(a)
    bh, bl = _split2(b)
    return _dot(ah, bh) + _dot(al, bh) + _dot(ah, bl)


def _sigmoid(x):
    return 1.0 / (1.0 + jnp.exp(-x))


def _silu(x):
    return x * _sigmoid(x)


def _expm1(x):
    u = jnp.exp(x)
    safe = jnp.where((u == 1.0) | (u == 0.0), 0.5, u)
    return jnp.where(u == 1.0, x, jnp.where(u == 0.0, -1.0, (safe - 1.0) * x / jnp.log(safe)))


def _rank_select(score, idx, n, axis):
    rank = jnp.zeros(score.shape, F32)
    for m in range(n):
        sm = score[m:m + 1, :] if axis == 0 else score[:, m:m + 1]
        beats = (sm > score) | ((sm == score) & (idx > m))
        rank = rank + jnp.where(beats, 1.0, 0.0)
    return rank


def _bias_kernel(rb_ref, bk_ref, o_ref, *, shift, scale):
    h = pl.program_id(0)
    bk = bk_ref[...]
    acc = jnp.zeros(bk.shape, F32)
    for b in range(NUM_BUCKETS):
        acc = jnp.where(bk == b, rb_ref[b, h], acc)
    if shift:
        acc = acc - rb_ref[NUM_BUCKETS - 1, h]
    o_ref[...] = acc * scale


def _bias_lookup(rel_bias, bucket, shift=False, scale=1.0):
    r, c = bucket.shape
    return pl.pallas_call(
        functools.partial(_bias_kernel, shift=shift, scale=scale),
        grid=(H_A,),
        in_specs=[pl.BlockSpec(memory_space=pltpu.SMEM),
                  pl.BlockSpec((r, c), lambda h: (0, 0))],
        out_specs=pl.BlockSpec((None, r, c), lambda h: (h, 0, 0)),
        out_shape=jax.ShapeDtypeStruct((H_A, r, c), F32),
        compiler_params=_cparams(("arbitrary",)),
        name="bias_lookup",
    )(rel_bias, jnp.asarray(bucket))


def _rms_bf16(x_ref, g_ref):
    x = x_ref[...]
    ms = jnp.mean(x * x, axis=-1, keepdims=True)
    return (x * lax.rsqrt(ms + EPS) * g_ref[...]).astype(BF16)


def _proj_kernel(x_ref, g_ref, w_ref, o_ref):
    u = _rms_bf16(x_ref, g_ref)
    step = 640
    for c in range(0, o_ref.shape[1], step):
        o_ref[:, c:c + step] = _dot(u, w_ref[:, c:c + step])


def _project(h, g, w, tm):
    m = h.shape[0]
    return pl.pallas_call(
        _proj_kernel,
        grid=(m // tm,),
        in_specs=[pl.BlockSpec((tm, D_MODEL), lambda i: (i, 0)),
                  pl.BlockSpec((1, D_MODEL), lambda i: (0, 0)),
                  pl.BlockSpec((D_MODEL, D_IN_PAD), lambda i: (0, 0))],
        out_specs=pl.BlockSpec((tm, D_IN_PAD), lambda i: (i, 0)),
        out_shape=jax.ShapeDtypeStruct((m, D_IN_PAD), F32),
        compiler_params=_cparams(("arbitrary",)),
        name="project",
    )(h, g, w)


def _proj2_kernel(x_ref, g_ref, wn_ref, wt_ref, wkv_ref, *refs):
    on_ref, ot_ref = refs[N_KV], refs[N_KV + 1]
    st_refs = refs[N_KV + 2:]
    u = _rms_bf16(x_ref, g_ref)
    for c in range(0, N_NAT, 640):
        on_ref[:, c:c + 640] = _dot(u, wn_ref[:, c:c + 640])
    for c in range(0, N_TR, 384):
        ot_ref[c:c + 384, :] = _dot_nt(wt_ref[c:c + 384, :], u)
    for k in range(0, N_KV, 2):
        kv = _dot_nt(wkv_ref[k * KV_COLS:(k + 2) * KV_COLS, :], u)
        st_refs[k][...] = kv[0:KV_COLS]
        st_refs[k + 1][...] = kv[KV_COLS:2 * KV_COLS]


def _project_prompt(h, g, wn, wt, wkv, states, layer, b, t, tm):
    per = t // tm
    st_spec = pl.BlockSpec((None, None, KV_COLS, tm), lambda i: (layer, i // per, 0, i % per))
    outs = pl.pallas_call(
        _proj2_kernel,
        grid=(b * per,),
        in_specs=[pl.BlockSpec((tm, D_MODEL), lambda i: (i, 0)),
                  pl.BlockSpec((1, D_MODEL), lambda i: (0, 0)),
                  pl.BlockSpec((D_MODEL, N_NAT), lambda i: (0, 0)),
                  pl.BlockSpec((N_TR, D_MODEL), lambda i: (0, 0)),
                  pl.BlockSpec((N_KV * KV_COLS, D_MODEL), lambda i: (0, 0))]
                 + [pl.BlockSpec(memory_space=pl.ANY)] * N_KV,
        out_specs=[pl.BlockSpec((tm, N_NAT), lambda i: (i, 0)),
                   pl.BlockSpec((None, N_TR, tm), lambda i: (i // per, 0, i % per))] + [st_spec] * N_KV,
        out_shape=[jax.ShapeDtypeStruct((b * t, N_NAT), F32), jax.ShapeDtypeStruct((b, N_TR, t), F32)]
                  + [jax.ShapeDtypeStruct(a.shape, a.dtype) for a in states],
        input_output_aliases={5 + k: 2 + k for k in range(N_KV)},
        compiler_params=_cparams(("arbitrary",)),
        name="project_prompt",
    )(h, g, wn, wt, wkv, *states)
    return outs[0], outs[1], list(outs[2:])


def _out_kernel(h_ref, a_ref, b_ref, c_ref, w_ref, g_ref, o_ref):
    y = (_dot(a_ref[...].astype(BF16), w_ref[0:W_A, :])
         + _dot(b_ref[...].astype(BF16), w_ref[W_A:W_A + W_B, :])
         + _dot(c_ref[...].astype(BF16), w_ref[W_A + W_B:, :]))
    ms = jnp.mean(y * y, axis=-1, keepdims=True)
    o_ref[...] = h_ref[...] + y * lax.rsqrt(ms + EPS) * g_ref[...]


def _output(h, a, b, c, w, g, tm):
    m = h.shape[0]
    return pl.pallas_call(
        _out_kernel,
        grid=(m // tm,),
        in_specs=[pl.BlockSpec((tm, D_MODEL), lambda i: (i, 0)),
                  pl.BlockSpec((tm, W_A), lambda i: (i, 0)),
                  pl.BlockSpec((tm, W_B), lambda i: (i, 0)),
                  pl.BlockSpec((tm, W_C), lambda i: (i, 0)),
                  pl.BlockSpec((D_MODEL, D_MODEL), lambda i: (0, 0)),
                  pl.BlockSpec((1, D_MODEL), lambda i: (0, 0))],
        out_specs=pl.BlockSpec((tm, D_MODEL), lambda i: (i, 0)),
        out_shape=jax.ShapeDtypeStruct((m, D_MODEL), F32),
        compiler_params=_cparams(("arbitrary",)),
        name="output",
    )(h, a, b, c, w, g)


def _softmax_steps(scores, vts, states):
    stats = []
    for s, st in zip(scores, states):
        m_new = jnp.max(s, axis=0, keepdims=True)
        alpha = None
        if st is not None:
            m_new = jnp.maximum(st[0], m_new)
            alpha = jnp.exp2(st[0] - m_new)
        stats.append((m_new, alpha, jnp.exp2(s - m_new).astype(BF16)))
    out = []
    for (m_new, alpha, p), vt, st in zip(stats, vts, states):
        pv = _dot(vt, p)
        out.append((m_new, pv if st is None else alpha * st[1] + pv))
    return out


def _softmax_finish(states):
    return jnp.concatenate([acc[0:HEAD_DIM] / acc[HEAD_DIM:HEAD_DIM + 1] for (_, acc) in states], axis=0)


def _attn_prompt_kernel(qt_ref, zat_ref, gt_ref, kct_ref, vct_ref, kst_ref, vst_ref, kwt_ref, vwt_ref,
                        wk_ref, wvt_ref, seg_ref, bnear_ref, bct_ref, ett_ref, e2t_ref,
                        o_ref, kc_s, vct_s, ks_s, kw_s, vst_s, vwt_s, *, ns, nt):
    i = pl.program_id(1)
    nsp = -(-ns // 8) * 8

    @pl.when(i == 0)
    def _():
        x3 = kct_ref[...].T.reshape(ns, SEL_BLOCK, KV_COLS)
        w = wk_ref[...]
        kc_s[0:ns, :] = jnp.sum(x3[:, :L_CMP, :] * w[None], axis=1)
        kc_s[ns:2 * ns, :] = jnp.sum(x3[:, L_CMP:, :] * w[None], axis=1)
        hi, mid = _split2(vct_ref[...] * wvt_ref[...])
        seg = seg_ref[...]
        vct_s[...] = _dot(hi, seg) + _dot(mid, seg)
        ks_n, kw_n = kst_ref[...].T, kwt_ref[...].T
        for g in range(KV_A):
            ks_s[g] = jnp.concatenate([ks_n[:, g * HEAD_DIM:(g + 1) * HEAD_DIM].astype(BF16), ett_ref[...]], axis=1)
            kw_s[g] = jnp.concatenate([kw_n[:, g * HEAD_DIM:(g + 1) * HEAD_DIM].astype(BF16),
                                       jnp.zeros(ett_ref.shape, BF16)], axis=1)
        ones_row = jnp.where(lax.broadcasted_iota(jnp.int32, (V_ROWS - HEAD_DIM, TK), 0) == 0, 1.0, 0.0).astype(BF16)
        for c in range(nt):
            for g in range(KV_A):
                gs = slice(g * HEAD_DIM, (g + 1) * HEAD_DIM)
                vst_s[c, g] = jnp.concatenate([vst_ref[gs, c * TK:(c + 1) * TK].astype(BF16), ones_row], axis=0)
                vwt_s[c, g] = jnp.concatenate([vwt_ref[gs, c * TK:(c + 1) * TK].astype(BF16), ones_row], axis=0)

    qt = qt_ref[...]
    q0 = i * TQ

    row_c = lax.broadcasted_iota(jnp.int32, (2 * ns, TQ), 0)
    qpos_c = q0 + lax.broadcasted_iota(jnp.int32, (2 * ns, TQ), 1)
    blk_c = jnp.where(row_c < ns, 2 * row_c, 2 * (row_c - ns) + 1)
    mask_c = qpos_c >= blk_c * L_CMP + (L_CMP - 1)
    blk_t = lax.broadcasted_iota(jnp.int32, (nsp, TQ), 0)
    cur_t = (q0 + lax.broadcasted_iota(jnp.int32, (nsp, TQ), 1)) // SEL_BLOCK
    forced = (blk_t == 0) | ((blk_t <= cur_t) & (blk_t > cur_t - 2))
    kq_gap = (lax.broadcasted_iota(jnp.int32, (TK, TQ), 0) - lax.broadcasted_iota(jnp.int32, (TK, TQ), 1))
    sel_neg = []

    oc_parts = []
    for g in range(KV_A):
        kh, kl = _split2(kc_s[:, g * HEAD_DIM:(g + 1) * HEAD_DIM])
        vcg = vct_s[g * HEAD_DIM:(g + 1) * HEAD_DIM, :].astype(BF16)
        imp = jnp.zeros((2 * ns, TQ), F32)
        for r in range(REP_A):
            h = g * REP_A + r
            qh, ql = _split2(qt[h * HEAD_DIM:(h + 1) * HEAD_DIM, :])
            s = (_dot(kh, qh) + _dot(kl, qh) + _dot(kh, ql)) * SM_SCALE + bct_ref[h]
            s = jnp.where(mask_c, s, NEG)
            e = jnp.exp(s - jnp.max(s, axis=0, keepdims=True))
            p = e / jnp.sum(e, axis=0, keepdims=True)
            p = jnp.where(mask_c, p, 0.0)
            oc_parts.append(_dot(vcg, p.astype(BF16)))
            imp = imp + p
        simp = imp[0:ns, :] + imp[ns:2 * ns, :]
        if nsp > ns:
            simp = jnp.concatenate([simp, jnp.zeros((nsp - ns, TQ), F32)], axis=0)
        score = jnp.where(forced, FORCE_SCORE, simp)
        score = jnp.where(blk_t <= cur_t, score, NEG)
        rank = _rank_select(score, blk_t, ns, 0)
        sel = jnp.where((rank < TOP_K) & (blk_t <= cur_t), 1.0, 0.0)
        sel = jnp.concatenate([sel, jnp.zeros((128 - nsp, TQ), F32)], axis=0)
        sel_neg.append(((sel - 1.0) * (-NEG)).astype(BF16))
    oc_t = jnp.concatenate(oc_parts, axis=0)

    ok1, ok2 = i >= 1, i >= 2
    c1, c2 = jnp.maximum(i - 1, 0), jnp.maximum(i - 2, 0)
    n_far = jnp.maximum(i - 1, 0)
    group = [h // REP_A for h in range(H_A)]
    qs = [jnp.concatenate([(qt[h * HEAD_DIM:(h + 1) * HEAD_DIM, :] * (SM_SCALE * LOG2E)).astype(BF16),
                           sel_neg[group[h]]], axis=0) for h in range(H_A)]

    def kv_sel(g, c):
        r0 = pl.multiple_of(c * TK, TK)
        return ks_s[g, pl.ds(r0, TK), :], vst_s[c, g]

    def kv_win(g, c):
        r0 = pl.multiple_of(c * TK, TK)
        return kw_s[g, pl.ds(r0, TK), :], vwt_s[c, g]

    def far(c, carry):
        kv = [kv_sel(g, c) for g in range(KV_A)]
        scores = [_dot(kv[group[h]][0], qs[h]) for h in range(H_A)]
        return tuple(_softmax_steps(scores, [kv[group[h]][1] for h in range(H_A)], carry))

    init = tuple((jnp.full((1, TQ), NEG, F32), jnp.zeros((V_ROWS, TQ), F32)) for _ in range(H_A))
    sel_st = list(lax.fori_loop(0, n_far, far, init))
    win_st = [None] * H_A
    for c, ok, d_idx in ((c1, ok1, 1), (i, None, 0)):
        kvs = [kv_sel(g, c) for g in range(KV_A)]
        kvw = [kv_win(g, c) for g in range(KV_A)]
        kcat = [jnp.concatenate([kvs[g][0], kvw[g][0]], axis=0) for g in range(KV_A)]
        s2 = [_dot(kcat[group[h]], qs[h]) for h in range(H_A)]
        scores, vts, states = [], [], []
        for h in range(H_A):
            bias = bnear_ref[h, d_idx] if ok is None else jnp.where(ok, bnear_ref[h, d_idx], NEG)
            scores += [s2[h][0:TK] + bias, s2[h][TK:2 * TK] + bias]
            vts += [kvs[group[h]][1], kvw[group[h]][1]]
            states += [sel_st[h], win_st[h]]
        new = _softmax_steps(scores, vts, states)
        sel_st, win_st = new[0::2], new[1::2]
    kvw = [kv_win(g, c2) for g in range(KV_A)]
    far_add = jnp.where((kq_gap > 0) & ok2, 0.0, NEG)
    win_st = _softmax_steps([_dot(kvw[group[h]][0], qs[h]) + far_add for h in range(H_A)],
                            [kvw[group[h]][1] for h in range(H_A)], win_st)
    os_t = _softmax_finish(sel_st)
    ow_t = _softmax_finish(win_st)

    ge = _dot_exact_lhs(e2t_ref[...], _sigmoid(gt_ref[...]))
    o_t = ge[0:W_A] * oc_t + ge[W_A:2 * W_A] * os_t + ge[2 * W_A:3 * W_A] * ow_t
    o_ref[...] = (o_t * _silu(zat_ref[...])).T


def _attn_prompt(layer, proj_t, states, wk, wvt, seg, bnear, bct, ett, e2t, b, t):
    ns, nt = t // SEL_BLOCK, t // TQ
    st_spec = pl.BlockSpec((None, None, KV_COLS, t), lambda bi, i: (layer, bi, 0, 0))

    def tr_spec(rows, row0, width):
        return pl.BlockSpec((None, rows, width), lambda bi, i: (bi, row0 // rows, i))

    def const_spec(a):
        return pl.BlockSpec(a.shape, lambda bi, i: (0,) * a.ndim)

    in_specs = [tr_spec(W_A, TR_Q, TQ), tr_spec(W_A, TR_ZA, TQ), tr_spec(128, TR_GATE, TQ),
                st_spec, st_spec, st_spec, st_spec, st_spec, st_spec,
                const_spec(wk), const_spec(wvt), const_spec(seg), const_spec(bnear),
                pl.BlockSpec((H_A, 2 * ns, TQ), lambda bi, i: (0, 0, i)),
                const_spec(ett), const_spec(e2t)]
    scratch = [pltpu.VMEM((2 * ns, KV_COLS), F32), pltpu.VMEM((KV_COLS, 2 * ns), F32),
               pltpu.VMEM((KV_A, t, HEAD_DIM + 128), BF16), pltpu.VMEM((KV_A, t, HEAD_DIM + 128), BF16),
               pltpu.VMEM((nt, KV_A, V_ROWS, TK), BF16), pltpu.VMEM((nt, KV_A, V_ROWS, TK), BF16)]
    return pl.pallas_call(
        functools.partial(_attn_prompt_kernel, ns=ns, nt=nt),
        grid=(b, nt),
        in_specs=in_specs,
        out_specs=pl.BlockSpec((TQ, W_A), lambda bi, i: (bi * nt + i, 0)),
        out_shape=jax.ShapeDtypeStruct((b * t, W_A), F32),
        scratch_shapes=scratch,
        compiler_params=_cparams(("arbitrary", "arbitrary")),
        name="attn_prompt",
    )(proj_t, proj_t, proj_t, *states, wk, wvt, seg, bnear, bct, ett, e2t)


def _dcmp_kernel(pt_ref, *refs, pages):
    del pt_ref
    k_refs, v_refs = refs[0:pages], refs[pages:2 * pages]
    wkt_ref, wvt_ref, seg_ref = refs[2 * pages:2 * pages + 3]
    ko_ref, vo_ref = refs[2 * pages + 3:]
    for src, w_ref, o_ref in ((k_refs, wkt_ref, ko_ref), (v_refs, wvt_ref, vo_ref)):
        for l in range(o_ref.shape[0]):
            w = w_ref[l]
            parts = [_split2(src[p][l] * w) for p in range(pages)]
            x = jnp.concatenate([hi for hi, _ in parts] + [lo for _, lo in parts], axis=1)
            o_ref[l] = _dot(x, seg_ref[...])


def _decode_compress(page_table, pool_k, pool_v, wkt, wvt, seg, pages):
    depth, db, n_pages = pool_k.shape[0], page_table.shape[0], page_table.shape[1]
    nc = n_pages // pages
    ncol = seg.shape[1]

    def page_spec(p):
        return pl.BlockSpec((depth, None, KV_COLS, PAGE), lambda b, c, pt: (0, pt[b, c * pages + p], 0, 0))

    w_spec = pl.BlockSpec((depth, KV_COLS, PAGE), lambda b, c, pt: (0, 0, 0))
    s_spec = pl.BlockSpec(seg.shape, lambda b, c, pt: (0, 0))
    o_spec = pl.BlockSpec((depth, None, None, KV_COLS, ncol), lambda b, c, pt: (0, b, c, 0, 0))
    o_shape = jax.ShapeDtypeStruct((depth, db, nc, KV_COLS, ncol), F32)
    return pl.pallas_call(
        functools.partial(_dcmp_kernel, pages=pages),
        grid_spec=pltpu.PrefetchScalarGridSpec(
            num_scalar_prefetch=1,
            grid=(db, nc),
            in_specs=[page_spec(p) for p in range(pages)] * 2 + [w_spec, w_spec, s_spec],
            out_specs=[o_spec] * 2),
        out_shape=[o_shape] * 2,
        compiler_params=_cparams(("arbitrary", "arbitrary")),
        name="decode_compress",
    )(page_table, *([pool_k] * pages), *([pool_v] * pages), wkt, wvt, seg)


def _softmax_rows(qbd, kt, add, m, l):
    s = _dot(qbd, kt) + add
    m_new = jnp.maximum(m, jnp.max(s, axis=-1, keepdims=True))
    alpha = jnp.exp2(m - m_new)
    p = jnp.exp2(s - m_new)
    return m_new, alpha, alpha * l + jnp.sum(p, axis=-1, keepdims=True), p.astype(BF16)


def _attn_decode_kernel(pt_ref, *refs, pages, nps, nc, dt):
    del pt_ref
    (q_ref, za_ref, kv_ref, gate_ref, kct_ref, vct_ref) = refs[0:6]
    sk_refs, sv_refs = refs[6:6 + pages], refs[6 + pages:6 + 2 * pages]
    (wink_ref, winv_ref, bcd_ref, blast_ref, bnew_ref, bwin_ref, e_ref, e2_ref,
     o_ref, wk_o, wv_o, qbd_s, m_s, l_s, acc_s, madd_s, oc_s) = refs[6 + 2 * pages:]
    c = pl.program_id(1)
    ck = pages * PAGE
    rows = KV_A * REP_A * dt
    zpad = jnp.zeros((dt, HEAD_DIM), F32)

    @pl.when(c == 0)
    def _():
        q = q_ref[...]
        blocks = []
        for g in range(KV_A):
            for r in range(REP_A):
                h = g * REP_A + r
                piece = q[:, h * HEAD_DIM:(h + 1) * HEAD_DIM]
                blocks.append(jnp.concatenate([piece, zpad] if g == 0 else [zpad, piece], axis=1))
        qbd = jnp.concatenate(blocks, axis=0)
        qbd_s[...] = qbd
        s = _dot_f32(qbd, kct_ref[...]) * SM_SCALE + bcd_ref[...]
        e = jnp.exp(s - jnp.max(s, axis=-1, keepdims=True))
        p = e / jnp.sum(e, axis=-1, keepdims=True)
        oc_s[...] = _dot_nt(p.astype(BF16), vct_ref[...].astype(BF16))
        scores = []
        for g in range(KV_A):
            pe = [p[(g * REP_A + r) * dt:(g * REP_A + r + 1) * dt, 0:nps] for r in range(REP_A)]
            po = [p[(g * REP_A + r) * dt:(g * REP_A + r + 1) * dt, nps:2 * nps] for r in range(REP_A)]
            scores.append((pe[0] + pe[1] + pe[2] + pe[3]) + (po[0] + po[1] + po[2] + po[3]))
        simp = jnp.concatenate(scores, axis=0)
        blk = lax.broadcasted_iota(jnp.int32, simp.shape, 1)
        score = jnp.where((blk == 0) | (blk >= nps - 1), FORCE_SCORE, simp)
        rank = _rank_select(score, blk, nps, 1)
        sel = jnp.where(rank < TOP_K - 1, 1.0, 0.0).astype(BF16)
        madd = (_dot(sel, e_ref[...]) - 1.0) * (-NEG)
        madd = jnp.concatenate([madd[0:dt]] * REP_A + [madd[dt:2 * dt]] * REP_A, axis=0)
        for j in range(nc):
            tile = madd[:, j * ck:(j + 1) * ck]
            if j == nc - 1:
                tile = jnp.concatenate([tile[:, :ck - PAGE], tile[:, ck - PAGE:] + blast_ref[...]], axis=1)
            madd_s[j] = tile
        m_s[...] = jnp.full(m_s.shape, NEG, F32)
        l_s[...] = jnp.zeros(l_s.shape, F32)
        acc_s[...] = jnp.zeros(acc_s.shape, F32)

    qbd = (qbd_s[...] * (SM_SCALE * LOG2E)).astype(BF16)
    kt = jnp.concatenate([r[...].astype(BF16) for r in sk_refs], axis=1)
    vt = jnp.concatenate([r[...].astype(BF16) for r in sv_refs], axis=1)
    m, alpha, l, p = _softmax_rows(qbd, kt, madd_s[c], m_s[...], l_s[...])
    acc = alpha * acc_s[...] + _dot_nt(p, vt)
    m_s[...], l_s[...], acc_s[...] = m, l, acc

    @pl.when(c == nc - 1)
    def _():
        row_t = lax.broadcasted_iota(jnp.int32, (rows, 128), 0) % dt
        col = lax.broadcasted_iota(jnp.int32, (rows, 128), 1)
        new_add = jnp.where((col <= row_t) & (col < dt), bnew_ref[...], NEG)
        row_w = lax.broadcasted_iota(jnp.int32, (rows, WINDOW), 0) % dt
        col_w = lax.broadcasted_iota(jnp.int32, (rows, WINDOW), 1)
        win_add = jnp.where(col_w > row_w, bwin_ref[...], NEG)
        pad = jnp.zeros((128 - dt, KV_COLS), F32)

        def new_tile(src):
            return jnp.concatenate([kv_ref[:, src * KV_COLS:(src + 1) * KV_COLS], pad], axis=0).astype(BF16)

        def attend_nat(kn, vn, add, m, l, acc):
            s = _dot_nt(qbd, kn) + add
            m_new = jnp.maximum(m, jnp.max(s, axis=-1, keepdims=True))
            alpha = jnp.exp2(m - m_new)
            p = jnp.exp2(s - m_new)
            return m_new, alpha * l + jnp.sum(p, axis=-1, keepdims=True), alpha * acc + _dot(p.astype(BF16), vn)

        m2, l2, acc2 = attend_nat(new_tile(2), new_tile(3), new_add, m, l, acc)
        os_full = acc2 / l2
        init = (jnp.full((rows, 1), NEG, F32), jnp.zeros((rows, 1), F32), jnp.zeros((rows, KV_COLS), F32))
        m_w, _, l_w, p_w = _softmax_rows(qbd, wink_ref[...].astype(BF16), win_add, init[0], init[1])
        carry = (m_w, l_w, _dot_nt(p_w, winv_ref[...].astype(BF16)))
        m3, l3, acc3 = attend_nat(new_tile(4), new_tile(5), new_add, *carry)
        ow_full = acc3 / l3

        def heads(full):
            parts = []
            for g in range(KV_A):
                for r in range(REP_A):
                    r0 = (g * REP_A + r) * dt
                    parts.append(full[r0:r0 + dt, g * HEAD_DIM:(g + 1) * HEAD_DIM])
            return jnp.concatenate(parts, axis=1)

        sg = _sigmoid(gate_ref[...])
        ge = _dot_exact_rhs(sg, e2_ref[...])
        o = ge[:, 0:W_A] * heads(oc_s[...]) + ge[:, W_A:2 * W_A] * heads(os_full) + ge[:, 2 * W_A:] * heads(ow_full)
        o_ref[...] = o * _silu(za_ref[...])
        for src, w_ref, wo_ref in ((4, wink_ref, wk_o), (5, winv_ref, wv_o)):
            new_t = jnp.concatenate([kv_ref[:, src * KV_COLS:(src + 1) * KV_COLS], pad], axis=0).T
            wo_ref[...] = jnp.concatenate([w_ref[:, dt:], new_t[:, 0:dt]], axis=1)


def _attn_decode(layer, page_table, proj, kct, vct, pool_sk, pool_sv, win_k, win_v, tabs, e_sel, e2, pages, dt):
    db, n_pages = page_table.shape
    nps = n_pages * (PAGE // SEL_BLOCK)
    nc = n_pages // pages
    ck = pages * PAGE
    rows = KV_A * REP_A * dt
    bcd, blast, bnew, bwin = tabs

    def row_spec(width, col):
        return pl.BlockSpec((dt, width), lambda b, c, pt: (b, col // width))

    def page_spec(p):
        return pl.BlockSpec((None, None, KV_COLS, PAGE), lambda b, c, pt: (layer, pt[b, c * pages + p], 0, 0))

    cmp_spec = pl.BlockSpec((None, None, KV_COLS, 2 * nps), lambda b, c, pt: (layer, b, 0, 0))
    win_spec = pl.BlockSpec((None, None, KV_COLS, WINDOW), lambda b, c, pt: (layer, b, 0, 0))

    def const_spec(a):
        return pl.BlockSpec(a.shape, lambda b, c, pt: (0,) * a.ndim)

    in_specs = ([row_spec(W_A, COL_Q), row_spec(W_A, COL_ZA), row_spec(6 * KV_COLS, COL_KV), row_spec(128, COL_GATE)]
                + [cmp_spec] * 2 + [page_spec(p) for p in range(pages)] * 2 + [win_spec] * 2
                + [const_spec(a) for a in (bcd, blast, bnew, bwin, e_sel, e2)])
    out_specs = [pl.BlockSpec((dt, W_A), lambda b, c, pt: (b, 0)),
                 pl.BlockSpec((None, KV_COLS, WINDOW), lambda b, c, pt: (b, 0, 0)),
                 pl.BlockSpec((None, KV_COLS, WINDOW), lambda b, c, pt: (b, 0, 0))]
    out_shape = [jax.ShapeDtypeStruct((db * dt, W_A), F32),
                 jax.ShapeDtypeStruct((db, KV_COLS, WINDOW), F32),
                 jax.ShapeDtypeStruct((db, KV_COLS, WINDOW), F32)]
    scratch = [pltpu.VMEM((rows, KV_COLS), F32), pltpu.VMEM((rows, 1), F32), pltpu.VMEM((rows, 1), F32),
               pltpu.VMEM((rows, KV_COLS), F32), pltpu.VMEM((nc, rows, ck), F32), pltpu.VMEM((rows, KV_COLS), F32)]
    return pl.pallas_call(
        functools.partial(_attn_decode_kernel, pages=pages, nps=nps, nc=nc, dt=dt),
        grid_spec=pltpu.PrefetchScalarGridSpec(
            num_scalar_prefetch=1, grid=(db, nc), in_specs=in_specs, out_specs=out_specs, scratch_shapes=scratch),
        out_shape=out_shape,
        compiler_params=_cparams(("arbitrary", "arbitrary")),
        name="attn_decode",
    )(page_table, proj, proj, proj, proj, kct, vct, *([pool_sk] * pages), *([pool_sv] * pages),
      win_k, win_v, bcd, blast, bnew, bwin, e_sel, e2)


EXT_B0 = 32
EXT_C0 = 8
CONV_ROWS = 32
SH_PAD = (CONV_B - 1) // 8 * 8


def _mixer_kernel(glu_ref, zb_ref, xc_ref, zc_ref, bufb_ref, bufc_ref, h0_ref,
                  cbw_ref, cbb_ref, gng_ref, gnb_ref, wpw_ref, gones_ref,
                  ccw_ref, ccb_ref, wa_ref, ba_ref, wx_ref, bx_ref, lam_ref,
                  bo_ref, co_ref, cbs_ref, ccs_ref, hs_ref,
                  extb, extc, hcar, cbuf, shb, *, tt):
    j = pl.program_id(1)
    nb, nc = CONV_B - 1, CONV_C - 1

    @pl.when(j == 0)
    def _():
        extb[EXT_B0 - nb:EXT_B0, :] = bufb_ref[...]
        extc[EXT_C0 - nc:EXT_C0, :] = bufc_ref[...]
        hcar[...] = h0_ref[...]

    glu = glu_ref[...]
    extb[EXT_B0:EXT_B0 + tt, :] = glu[:, 0:W_B] * _sigmoid(glu[:, W_B:2 * W_B])
    for r in range(8):
        n_r = tt + (CONV_B - 1 - r) // 8 * 8
        shb[r, 0:n_r, :] = extb[EXT_B0 - nb + r:EXT_B0 - nb + r + n_r, :]
    step = min(CONV_ROWS, tt)
    for r0 in range(0, tt, step):
        acc = jnp.zeros((step, W_B), F32)
        for k in range(CONV_B):
            lo = (k // 8) * 8 + r0
            acc = acc + cbw_ref[k:k + 1, :] * shb[k % 8, lo:lo + step, :]
        cbuf[r0:r0 + step, :] = acc + cbb_ref[...]
    cv = cbuf[...]
    gones = gones_ref[...]
    mu = _dot_exact_rhs(cv, gones) * (1.0 / GN_GROUP)
    d = cv - mu
    var = _dot_exact_rhs(d * d, gones) * (1.0 / GN_GROUP)
    cn = d * lax.rsqrt(var + EPS) * gng_ref[...] + gnb_ref[...]
    bo_ref[...] = _dot(_silu(cn).astype(BF16), wpw_ref[...]) * _silu(zb_ref[...])

    extc[EXT_C0:EXT_C0 + tt, :] = xc_ref[...]
    u = jnp.zeros((tt, W_C), F32)
    for k in range(CONV_C):
        lo = EXT_C0 - nc + k
        u = u + ccw_ref[k:k + 1, :] * extc[lo:lo + tt, :]
    u = u + ccb_ref[...]
    ub = u.astype(BF16)
    r = _sigmoid(_dot(ub, wa_ref[...]) + ba_ref[...])
    ig = _sigmoid(_dot(ub, wx_ref[...]) + bx_ref[...])
    nl = -lam_ref[...]
    softplus = jnp.maximum(nl, 0.0) + jnp.log1p(jnp.exp(-jnp.abs(nl)))
    log_a = -LRU_C * r * softplus
    a = jnp.exp(log_a)
    b = jnp.sqrt(-_expm1(2.0 * log_a)) * (ig * u)
    row = lax.broadcasted_iota(jnp.int32, (tt, W_C), 0)
    s = 1
    while s < tt:
        a_sh = jnp.where(row < s, 1.0, pltpu.roll(a, s, axis=0))
        b_sh = jnp.where(row < s, 0.0, pltpu.roll(b, s, axis=0))
        b = a * b_sh + b
        a = a * a_sh
        s *= 2
    hh = a * hcar[...] + b
    co_ref[...] = hh * _silu(zc_ref[...])
    hcar[...] = hh[tt - 1:tt, :]

    new_b = extb[EXT_B0 + tt - nb:EXT_B0 + tt, :]
    new_c = extc[EXT_C0 + tt - nc:EXT_C0 + tt, :]
    extb[EXT_B0 - nb:EXT_B0, :] = new_b
    extc[EXT_C0 - nc:EXT_C0, :] = new_c

    @pl.when(j == pl.num_programs(1) - 1)
    def _():
        cbs_ref[...] = new_b
        ccs_ref[...] = new_c
        hs_ref[...] = hh[tt - 1:tt, :]


def _mixers(proj, cols, bufb, bufc, h0, lw, gones, b, t, tt):
    nj = t // tt
    c_glu, c_zb, c_xc, c_zc = cols

    def row_spec(width, col):
        return pl.BlockSpec((tt, width), lambda bi, j: (bi * nj + j, col // width))

    def st_spec(n, w):
        return pl.BlockSpec((None, n, w), lambda bi, j: (bi, 0, 0))

    def const_spec(a):
        return pl.BlockSpec(a.shape, lambda bi, j: (0,) * a.ndim)

    consts = [lw["cbw"], lw["cbb"], lw["gng"], lw["gnb"], lw["wpw"], gones,
              lw["ccw"], lw["ccb"], lw["wa"], lw["ba"], lw["wx"], lw["bx"], lw["lam"]]
    in_specs = ([row_spec(2 * W_B, c_glu), row_spec(W_B, c_zb), row_spec(W_C, c_xc), row_spec(W_C, c_zc),
                 st_spec(CONV_B - 1, W_B), st_spec(CONV_C - 1, W_C), st_spec(1, W_C)]
                + [const_spec(a) for a in consts])
    out_specs = [pl.BlockSpec((tt, W_B), lambda bi, j: (bi * nj + j, 0)),
                 pl.BlockSpec((tt, W_C), lambda bi, j: (bi * nj + j, 0)),
                 st_spec(CONV_B - 1, W_B), st_spec(CONV_C - 1, W_C), st_spec(1, W_C)]
    out_shape = [jax.ShapeDtypeStruct((b * t, W_B), F32), jax.ShapeDtypeStruct((b * t, W_C), F32),
                 jax.ShapeDtypeStruct((b, CONV_B - 1, W_B), F32), jax.ShapeDtypeStruct((b, CONV_C - 1, W_C), F32),
                 jax.ShapeDtypeStruct((b, 1, W_C), F32)]
    scratch = [pltpu.VMEM((EXT_B0 + tt, W_B), F32), pltpu.VMEM((EXT_C0 + tt, W_C), F32),
               pltpu.VMEM((1, W_C), F32), pltpu.VMEM((tt, W_B), F32), pltpu.VMEM((8, tt + SH_PAD, W_B), F32)]
    return pl.pallas_call(
        functools.partial(_mixer_kernel, tt=tt),
        grid=(b, nj),
        in_specs=in_specs, out_specs=out_specs, out_shape=out_shape, scratch_shapes=scratch,
        compiler_params=_cparams(("arbitrary", "arbitrary")),
        name="mixers",
    )(proj, proj, proj, proj, bufb, bufc, h0, *consts)


def _block_diag(w):
    nblk, c, d = w.shape
    eye = jnp.eye(nblk, dtype=w.dtype)
    return (eye[:, None, :, None] * w[:, :, None, :]).reshape(nblk * c, nblk * d)


def _row_tile(m, cap, step=8):
    t = min(m, cap)
    while m % t:
        t -= step
    return t


def kernel(x_prompt, x_sample, cache_cmp_k, cache_cmp_v, cache_sel_k, cache_sel_v, cache_win_k, cache_win_v,
           state_conv_b, state_conv_c, state_rglru, page_table, rel_bias, g_pre, g_post, w_in, w_out,
           w_cmp_k, w_cmp_v, conv_b_w, conv_b_b, gn_gain, gn_bias, w_pw_b, conv_c_w, conv_c_b,
           w_lru_a, b_lru_a, w_lru_x, b_lru_x, lru_lambda):
    depth = w_in.shape[0]
    b, t, _ = x_prompt.shape
    db, dt, _ = x_sample.shape
    n_pages = page_table.shape[1]
    past = n_pages * PAGE
    nps = past // SEL_BLOCK
    ns, nt = t // SEL_BLOCK, t // TQ
    assert t % TQ == 0 and TQ == TK and WINDOW == 2 * TK and TOP_K <= ns <= 128 and TOP_K < nps <= 128
    assert dt <= 8 and (past + dt) // L_CMP == past // L_CMP and cache_win_k.shape[2] == WINDOW
    pages = min(32, n_pages)
    pages_c = min(16, n_pages)
    assert n_pages % pages == 0 and n_pages % pages_c == 0
    nc = n_pages // pages_c
    rows = KV_A * REP_A * dt

    zpad = jnp.zeros((depth, D_MODEL, D_IN_PAD - D_IN), w_in.dtype)
    w_r = jnp.concatenate(
        [w_in[..., 0:512], w_in[..., 1304:1816], w_in[..., 1816:2328], w_in[..., 512:1280],
         w_in[..., 2328:3096], w_in[..., 1280:1304], zpad], axis=-1).astype(BF16)
    w_n = w_in[..., 1816:3096].astype(BF16)
    w_it = jnp.swapaxes(w_in, 1, 2)
    w_t = jnp.concatenate(
        [w_it[:, 0:512], w_it[:, 1304:1816], w_it[:, 1280:1304],
         jnp.zeros((depth, N_TR - 1048, D_MODEL), w_in.dtype)], axis=1).astype(BF16)
    w_kv = w_it[:, 512:1280].astype(BF16)
    w_o = w_out.astype(BF16)
    wck = jnp.tile(w_cmp_k, (1, 1, KV_A))
    wvt_p = jnp.tile(jnp.swapaxes(w_cmp_v, 1, 2), (1, KV_A, t // L_CMP))
    wkt_d = jnp.tile(jnp.swapaxes(w_cmp_k, 1, 2), (1, KV_A, PAGE // L_CMP))
    wvt_d = jnp.tile(jnp.swapaxes(w_cmp_v, 1, 2), (1, KV_A, PAGE // L_CMP))
    gones = jnp.asarray(_group_ones_np(), BF16)
    e2 = jnp.asarray(_gate_expand_np(), BF16)
    e2t = jnp.asarray(_gate_expand_np().T, BF16)
    ett = jnp.asarray(_expand_np(ns, 128).T, BF16)
    e_d = jnp.asarray(_expand_np(nps, nps), BF16)
    seg_p = jnp.asarray(_segment_np(t, ns), BF16)
    seg_d = jnp.asarray(np.tile(_segment_np(pages_c * PAGE, pages_c * PAGE // SEL_BLOCK), (2, 1)), BF16)

    kk, qq = np.arange(TK)[:, None], np.arange(TQ)[None, :]
    bk_near = np.concatenate([_bucket_np(d0 + qq - kk) for d0 in (0, TQ)], axis=0)
    bnear = _bias_lookup(rel_bias, bk_near, shift=True, scale=LOG2E).reshape(H_A, 2, TK, TQ)
    bnear = jnp.where(jnp.asarray((kk > qq)[None, None] & (np.arange(2) == 0)[None, :, None, None]), NEG, bnear)
    blk_eo = np.concatenate([2 * np.arange(ns), 2 * np.arange(ns) + 1])[:, None]
    bct = _bias_lookup(rel_bias, _bucket_np(np.arange(t)[None, :] - (blk_eo * L_CMP + L_CMP - 1)))
    qd = past + np.arange(dt)[:, None]
    blk_d = np.concatenate([2 * np.arange(nps), 2 * np.arange(nps) + 1])[None, :]
    bcd = _bias_lookup(rel_bias, _bucket_np(qd - (blk_d * L_CMP + L_CMP - 1))).reshape(rows, 2 * nps)
    blast = _bias_lookup(rel_bias, _bucket_np(qd - (past - PAGE + np.arange(PAGE))[None, :]), shift=True, scale=LOG2E)
    bnew = _bias_lookup(rel_bias, _bucket_np(np.arange(dt)[:, None] - np.arange(128)[None, :]), shift=True, scale=LOG2E)
    bwin = _bias_lookup(rel_bias, _bucket_np(WINDOW + np.arange(dt)[:, None] - np.arange(WINDOW)[None, :]),
                        shift=True, scale=LOG2E)
    dtabs = (bcd, blast.reshape(rows, PAGE), bnew.reshape(rows, 128), bwin.reshape(rows, WINDOW))

    pool = lambda a: jnp.transpose(a, (0, 1, 3, 4, 2)).reshape(a.shape[0], a.shape[1], KV_COLS, PAGE)
    kc_ch, vc_ch = _decode_compress(page_table, pool(cache_cmp_k), pool(cache_cmp_v), wkt_d, wvt_d, seg_d, pages_c)

    def eo_table(x):
        x = x.reshape(depth, db, nc, KV_COLS, 2, nps // nc).transpose(0, 1, 3, 4, 2, 5)
        return x.reshape(depth, db, KV_COLS, 2 * nps)

    kct, vct = eo_table(kc_ch), eo_table(vc_ch)
    pool_sk, pool_sv = pool(cache_sel_k), pool(cache_sel_v)
    win_k = jnp.transpose(cache_win_k, (0, 1, 3, 4, 2)).reshape(depth, db, KV_COLS, WINDOW)
    win_v = jnp.transpose(cache_win_v, (0, 1, 3, 4, 2)).reshape(depth, db, KV_COLS, WINDOW)

    hp = x_prompt.reshape(b * t, D_MODEL)
    hs = x_sample.reshape(db * dt, D_MODEL)
    zeros_b = jnp.zeros((b, CONV_B - 1, W_B), F32)
    zeros_c = jnp.zeros((b, CONV_C - 1, W_C), F32)
    zeros_h = jnp.zeros((b, 1, W_C), F32)
    tm_p, tm_s = _row_tile(t, 512, 128), _row_tile(db * dt, 512)
    tt_p = _row_tile(t, 256)
    nat_cols = (NAT_GLU, NAT_ZB, NAT_XC, NAT_ZC)
    dec_cols = (COL_GLU, COL_ZB, COL_XC, COL_ZC)

    kv_states = [jnp.zeros((depth, b, KV_COLS, t), F32) for _ in range(N_KV)]
    p_states, s_states = [], []
    for l in range(depth):
        row = lambda a: a[l][None, :]
        lw = dict(cbw=conv_b_w[l], cbb=row(conv_b_b), gng=row(gn_gain), gnb=row(gn_bias), wpw=w_pw_b[l].astype(BF16),
                  ccw=conv_c_w[l], ccb=row(conv_c_b), wa=_block_diag(w_lru_a[l]).astype(BF16), ba=row(b_lru_a),
                  wx=_block_diag(w_lru_x[l]).astype(BF16), bx=row(b_lru_x), lam=row(lru_lambda))
        proj_n, proj_t, kv_states = _project_prompt(hp, row(g_pre), w_n[l], w_t[l], w_kv[l], kv_states, l, b, t, tm_p)
        a_out = _attn_prompt(l, proj_t, kv_states, wck[l], wvt_p[l], seg_p, bnear, bct, ett, e2t, b, t)
        b_out, c_out, cb, cc, hc = _mixers(proj_n, nat_cols, zeros_b, zeros_c, zeros_h, lw, gones, b, t, tt_p)
        hp = _output(hp, a_out, b_out, c_out, w_o[l], row(g_post), tm_p)
        p_states.append((cb, cc, hc[:, 0]))
        proj = _project(hs, row(g_pre), w_r[l], tm_s)
        a_out, wk_n, wv_n = _attn_decode(l, page_table, proj, kct, vct, pool_sk, pool_sv, win_k, win_v,
                                         dtabs, e_d, e2, pages, dt)
        b_out, c_out, cb, cc, hc = _mixers(proj, dec_cols, state_conv_b[l], state_conv_c[l],
                                           state_rglru[l][:, None, :], lw, gones, db, dt, dt)
        hs = _output(hs, a_out, b_out, c_out, w_o[l], row(g_post), tm_s)
        kv = proj[:, COL_KV:COL_KV + 6 * KV_COLS].reshape(db, dt, 6, KV_A, HEAD_DIM)
        s_states.append((kv[:, :, 0], kv[:, :, 1], kv[:, :, 2], kv[:, :, 3],
                         wk_n, wv_n, cb, cc, hc[:, 0]))

    cb_p, cc_p, h_p = [jnp.stack(a) for a in zip(*p_states)]
    kv_states = kv_states[:4] + [a[..., t - WINDOW:] for a in kv_states[4:]]
    ck_p, cv_p, sk_p, sv_p, wk_p, wv_p = [
        jnp.transpose(a.reshape(depth, b, KV_A, HEAD_DIM, a.shape[-1]), (0, 1, 4, 2, 3)) for a in kv_states]
    ck_s, cv_s, sk_s, sv_s, wk_s, wv_s, cb_s, cc_s, h_s = [jnp.stack(a) for a in zip(*s_states)]
    wk_s, wv_s = [jnp.transpose(a.reshape(depth, db, KV_A, HEAD_DIM, WINDOW), (0, 1, 4, 2, 3)) for a in (wk_s, wv_s)]
    return (hp.reshape(b, t, D_MODEL), hs.reshape(db, dt, D_MODEL),
            ck_p, ck_s, cv_p, cv_s, sk_p, sk_s, sv_p, sv_s, wk_p, wk_s, wv_p, wv_s,
            cb_p, cb_s, cc_p, cc_s, h_p, h_s)
```

```python
import functools
import math

import numpy as np
import jax
import jax.numpy as jnp
from jax import lax
from jax.experimental import pallas as pl
from jax.experimental.pallas import tpu as pltpu

F32 = jnp.float32
BF16 = jnp.bfloat16

D_MODEL = 1024
HEAD_DIM = 64
W_A = D_MODEL // 2
W_B = D_MODEL // 4
W_C = D_MODEL // 4
H_A = W_A // HEAD_DIM
KV_A = 2
REP_A = H_A // KV_A
KV_COLS = KV_A * HEAD_DIM
L_CMP = 32
SEL_BLOCK = 64
TOP_K = 16
FORCE_SCORE = 1.0e4
WINDOW = 512
PAGE = 128
CONV_B = 31
CONV_C = 4
GN_GROUP = W_B // 4
LRU_C = 8.0
NUM_BUCKETS = 32
MAX_DISTANCE = 128
SM_SCALE = HEAD_DIM ** -0.5
LOG2E = 1.4426950408889634
EPS = 1e-6
NEG = -1e30

TQ = 256
TK = 256
V_ROWS = HEAD_DIM + 16
D_IN = 3096
D_IN_PAD = 3200
COL_Q, COL_ZA, COL_GLU, COL_KV, COL_ZB, COL_XC, COL_ZC, COL_GATE = 0, 512, 1024, 1536, 2304, 2560, 2816, 3072
N_NAT = 1280
NAT_GLU, NAT_ZB, NAT_XC, NAT_ZC = 0, 512, 768, 1024
N_TR = 1152
TR_Q, TR_ZA, TR_GATE = 0, 512, 1024
N_KV = 6
VMEM_LIMIT = 56 * 1024 * 1024


def _cparams(sem):
    return pltpu.CompilerParams(dimension_semantics=sem, vmem_limit_bytes=VMEM_LIMIT)


def _bucket_np(dist):
    n = np.maximum(dist, 0)
    max_exact = NUM_BUCKETS // 2
    nf = np.maximum(n, 1).astype(np.float32)
    large = max_exact + (np.log(nf / np.float32(max_exact)) / np.float32(math.log(MAX_DISTANCE / max_exact))
                         * np.float32(NUM_BUCKETS - max_exact)).astype(np.int32)
    return np.where(n < max_exact, n, np.minimum(large, NUM_BUCKETS - 1)).astype(np.int32)


def _expand_np(n_blocks, rows):
    e = np.zeros((rows, n_blocks * SEL_BLOCK), np.float32)
    for m in range(n_blocks):
        e[m, m * SEL_BLOCK:(m + 1) * SEL_BLOCK] = 1.0
    return e


def _gate_expand_np():
    e = np.zeros((128, 3 * W_A), np.float32)
    for br in range(3):
        for h in range(H_A):
            e[br * H_A + h, br * W_A + h * HEAD_DIM: br * W_A + (h + 1) * HEAD_DIM] = 1.0
    return e


def _group_ones_np():
    g = np.zeros((W_B, W_B), np.float32)
    for k in range(W_B // GN_GROUP):
        g[k * GN_GROUP:(k + 1) * GN_GROUP, k * GN_GROUP:(k + 1) * GN_GROUP] = 1.0
    return g


def _segment_np(n_rows, n_half):
    s = np.zeros((n_rows, 2 * n_half), np.float32)
    blk = np.arange(n_rows) // L_CMP
    s[np.arange(n_rows), np.where(blk % 2 == 0, blk // 2, n_half + blk // 2)] = 1.0
    return s


def _split2(a):
    hi = a.astype(BF16)
    return hi, (a - hi.astype(F32)).astype(BF16)


def _split3(a):
    hi = a.astype(BF16)
    r1 = a - hi.astype(F32)
    mid = r1.astype(BF16)
    return hi, mid, (r1 - mid.astype(F32)).astype(BF16)


def _dot(a, b):
    return jnp.dot(a, b, preferred_element_type=F32)


def _dot_nt(a, b):
    return lax.dot_general(a, b, (((1,), (1,)), ((), ())), preferred_element_type=F32)


def _dot_exact_rhs(a, b_bf16):
    hi, mid, lo = _split3(a)
    return _dot(hi, b_bf16) + _dot(mid, b_bf16) + _dot(lo, b_bf16)


def _dot_exact_lhs(a_bf16, b):
    hi, mid, lo = _split3(b)
    return _dot(a_bf16, hi) + _dot(a_bf16, mid) + _dot(a_bf16, lo)


def _dot_f32(a, b):
    ah, al = _split2(a)
    bh, bl = _split2(b)
    return _dot(ah, bh) + _dot(al, bh) + _dot(ah, bl)


def _sigmoid(x):
    return 1.0 / (1.0 + jnp.exp(-x))


def _silu(x):
    return x * _sigmoid(x)


def _expm1(x):
    u = jnp.exp(x)
    safe = jnp.where((u == 1.0) | (u == 0.0), 0.5, u)
    return jnp.where(u == 1.0, x, jnp.where(u == 0.0, -1.0, (safe - 1.0) * x / jnp.log(safe)))


def _rank_select(score, idx, n, axis):
    rank = jnp.zeros(score.shape, F32)
    for m in range(n):
        sm = score[m:m + 1, :] if axis == 0 else score[:, m:m + 1]
        beats = (sm > score) | ((sm == score) & (idx > m))
        rank = rank + jnp.where(beats, 1.0, 0.0)
    return rank


def _bias_kernel(rb_ref, bk_ref, o_ref, *, shift, scale):
    h = pl.program_id(0)
    bk = bk_ref[...]
    acc = jnp.zeros(bk.shape, F32)
    for b in range(NUM_BUCKETS):
        acc = jnp.where(bk == b, rb_ref[b, h], acc)
    if shift:
        acc = acc - rb_ref[NUM_BUCKETS - 1, h]
    o_ref[...] = acc * scale


def _bias_lookup(rel_bias, bucket, shift=False, scale=1.0):
    r, c = bucket.shape
    return pl.pallas_call(
        functools.partial(_bias_kernel, shift=shift, scale=scale),
        grid=(H_A,),
        in_specs=[pl.BlockSpec(memory_space=pltpu.SMEM),
                  pl.BlockSpec((r, c), lambda h: (0, 0))],
        out_specs=pl.BlockSpec((None, r, c), lambda h: (h, 0, 0)),
        out_shape=jax.ShapeDtypeStruct((H_A, r, c), F32),
        compiler_params=_cparams(("arbitrary",)),
        name="bias_lookup",
    )(rel_bias, jnp.asarray(bucket))


def _rms_bf16(x_ref, g_ref):
    x = x_ref[...]
    ms = jnp.mean(x * x, axis=-1, keepdims=True)
    return (x * lax.rsqrt(ms + EPS) * g_ref[...]).astype(BF16)


def _proj_kernel(x_ref, g_ref, w_ref, o_ref):
    u = _rms_bf16(x_ref, g_ref)
    step = 640
    for c in range(0, o_ref.shape[1], step):
        o_ref[:, c:c + step] = _dot_nt(u, w_ref[c:c + step, :])


def _project(h, g, w, tm):
    m = h.shape[0]
    return pl.pallas_call(
        _proj_kernel,
        grid=(m // tm,),
        in_specs=[pl.BlockSpec((tm, D_MODEL), lambda i: (i, 0)),
                  pl.BlockSpec((1, D_MODEL), lambda i: (0, 0)),
                  pl.BlockSpec((D_IN_PAD, D_MODEL), lambda i: (0, 0))],
        out_specs=pl.BlockSpec((tm, D_IN_PAD), lambda i: (i, 0)),
        out_shape=jax.ShapeDtypeStruct((m, D_IN_PAD), F32),
        compiler_params=_cparams(("arbitrary",)),
        name="project",
    )(h, g, w)


def _proj2_kernel(x_ref, g_ref, wn_ref, wt_ref, wkv_ref, *refs):
    on_ref, ot_ref = refs[N_KV], refs[N_KV + 1]
    st_refs = refs[N_KV + 2:]
    u = _rms_bf16(x_ref, g_ref)
    for c in range(0, N_NAT, 640):
        on_ref[:, c:c + 640] = _dot_nt(u, wn_ref[c:c + 640, :])
    for c in range(0, N_TR, 384):
        ot_ref[c:c + 384, :] = _dot_nt(wt_ref[c:c + 384, :], u)
    for k in range(0, N_KV, 2):
        kv = _dot_nt(wkv_ref[k * KV_COLS:(k + 2) * KV_COLS, :], u)
        st_refs[k][...] = kv[0:KV_COLS]
        st_refs[k + 1][...] = kv[KV_COLS:2 * KV_COLS]


def _project_prompt(h, g, wn, wt, wkv, states, layer, b, t, tm):
    per = t // tm
    st_spec = pl.BlockSpec((None, None, KV_COLS, tm), lambda i: (layer, i // per, 0, i % per))
    outs = pl.pallas_call(
        _proj2_kernel,
        grid=(b * per,),
        in_specs=[pl.BlockSpec((tm, D_MODEL), lambda i: (i, 0)),
                  pl.BlockSpec((1, D_MODEL), lambda i: (0, 0)),
                  pl.BlockSpec((N_NAT, D_MODEL), lambda i: (0, 0)),
                  pl.BlockSpec((N_TR, D_MODEL), lambda i: (0, 0)),
                  pl.BlockSpec((N_KV * KV_COLS, D_MODEL), lambda i: (0, 0))]
                 + [pl.BlockSpec(memory_space=pl.ANY)] * N_KV,
        out_specs=[pl.BlockSpec((tm, N_NAT), lambda i: (i, 0)),
                   pl.BlockSpec((None, N_TR, tm), lambda i: (i // per, 0, i % per))] + [st_spec] * N_KV,
        out_shape=[jax.ShapeDtypeStruct((b * t, N_NAT), F32), jax.ShapeDtypeStruct((b, N_TR, t), F32)]
                  + [jax.ShapeDtypeStruct(a.shape, a.dtype) for a in states],
        input_output_aliases={5 + k: 2 + k for k in range(N_KV)},
        compiler_params=_cparams(("arbitrary",)),
        name="project_prompt",
    )(h, g, wn, wt, wkv, *states)
    return outs[0], outs[1], list(outs[2:])


def _out_kernel(h_ref, a_ref, b_ref, c_ref, w_ref, g_ref, o_ref):
    y = (_dot(a_ref[...].astype(BF16), w_ref[0:W_A, :])
         + _dot(b_ref[...].astype(BF16), w_ref[W_A:W_A + W_B, :])
         + _dot(c_ref[...].astype(BF16), w_ref[W_A + W_B:, :]))
    ms = jnp.mean(y * y, axis=-1, keepdims=True)
    o_ref[...] = h_ref[...] + y * lax.rsqrt(ms + EPS) * g_ref[...]


def _output(h, a, b, c, w, g, tm):
    m = h.shape[0]
    return pl.pallas_call(
        _out_kernel,
        grid=(m // tm,),
        in_specs=[pl.BlockSpec((tm, D_MODEL), lambda i: (i, 0)),
                  pl.BlockSpec((tm, W_A), lambda i: (i, 0)),
                  pl.BlockSpec((tm, W_B), lambda i: (i, 0)),
                  pl.BlockSpec((tm, W_C), lambda i: (i, 0)),
                  pl.BlockSpec((D_MODEL, D_MODEL), lambda i: (0, 0)),
                  pl.BlockSpec((1, D_MODEL), lambda i: (0, 0))],
        out_specs=pl.BlockSpec((tm, D_MODEL), lambda i: (i, 0)),
        out_shape=jax.ShapeDtypeStruct((m, D_MODEL), F32),
        compiler_params=_cparams(("arbitrary",)),
        name="output",
    )(h, a, b, c, w, g)


def _softmax_steps(scores, vts, states):
    stats = []
    for s, st in zip(scores, states):
        m_new = jnp.max(s, axis=0, keepdims=True)
        alpha = None
        if st is not None:
            m_new = jnp.maximum(st[0], m_new)
            alpha = jnp.exp2(st[0] - m_new)
        stats.append((m_new, alpha, jnp.exp2(s - m_new).astype(BF16)))
    out = []
    for (m_new, alpha, p), vt, st in zip(stats, vts, states):
        pv = _dot(vt, p)
        out.append((m_new, pv if st is None else alpha * st[1] + pv))
    return out


def _softmax_finish(states):
    return jnp.concatenate([acc[0:HEAD_DIM] / acc[HEAD_DIM:HEAD_DIM + 1] for (_, acc) in states], axis=0)


def _attn_prompt_kernel(qt_ref, zat_ref, gt_ref, kct_ref, vct_ref, kst_ref, vst_ref, kwt_ref, vwt_ref,
                        wk_ref, wvt_ref, seg_ref, bnear_ref, bct_ref, ett_ref, e2t_ref,
                        o_ref, kc_s, vct_s, ks_s, kw_s, vst_s, vwt_s, *, ns, nt):
    i = pl.program_id(1)
    nsp = -(-ns // 8) * 8

    @pl.when(i == 0)
    def _():
        x3 = kct_ref[...].T.reshape(ns, SEL_BLOCK, KV_COLS)
        w = wk_ref[...]
        kc_s[0:ns, :] = jnp.sum(x3[:, :L_CMP, :] * w[None], axis=1)
        kc_s[ns:2 * ns, :] = jnp.sum(x3[:, L_CMP:, :] * w[None], axis=1)
        hi, mid = _split2(vct_ref[...] * wvt_ref[...])
        seg = seg_ref[...]
        vct_s[...] = _dot(hi, seg) + _dot(mid, seg)
        ks_n, kw_n = kst_ref[...].T, kwt_ref[...].T
        for g in range(KV_A):
            ks_s[g] = jnp.concatenate([ks_n[:, g * HEAD_DIM:(g + 1) * HEAD_DIM].astype(BF16), ett_ref[...]], axis=1)
            kw_s[g] = jnp.concatenate([kw_n[:, g * HEAD_DIM:(g + 1) * HEAD_DIM].astype(BF16),
                                       jnp.zeros(ett_ref.shape, BF16)], axis=1)
        ones_row = jnp.where(lax.broadcasted_iota(jnp.int32, (V_ROWS - HEAD_DIM, TK), 0) == 0, 1.0, 0.0).astype(BF16)
        for c in range(nt):
            for g in range(KV_A):
                gs = slice(g * HEAD_DIM, (g + 1) * HEAD_DIM)
                vst_s[c, g] = jnp.concatenate([vst_ref[gs, c * TK:(c + 1) * TK].astype(BF16), ones_row], axis=0)
                vwt_s[c, g] = jnp.concatenate([vwt_ref[gs, c * TK:(c + 1) * TK].astype(BF16), ones_row], axis=0)

    qt = qt_ref[...]
    q0 = i * TQ

    row_c = lax.broadcasted_iota(jnp.int32, (2 * ns, TQ), 0)
    qpos_c = q0 + lax.broadcasted_iota(jnp.int32, (2 * ns, TQ), 1)
    blk_c = jnp.where(row_c < ns, 2 * row_c, 2 * (row_c - ns) + 1)
    mask_c = qpos_c >= blk_c * L_CMP + (L_CMP - 1)
    blk_t = lax.broadcasted_iota(jnp.int32, (nsp, TQ), 0)
    cur_t = (q0 + lax.broadcasted_iota(jnp.int32, (nsp, TQ), 1)) // SEL_BLOCK
    forced = (blk_t == 0) | ((blk_t <= cur_t) & (blk_t > cur_t - 2))
    kq_gap = (lax.broadcasted_iota(jnp.int32, (TK, TQ), 0) - lax.broadcasted_iota(jnp.int32, (TK, TQ), 1))
    sel_neg = []

    oc_parts = []
    for g in range(KV_A):
        kh, kl = _split2(kc_s[:, g * HEAD_DIM:(g + 1) * HEAD_DIM])
        vcg = vct_s[g * HEAD_DIM:(g + 1) * HEAD_DIM, :].astype(BF16)
        imp = jnp.zeros((2 * ns, TQ), F32)
        for r in range(REP_A):
            h = g * REP_A + r
            qh, ql = _split2(qt[h * HEAD_DIM:(h + 1) * HEAD_DIM, :])
            s = (_dot(kh, qh) + _dot(kl, qh) + _dot(kh, ql)) * SM_SCALE + bct_ref[h]
            s = jnp.where(mask_c, s, NEG)
            e = jnp.exp(s - jnp.max(s, axis=0, keepdims=True))
            p = e / jnp.sum(e, axis=0, keepdims=True)
            p = jnp.where(mask_c, p, 0.0)
            oc_parts.append(_dot(vcg, p.astype(BF16)))
            imp = imp + p
        simp = imp[0:ns, :] + imp[ns:2 * ns, :]
        if nsp > ns:
            simp = jnp.concatenate([simp, jnp.zeros((nsp - ns, TQ), F32)], axis=0)
        score = jnp.where(forced, FORCE_SCORE, simp)
        score = jnp.where(blk_t <= cur_t, score, NEG)
        rank = _rank_select(score, blk_t, ns, 0)
        sel = jnp.where((rank < TOP_K) & (blk_t <= cur_t), 1.0, 0.0)
        sel = jnp.concatenate([sel, jnp.zeros((128 - nsp, TQ), F32)], axis=0)
        sel_neg.append(((sel - 1.0) * (-NEG)).astype(BF16))
    oc_t = jnp.concatenate(oc_parts, axis=0)

    ok1, ok2 = i >= 1, i >= 2
    c1, c2 = jnp.maximum(i - 1, 0), jnp.maximum(i - 2, 0)
    n_far = jnp.maximum(i - 1, 0)
    group = [h // REP_A for h in range(H_A)]
    qs = [jnp.concatenate([(qt[h * HEAD_DIM:(h + 1) * HEAD_DIM, :] * (SM_SCALE * LOG2E)).astype(BF16),
                           sel_neg[group[h]]], axis=0) for h in range(H_A)]

    def kv_sel(g, c):
        r0 = pl.multiple_of(c * TK, TK)
        return ks_s[g, pl.ds(r0, TK), :], vst_s[c, g]

    def kv_win(g, c):
        r0 = pl.multiple_of(c * TK, TK)
        return kw_s[g, pl.ds(r0, TK), :], vwt_s[c, g]

    def far(c, carry):
        kv = [kv_sel(g, c) for g in range(KV_A)]
        scores = [_dot(kv[group[h]][0], qs[h]) for h in range(H_A)]
        return tuple(_softmax_steps(scores, [kv[group[h]][1] for h in range(H_A)], carry))

    init = tuple((jnp.full((1, TQ), NEG, F32), jnp.zeros((V_ROWS, TQ), F32)) for _ in range(H_A))
    sel_st = list(lax.fori_loop(0, n_far, far, init))
    win_st = [None] * H_A
    for c, ok, d_idx in ((c1, ok1, 1), (i, None, 0)):
        kvs = [kv_sel(g, c) for g in range(KV_A)]
        kvw = [kv_win(g, c) for g in range(KV_A)]
        kcat = [jnp.concatenate([kvs[g][0], kvw[g][0]], axis=0) for g in range(KV_A)]
        s2 = [_dot(kcat[group[h]], qs[h]) for h in range(H_A)]
        scores, vts, states = [], [], []
        for h in range(H_A):
            bias = bnear_ref[h, d_idx] if ok is None else jnp.where(ok, bnear_ref[h, d_idx], NEG)
            scores += [s2[h][0:TK] + bias, s2[h][TK:2 * TK] + bias]
            vts += [kvs[group[h]][1], kvw[group[h]][1]]
            states += [sel_st[h], win_st[h]]
        new = _softmax_steps(scores, vts, states)
        sel_st, win_st = new[0::2], new[1::2]
    kvw = [kv_win(g, c2) for g in range(KV_A)]
    far_add = jnp.where((kq_gap > 0) & ok2, 0.0, NEG)
    win_st = _softmax_steps([_dot(kvw[group[h]][0], qs[h]) + far_add for h in range(H_A)],
                            [kvw[group[h]][1] for h in range(H_A)], win_st)
    os_t = _softmax_finish(sel_st)
    ow_t = _softmax_finish(win_st)

    ge = _dot_exact_lhs(e2t_ref[...], _sigmoid(gt_ref[...]))
    o_t = ge[0:W_A] * oc_t + ge[W_A:2 * W_A] * os_t + ge[2 * W_A:3 * W_A] * ow_t
    o_ref[...] = (o_t * _silu(zat_ref[...])).T


def _attn_prompt(layer, proj_t, states, wk, wvt, seg, bnear, bct, ett, e2t, b, t):
    ns, nt = t // SEL_BLOCK, t // TQ
    st_spec = pl.BlockSpec((None, None, KV_COLS, t), lambda bi, i: (layer, bi, 0, 0))

    def tr_spec(rows, row0, width):
        return pl.BlockSpec((None, rows, width), lambda bi, i: (bi, row0 // rows, i))

    def const_spec(a):
        return pl.BlockSpec(a.shape, lambda bi, i: (0,) * a.ndim)

    in_specs = [tr_spec(W_A, TR_Q, TQ), tr_spec(W_A, TR_ZA, TQ), tr_spec(128, TR_GATE, TQ),
                st_spec, st_spec, st_spec, st_spec, st_spec, st_spec,
                const_spec(wk), const_spec(wvt), const_spec(seg), const_spec(bnear),
                pl.BlockSpec((H_A, 2 * ns, TQ), lambda bi, i: (0, 0, i)),
                const_spec(ett), const_spec(e2t)]
    scratch = [pltpu.VMEM((2 * ns, KV_COLS), F32), pltpu.VMEM((KV_COLS, 2 * ns), F32),
               pltpu.VMEM((KV_A, t, HEAD_DIM + 128), BF16), pltpu.VMEM((KV_A, t, HEAD_DIM + 128), BF16),
               pltpu.VMEM((nt, KV_A, V_ROWS, TK), BF16), pltpu.VMEM((nt, KV_A, V_ROWS, TK), BF16)]
    return pl.pallas_call(
        functools.partial(_attn_prompt_kernel, ns=ns, nt=nt),
        grid=(b, nt),
        in_specs=in_specs,
        out_specs=pl.BlockSpec((TQ, W_A), lambda bi, i: (bi * nt + i, 0)),
        out_shape=jax.ShapeDtypeStruct((b * t, W_A), F32),
        scratch_shapes=scratch,
        compiler_params=_cparams(("arbitrary", "arbitrary")),
        name="attn_prompt",
    )(proj_t, proj_t, proj_t, *states, wk, wvt, seg, bnear, bct, ett, e2t)


def _dcmp_kernel(pt_ref, *refs, pages):
    del pt_ref
    k_refs, v_refs = refs[0:pages], refs[pages:2 * pages]
    wkt_ref, wvt_ref, seg_ref = refs[2 * pages:2 * pages + 3]
    ko_ref, vo_ref = refs[2 * pages + 3:]
    for src, w_ref, o_ref in ((k_refs, wkt_ref, ko_ref), (v_refs, wvt_ref, vo_ref)):
        for l in range(o_ref.shape[0]):
            w = w_ref[l]
            parts = [_split2(src[p][l] * w) for p in range(pages)]
            x = jnp.concatenate([hi for hi, _ in parts] + [lo for _, lo in parts], axis=1)
            o_ref[l] = _dot(x, seg_ref[...])


def _decode_compress(page_table, pool_k, pool_v, wkt, wvt, seg, pages):
    depth, db, n_pages = pool_k.shape[0], page_table.shape[0], page_table.shape[1]
    nc = n_pages // pages
    ncol = seg.shape[1]

    def page_spec(p):
        return pl.BlockSpec((depth, None, KV_COLS, PAGE), lambda b, c, pt: (0, pt[b, c * pages + p], 0, 0))

    w_spec = pl.BlockSpec((depth, KV_COLS, PAGE), lambda b, c, pt: (0, 0, 0))
    s_spec = pl.BlockSpec(seg.shape, lambda b, c, pt: (0, 0))
    o_spec = pl.BlockSpec((depth, None, None, KV_COLS, ncol), lambda b, c, pt: (0, b, c, 0, 0))
    o_shape = jax.ShapeDtypeStruct((depth, db, nc, KV_COLS, ncol), F32)
    return pl.pallas_call(
        functools.partial(_dcmp_kernel, pages=pages),
        grid_spec=pltpu.PrefetchScalarGridSpec(
            num_scalar_prefetch=1,
            grid=(db, nc),
            in_specs=[page_spec(p) for p in range(pages)] * 2 + [w_spec, w_spec, s_spec],
            out_specs=[o_spec] * 2),
        out_shape=[o_shape] * 2,
        compiler_params=_cparams(("arbitrary", "arbitrary")),
        name="decode_compress",
    )(page_table, *([pool_k] * pages), *([pool_v] * pages), wkt, wvt, seg)


def _softmax_rows(qbd, kt, add, m, l):
    s = _dot(qbd, kt) + add
    m_new = jnp.maximum(m, jnp.max(s, axis=-1, keepdims=True))
    alpha = jnp.exp2(m - m_new)
    p = jnp.exp2(s - m_new)
    return m_new, alpha, alpha * l + jnp.sum(p, axis=-1, keepdims=True), p.astype(BF16)


def _attn_decode_kernel(pt_ref, *refs, pages, nps, nc, dt):
    del pt_ref
    (q_ref, za_ref, kv_ref, gate_ref, kct_ref, vct_ref) = refs[0:6]
    sk_refs, sv_refs = refs[6:6 + pages], refs[6 + pages:6 + 2 * pages]
    (wink_ref, winv_ref, bcd_ref, blast_ref, bnew_ref, bwin_ref, e_ref, e2_ref,
     o_ref, wk_o, wv_o, qbd_s, m_s, l_s, acc_s, madd_s, oc_s) = refs[6 + 2 * pages:]
    c = pl.program_id(1)
    ck = pages * PAGE
    rows = KV_A * REP_A * dt
    zpad = jnp.zeros((dt, HEAD_DIM), F32)

    @pl.when(c == 0)
    def _():
        q = q_ref[...]
        blocks = []
        for g in range(KV_A):
            for r in range(REP_A):
                h = g * REP_A + r
                piece = q[:, h * HEAD_DIM:(h + 1) * HEAD_DIM]
                blocks.append(jnp.concatenate([piece, zpad] if g == 0 else [zpad, piece], axis=1))
        qbd = jnp.concatenate(blocks, axis=0)
        qbd_s[...] = qbd
        s = _dot_f32(qbd, kct_ref[...]) * SM_SCALE + bcd_ref[...]
        e = jnp.exp(s - jnp.max(s, axis=-1, keepdims=True))
        p = e / jnp.sum(e, axis=-1, keepdims=True)
        oc_s[...] = _dot_nt(p.astype(BF16), vct_ref[...].astype(BF16))
        scores = []
        for g in range(KV_A):
            pe = [p[(g * REP_A + r) * dt:(g * REP_A + r + 1) * dt, 0:nps] for r in range(REP_A)]
            po = [p[(g * REP_A + r) * dt:(g * REP_A + r + 1) * dt, nps:2 * nps] for r in range(REP_A)]
            scores.append((pe[0] + pe[1] + pe[2] + pe[3]) + (po[0] + po[1] + po[2] + po[3]))
        simp = jnp.concatenate(scores, axis=0)
        blk = lax.broadcasted_iota(jnp.int32, simp.shape, 1)
        score = jnp.where((blk == 0) | (blk >= nps - 1), FORCE_SCORE, simp)
        rank = _rank_select(score, blk, nps, 1)
        sel = jnp.where(rank < TOP_K - 1, 1.0, 0.0).astype(BF16)
        madd = (_dot(sel, e_ref[...]) - 1.0) * (-NEG)
        madd = jnp.concatenate([madd[0:dt]] * REP_A + [madd[dt:2 * dt]] * REP_A, axis=0)
        for j in range(nc):
            tile = madd[:, j * ck:(j + 1) * ck]
            if j == nc - 1:
                tile = jnp.concatenate([tile[:, :ck - PAGE], tile[:, ck - PAGE:] + blast_ref[...]], axis=1)
            madd_s[j] = tile
        m_s[...] = jnp.full(m_s.shape, NEG, F32)
        l_s[...] = jnp.zeros(l_s.shape, F32)
        acc_s[...] = jnp.zeros(acc_s.shape, F32)

    qbd = (qbd_s[...] * (SM_SCALE * LOG2E)).astype(BF16)
    kt = jnp.concatenate([r[...].astype(BF16) for r in sk_refs], axis=1)
    vt = jnp.concatenate([r[...].astype(BF16) for r in sv_refs], axis=1)
    m, alpha, l, p = _softmax_rows(qbd, kt, madd_s[c], m_s[...], l_s[...])
    acc = alpha * acc_s[...] + _dot_nt(p, vt)
    m_s[...], l_s[...], acc_s[...] = m, l, acc

    @pl.when(c == nc - 1)
    def _():
        row_t = lax.broadcasted_iota(jnp.int32, (rows, 128), 0) % dt
        col = lax.broadcasted_iota(jnp.int32, (rows, 128), 1)
        new_add = jnp.where((col <= row_t) & (col < dt), bnew_ref[...], NEG)
        row_w = lax.broadcasted_iota(jnp.int32, (rows, WINDOW), 0) % dt
        col_w = lax.broadcasted_iota(jnp.int32, (rows, WINDOW), 1)
        win_add = jnp.where(col_w > row_w, bwin_ref[...], NEG)
        pad = jnp.zeros((128 - dt, KV_COLS), F32)

        def new_tile(src):
            return jnp.concatenate([kv_ref[:, src * KV_COLS:(src + 1) * KV_COLS], pad], axis=0).astype(BF16)

        def attend_nat(kn, vn, add, m, l, acc):
            s = _dot_nt(qbd, kn) + add
            m_new = jnp.maximum(m, jnp.max(s, axis=-1, keepdims=True))
            alpha = jnp.exp2(m - m_new)
            p = jnp.exp2(s - m_new)
            return m_new, alpha * l + jnp.sum(p, axis=-1, keepdims=True), alpha * acc + _dot(p.astype(BF16), vn)

        m2, l2, acc2 = attend_nat(new_tile(2), new_tile(3), new_add, m, l, acc)
        os_full = acc2 / l2
        init = (jnp.full((rows, 1), NEG, F32), jnp.zeros((rows, 1), F32), jnp.zeros((rows, KV_COLS), F32))
        m_w, _, l_w, p_w = _softmax_rows(qbd, wink_ref[...].astype(BF16), win_add, init[0], init[1])
        carry = (m_w, l_w, _dot_nt(p_w, winv_ref[...].astype(BF16)))
        m3, l3, acc3 = attend_nat(new_tile(4), new_tile(5), new_add, *carry)
        ow_full = acc3 / l3

        def heads(full):
            parts = []
            for g in range(KV_A):
                for r in range(REP_A):
                    r0 = (g * REP_A + r) * dt
                    parts.append(full[r0:r0 + dt, g * HEAD_DIM:(g + 1) * HEAD_DIM])
            return jnp.concatenate(parts, axis=1)

        sg = _sigmoid(gate_ref[...])
        ge = _dot_exact_rhs(sg, e2_ref[...])
        o = ge[:, 0:W_A] * heads(oc_s[...]) + ge[:, W_A:2 * W_A] * heads(os_full) + ge[:, 2 * W_A:] * heads(ow_full)
        o_ref[...] = o * _silu(za_ref[...])
        for src, w_ref, wo_ref in ((4, wink_ref, wk_o), (5, winv_ref, wv_o)):
            new_t = jnp.concatenate([kv_ref[:, src * KV_COLS:(src + 1) * KV_COLS], pad], axis=0).T
            wo_ref[...] = jnp.concatenate([w_ref[:, dt:], new_t[:, 0:dt]], axis=1)


def _attn_decode(layer, page_table, proj, kct, vct, pool_sk, pool_sv, win_k, win_v, tabs, e_sel, e2, pages, dt):
    db, n_pages = page_table.shape
    nps = n_pages * (PAGE // SEL_BLOCK)
    nc = n_pages // pages
    ck = pages * PAGE
    rows = KV_A * REP_A * dt
    bcd, blast, bnew, bwin = tabs

    def row_spec(width, col):
        return pl.BlockSpec((dt, width), lambda b, c, pt: (b, col // width))

    def page_spec(p):
        return pl.BlockSpec((None, None, KV_COLS, PAGE), lambda b, c, pt: (layer, pt[b, c * pages + p], 0, 0))

    cmp_spec = pl.BlockSpec((None, None, KV_COLS, 2 * nps), lambda b, c, pt: (layer, b, 0, 0))
    win_spec = pl.BlockSpec((None, None, KV_COLS, WINDOW), lambda b, c, pt: (layer, b, 0, 0))

    def const_spec(a):
        return pl.BlockSpec(a.shape, lambda b, c, pt: (0,) * a.ndim)

    in_specs = ([row_spec(W_A, COL_Q), row_spec(W_A, COL_ZA), row_spec(6 * KV_COLS, COL_KV), row_spec(128, COL_GATE)]
                + [cmp_spec] * 2 + [page_spec(p) for p in range(pages)] * 2 + [win_spec] * 2
                + [const_spec(a) for a in (bcd, blast, bnew, bwin, e_sel, e2)])
    out_specs = [pl.BlockSpec((dt, W_A), lambda b, c, pt: (b, 0)),
                 pl.BlockSpec((None, KV_COLS, WINDOW), lambda b, c, pt: (b, 0, 0)),
                 pl.BlockSpec((None, KV_COLS, WINDOW), lambda b, c, pt: (b, 0, 0))]
    out_shape = [jax.ShapeDtypeStruct((db * dt, W_A), F32),
                 jax.ShapeDtypeStruct((db, KV_COLS, WINDOW), F32),
                 jax.ShapeDtypeStruct((db, KV_COLS, WINDOW), F32)]
    scratch = [pltpu.VMEM((rows, KV_COLS), F32), pltpu.VMEM((rows, 1), F32), pltpu.VMEM((rows, 1), F32),
               pltpu.VMEM((rows, KV_COLS), F32), pltpu.VMEM((nc, rows, ck), F32), pltpu.VMEM((rows, KV_COLS), F32)]
    return pl.pallas_call(
        functools.partial(_attn_decode_kernel, pages=pages, nps=nps, nc=nc, dt=dt),
        grid_spec=pltpu.PrefetchScalarGridSpec(
            num_scalar_prefetch=1, grid=(db, nc), in_specs=in_specs, out_specs=out_specs, scratch_shapes=scratch),
        out_shape=out_shape,
        compiler_params=_cparams(("arbitrary", "arbitrary")),
        name="attn_decode",
    )(page_table, proj, proj, proj, proj, kct, vct, *([pool_sk] * pages), *([pool_sv] * pages),
      win_k, win_v, bcd, blast, bnew, bwin, e_sel, e2)


EXT_B0 = 32
EXT_C0 = 8
CONV_ROWS = 32
SH_PAD = (CONV_B - 1) // 8 * 8


def _mixer_kernel(glu_ref, zb_ref, xc_ref, zc_ref, bufb_ref, bufc_ref, h0_ref,
                  cbw_ref, cbb_ref, gng_ref, gnb_ref, wpw_ref, gones_ref,
                  ccw_ref, ccb_ref, wa_ref, ba_ref, wx_ref, bx_ref, lam_ref,
                  bo_ref, co_ref, cbs_ref, ccs_ref, hs_ref,
                  extb, extc, hcar, cbuf, shb, *, tt):
    j = pl.program_id(1)
    nb, nc = CONV_B - 1, CONV_C - 1

    @pl.when(j == 0)
    def _():
        extb[EXT_B0 - nb:EXT_B0, :] = bufb_ref[...]
        extc[EXT_C0 - nc:EXT_C0, :] = bufc_ref[...]
        hcar[...] = h0_ref[...]

    glu = glu_ref[...]
    extb[EXT_B0:EXT_B0 + tt, :] = glu[:, 0:W_B] * _sigmoid(glu[:, W_B:2 * W_B])
    for r in range(8):
        n_r = tt + (CONV_B - 1 - r) // 8 * 8
        shb[r, 0:n_r, :] = extb[EXT_B0 - nb + r:EXT_B0 - nb + r + n_r, :]
    step = min(CONV_ROWS, tt)
    for r0 in range(0, tt, step):
        acc = jnp.zeros((step, W_B), F32)
        for k in range(CONV_B):
            lo = (k // 8) * 8 + r0
            acc = acc + cbw_ref[k:k + 1, :] * shb[k % 8, lo:lo + step, :]
        cbuf[r0:r0 + step, :] = acc + cbb_ref[...]
    cv = cbuf[...]
    gones = gones_ref[...]
    mu = _dot_exact_rhs(cv, gones) * (1.0 / GN_GROUP)
    d = cv - mu
    var = _dot_exact_rhs(d * d, gones) * (1.0 / GN_GROUP)
    cn = d * lax.rsqrt(var + EPS) * gng_ref[...] + gnb_ref[...]
    bo_ref[...] = _dot(_silu(cn).astype(BF16), wpw_ref[...]) * _silu(zb_ref[...])

    extc[EXT_C0:EXT_C0 + tt, :] = xc_ref[...]
    u = jnp.zeros((tt, W_C), F32)
    for k in range(CONV_C):
        lo = EXT_C0 - nc + k
        u = u + ccw_ref[k:k + 1, :] * extc[lo:lo + tt, :]
    u = u + ccb_ref[...]
    ub = u.astype(BF16)
    r = _sigmoid(_dot(ub, wa_ref[...]) + ba_ref[...])
    ig = _sigmoid(_dot(ub, wx_ref[...]) + bx_ref[...])
    nl = -lam_ref[...]
    softplus = jnp.maximum(nl, 0.0) + jnp.log1p(jnp.exp(-jnp.abs(nl)))
    log_a = -LRU_C * r * softplus
    a = jnp.exp(log_a)
    b = jnp.sqrt(-_expm1(2.0 * log_a)) * (ig * u)
    row = lax.broadcasted_iota(jnp.int32, (tt, W_C), 0)
    s = 1
    while s < tt:
        a_sh = jnp.where(row < s, 1.0, pltpu.roll(a, s, axis=0))
        b_sh = jnp.where(row < s, 0.0, pltpu.roll(b, s, axis=0))
        b = a * b_sh + b
        a = a * a_sh
        s *= 2
    hh = a * hcar[...] + b
    co_ref[...] = hh * _silu(zc_ref[...])
    hcar[...] = hh[tt - 1:tt, :]

    new_b = extb[EXT_B0 + tt - nb:EXT_B0 + tt, :]
    new_c = extc[EXT_C0 + tt - nc:EXT_C0 + tt, :]
    extb[EXT_B0 - nb:EXT_B0, :] = new_b
    extc[EXT_C0 - nc:EXT_C0, :] = new_c

    @pl.when(j == pl.num_programs(1) - 1)
    def _():
        cbs_ref[...] = new_b
        ccs_ref[...] = new_c
        hs_ref[...] = hh[tt - 1:tt, :]


def _mixers(proj, cols, bufb, bufc, h0, lw, gones, b, t, tt):
    nj = t // tt
    c_glu, c_zb, c_xc, c_zc = cols

    def row_spec(width, col):
        return pl.BlockSpec((tt, width), lambda bi, j: (bi * nj + j, col // width))

    def st_spec(n, w):
        return pl.BlockSpec((None, n, w), lambda bi, j: (bi, 0, 0))

    def const_spec(a):
        return pl.BlockSpec(a.shape, lambda bi, j: (0,) * a.ndim)

    consts = [lw["cbw"], lw["cbb"], lw["gng"], lw["gnb"], lw["wpw"], gones,
              lw["ccw"], lw["ccb"], lw["wa"], lw["ba"], lw["wx"], lw["bx"], lw["lam"]]
    in_specs = ([row_spec(2 * W_B, c_glu), row_spec(W_B, c_zb), row_spec(W_C, c_xc), row_spec(W_C, c_zc),
                 st_spec(CONV_B - 1, W_B), st_spec(CONV_C - 1, W_C), st_spec(1, W_C)]
                + [const_spec(a) for a in consts])
    out_specs = [pl.BlockSpec((tt, W_B), lambda bi, j: (bi * nj + j, 0)),
                 pl.BlockSpec((tt, W_C), lambda bi, j: (bi * nj + j, 0)),
                 st_spec(CONV_B - 1, W_B), st_spec(CONV_C - 1, W_C), st_spec(1, W_C)]
    out_shape = [jax.ShapeDtypeStruct((b * t, W_B), F32), jax.ShapeDtypeStruct((b * t, W_C), F32),
                 jax.ShapeDtypeStruct((b, CONV_B - 1, W_B), F32), jax.ShapeDtypeStruct((b, CONV_C - 1, W_C), F32),
                 jax.ShapeDtypeStruct((b, 1, W_C), F32)]
    scratch = [pltpu.VMEM((EXT_B0 + tt, W_B), F32), pltpu.VMEM((EXT_C0 + tt, W_C), F32),
               pltpu.VMEM((1, W_C), F32), pltpu.VMEM((tt, W_B), F32), pltpu.VMEM((8, tt + SH_PAD, W_B), F32)]
    return pl.pallas_call(
        functools.partial(_mixer_kernel, tt=tt),
        grid=(b, nj),
        in_specs=in_specs, out_specs=out_specs, out_shape=out_shape, scratch_shapes=scratch,
        compiler_params=_cparams(("arbitrary", "arbitrary")),
        name="mixers",
    )(proj, proj, proj, proj, bufb, bufc, h0, *consts)


def _block_diag(w):
    nblk, c, d = w.shape
    eye = jnp.eye(nblk, dtype=w.dtype)
    return (eye[:, None, :, None] * w[:, :, None, :]).reshape(nblk * c, nblk * d)


def _row_tile(m, cap, step=8):
    t = min(m, cap)
    while m % t:
        t -= step
    return t


def kernel(x_prompt, x_sample, cache_cmp_k, cache_cmp_v, cache_sel_k, cache_sel_v, cache_win_k, cache_win_v,
           state_conv_b, state_conv_c, state_rglru, page_table, rel_bias, g_pre, g_post, w_in, w_out,
           w_cmp_k, w_cmp_v, conv_b_w, conv_b_b, gn_gain, gn_bias, w_pw_b, conv_c_w, conv_c_b,
           w_lru_a, b_lru_a, w_lru_x, b_lru_x, lru_lambda):
    depth = w_in.shape[0]
    b, t, _ = x_prompt.shape
    db, dt, _ = x_sample.shape
    n_pages = page_table.shape[1]
    past = n_pages * PAGE
    nps = past // SEL_BLOCK
    ns, nt = t // SEL_BLOCK, t // TQ
    assert t % TQ == 0 and TQ == TK and WINDOW == 2 * TK and TOP_K <= ns <= 128 and TOP_K < nps <= 128
    assert dt <= 8 and (past + dt) // L_CMP == past // L_CMP and cache_win_k.shape[2] == WINDOW
    pages = min(32, n_pages)
    pages_c = min(16, n_pages)
    assert n_pages % pages == 0 and n_pages % pages_c == 0
    nc = n_pages // pages_c
    rows = KV_A * REP_A * dt

    w_it = jnp.swapaxes(w_in, 1, 2)
    w_r = jnp.concatenate(
        [w_it[:, 0:512], w_it[:, 1304:1816], w_it[:, 1816:2328], w_it[:, 512:1280], w_it[:, 2328:3096],
         w_it[:, 1280:1304], jnp.zeros((depth, D_IN_PAD - D_IN, D_MODEL), w_in.dtype)], axis=1).astype(BF16)
    w_n = w_it[:, 1816:3096].astype(BF16)
    w_t = jnp.concatenate(
        [w_it[:, 0:512], w_it[:, 1304:1816], w_it[:, 1280:1304],
         jnp.zeros((depth, N_TR - 1048, D_MODEL), w_in.dtype)], axis=1).astype(BF16)
    w_kv = w_it[:, 512:1280].astype(BF16)
    w_o = w_out.astype(BF16)
    wck = jnp.tile(w_cmp_k, (1, 1, KV_A))
    wvt_p = jnp.tile(jnp.swapaxes(w_cmp_v, 1, 2), (1, KV_A, t // L_CMP))
    wkt_d = jnp.tile(jnp.swapaxes(w_cmp_k, 1, 2), (1, KV_A, PAGE // L_CMP))
    wvt_d = jnp.tile(jnp.swapaxes(w_cmp_v, 1, 2), (1, KV_A, PAGE // L_CMP))
    gones = jnp.asarray(_group_ones_np(), BF16)
    e2 = jnp.asarray(_gate_expand_np(), BF16)
    e2t = jnp.asarray(_gate_expand_np().T, BF16)
    ett = jnp.asarray(_expand_np(ns, 128).T, BF16)
    e_d = jnp.asarray(_expand_np(nps, nps), BF16)
    seg_p = jnp.asarray(_segment_np(t, ns), BF16)
    seg_d = jnp.asarray(np.tile(_segment_np(pages_c * PAGE, pages_c * PAGE // SEL_BLOCK), (2, 1)), BF16)

    kk, qq = np.arange(TK)[:, None], np.arange(TQ)[None, :]
    bk_near = np.concatenate([_bucket_np(d0 + qq - kk) for d0 in (0, TQ)], axis=0)
    bnear = _bias_lookup(rel_bias, bk_near, shift=True, scale=LOG2E).reshape(H_A, 2, TK, TQ)
    bnear = jnp.where(jnp.asarray((kk > qq)[None, None] & (np.arange(2) == 0)[None, :, None, None]), NEG, bnear)
    blk_eo = np.concatenate([2 * np.arange(ns), 2 * np.arange(ns) + 1])[:, None]
    bct = _bias_lookup(rel_bias, _bucket_np(np.arange(t)[None, :] - (blk_eo * L_CMP + L_CMP - 1)))
    qd = past + np.arange(dt)[:, None]
    blk_d = np.concatenate([2 * np.arange(nps), 2 * np.arange(nps) + 1])[None, :]
    bcd = _bias_lookup(rel_bias, _bucket_np(qd - (blk_d * L_CMP + L_CMP - 1))).reshape(rows, 2 * nps)
    blast = _bias_lookup(rel_bias, _bucket_np(qd - (past - PAGE + np.arange(PAGE))[None, :]), shift=True, scale=LOG2E)
    bnew = _bias_lookup(rel_bias, _bucket_np(np.arange(dt)[:, None] - np.arange(128)[None, :]), shift=True, scale=LOG2E)
    bwin = _bias_lookup(rel_bias, _bucket_np(WINDOW + np.arange(dt)[:, None] - np.arange(WINDOW)[None, :]),
                        shift=True, scale=LOG2E)
    dtabs = (bcd, blast.reshape(rows, PAGE), bnew.reshape(rows, 128), bwin.reshape(rows, WINDOW))

    pool = lambda a: jnp.transpose(a, (0, 1, 3, 4, 2)).reshape(a.shape[0], a.shape[1], KV_COLS, PAGE)
    kc_ch, vc_ch = _decode_compress(page_table, pool(cache_cmp_k), pool(cache_cmp_v), wkt_d, wvt_d, seg_d, pages_c)

    def eo_table(x):
        x = x.reshape(depth, db, nc, KV_COLS, 2, nps // nc).transpose(0, 1, 3, 4, 2, 5)
        return x.reshape(depth, db, KV_COLS, 2 * nps)

    kct, vct = eo_table(kc_ch), eo_table(vc_ch)
    pool_sk, pool_sv = pool(cache_sel_k), pool(cache_sel_v)
    win_k = jnp.transpose(cache_win_k, (0, 1, 3, 4, 2)).reshape(depth, db, KV_COLS, WINDOW)
    win_v = jnp.transpose(cache_win_v, (0, 1, 3, 4, 2)).reshape(depth, db, KV_COLS, WINDOW)

    hp = x_prompt.reshape(b * t, D_MODEL)
    hs = x_sample.reshape(db * dt, D_MODEL)
    zeros_b = jnp.zeros((b, CONV_B - 1, W_B), F32)
    zeros_c = jnp.zeros((b, CONV_C - 1, W_C), F32)
    zeros_h = jnp.zeros((b, 1, W_C), F32)
    tm_p, tm_s = _row_tile(t, 512, 128), _row_tile(db * dt, 512)
    tt_p = _row_tile(t, 256)
    nat_cols = (NAT_GLU, NAT_ZB, NAT_XC, NAT_ZC)
    dec_cols = (COL_GLU, COL_ZB, COL_XC, COL_ZC)

    kv_states = [jnp.zeros((depth, b, KV_COLS, t), F32) for _ in range(N_KV)]
    p_states, s_states = [], []
    for l in range(depth):
        row = lambda a: a[l][None, :]
        lw = dict(cbw=conv_b_w[l], cbb=row(conv_b_b), gng=row(gn_gain), gnb=row(gn_bias), wpw=w_pw_b[l].astype(BF16),
                  ccw=conv_c_w[l], ccb=row(conv_c_b), wa=_block_diag(w_lru_a[l]).astype(BF16), ba=row(b_lru_a),
                  wx=_block_diag(w_lru_x[l]).astype(BF16), bx=row(b_lru_x), lam=row(lru_lambda))
        proj_n, proj_t, kv_states = _project_prompt(hp, row(g_pre), w_n[l], w_t[l], w_kv[l], kv_states, l, b, t, tm_p)
        a_out = _attn_prompt(l, proj_t, kv_states, wck[l], wvt_p[l], seg_p, bnear, bct, ett, e2t, b, t)
        b_out, c_out, cb, cc, hc = _mixers(proj_n, nat_cols, zeros_b, zeros_c, zeros_h, lw, gones, b, t, tt_p)
        hp = _output(hp, a_out, b_out, c_out, w_o[l], row(g_post), tm_p)
        p_states.append((cb, cc, hc[:, 0]))
        proj = _project(hs, row(g_pre), w_r[l], tm_s)
        a_out, wk_n, wv_n = _attn_decode(l, page_table, proj, kct, vct, pool_sk, pool_sv, win_k, win_v,
                                         dtabs, e_d, e2, pages, dt)
        b_out, c_out, cb, cc, hc = _mixers(proj, dec_cols, state_conv_b[l], state_conv_c[l],
                                           state_rglru[l][:, None, :], lw, gones, db, dt, dt)
        hs = _output(hs, a_out, b_out, c_out, w_o[l], row(g_post), tm_s)
        kv = proj[:, COL_KV:COL_KV + 6 * KV_COLS].reshape(db, dt, 6, KV_A, HEAD_DIM)
        s_states.append((kv[:, :, 0], kv[:, :, 1], kv[:, :, 2], kv[:, :, 3],
                         wk_n, wv_n, cb, cc, hc[:, 0]))

    cb_p, cc_p, h_p = [jnp.stack(a) for a in zip(*p_states)]
    kv_states = kv_states[:4] + [a[..., t - WINDOW:] for a in kv_states[4:]]
    ck_p, cv_p, sk_p, sv_p, wk_p, wv_p = [
        jnp.transpose(a.reshape(depth, b, KV_A, HEAD_DIM, a.shape[-1]), (0, 1, 4, 2, 3)) for a in kv_states]
    ck_s, cv_s, sk_s, sv_s, wk_s, wv_s, cb_s, cc_s, h_s = [jnp.stack(a) for a in zip(*s_states)]
    wk_s, wv_s = [jnp.transpose(a.reshape(depth, db, KV_A, HEAD_DIM, WINDOW), (0, 1, 4, 2, 3)) for a in (wk_s, wv_s)]
    return (hp.reshape(b, t, D_MODEL), hs.reshape(db, dt, D_MODEL),
            ck_p, ck_s, cv_p, cv_s, sk_p, sk_s, sv_p, sv_s, wk_p, wk_s, wv_p, wv_s,
            cb_p, cb_s, cc_p, cc_s, h_p, h_s)
```

```python
import functools
import math

import numpy as np
import jax
import jax.numpy as jnp
from jax import lax
from jax.experimental import pallas as pl
from jax.experimental.pallas import tpu as pltpu

F32 = jnp.float32
BF16 = jnp.bfloat16

D_MODEL = 1024
HEAD_DIM = 64
W_A = D_MODEL // 2
W_B = D_MODEL // 4
W_C = D_MODEL // 4
H_A = W_A // HEAD_DIM
KV_A = 2
REP_A = H_A // KV_A
KV_COLS = KV_A * HEAD_DIM
L_CMP = 32
SEL_BLOCK = 64
TOP_K = 16
FORCE_SCORE = 1.0e4
WINDOW = 512
PAGE = 128
CONV_B = 31
CONV_C = 4
GN_GROUP = W_B // 4
LRU_C = 8.0
NUM_BUCKETS = 32
MAX_DISTANCE = 128
SM_SCALE = HEAD_DIM ** -0.5
LOG2E = 1.4426950408889634
EPS = 1e-6
NEG = -1e30

TQ = 256
TK = 256
V_ROWS = HEAD_DIM + 16
D_IN = 3096
D_IN_PAD = 3200
COL_Q, COL_ZA, COL_GLU, COL_KV, COL_ZB, COL_XC, COL_ZC, COL_GATE = 0, 512, 1024, 1536, 2304, 2560, 2816, 3072
N_NAT = 1280
NAT_GLU, NAT_ZB, NAT_XC, NAT_ZC = 0, 512, 768, 1024
N_TR = 1152
TR_Q, TR_ZA, TR_GATE = 0, 512, 1024
N_KV = 6
VMEM_LIMIT = 56 * 1024 * 1024


def _cparams(sem):
    return pltpu.CompilerParams(dimension_semantics=sem, vmem_limit_bytes=VMEM_LIMIT)


def _bucket_np(dist):
    n = np.maximum(dist, 0)
    max_exact = NUM_BUCKETS // 2
    nf = np.maximum(n, 1).astype(np.float32)
    large = max_exact + (np.log(nf / np.float32(max_exact)) / np.float32(math.log(MAX_DISTANCE / max_exact))
                         * np.float32(NUM_BUCKETS - max_exact)).astype(np.int32)
    return np.where(n < max_exact, n, np.minimum(large, NUM_BUCKETS - 1)).astype(np.int32)


def _expand_np(n_blocks, rows):
    e = np.zeros((rows, n_blocks * SEL_BLOCK), np.float32)
    for m in range(n_blocks):
        e[m, m * SEL_BLOCK:(m + 1) * SEL_BLOCK] = 1.0
    return e


def _gate_expand_np():
    e = np.zeros((128, 3 * W_A), np.float32)
    for br in range(3):
        for h in range(H_A):
            e[br * H_A + h, br * W_A + h * HEAD_DIM: br * W_A + (h + 1) * HEAD_DIM] = 1.0
    return e


def _group_ones_np():
    g = np.zeros((W_B, W_B), np.float32)
    for k in range(W_B // GN_GROUP):
        g[k * GN_GROUP:(k + 1) * GN_GROUP, k * GN_GROUP:(k + 1) * GN_GROUP] = 1.0
    return g


def _segment_np(n_rows, n_half):
    s = np.zeros((n_rows, 2 * n_half), np.float32)
    blk = np.arange(n_rows) // L_CMP
    s[np.arange(n_rows), np.where(blk % 2 == 0, blk // 2, n_half + blk // 2)] = 1.0
    return s


def _split2(a):
    hi = a.astype(BF16)
    return hi, (a - hi.astype(F32)).astype(BF16)


def _split3(a):
    hi = a.astype(BF16)
    r1 = a - hi.astype(F32)
    mid = r1.astype(BF16)
    return hi, mid, (r1 - mid.astype(F32)).astype(BF16)


def _dot(a, b):
    return jnp.dot(a, b, preferred_element_type=F32)


def _dot_nt(a, b):
    return lax.dot_general(a, b, (((1,), (1,)), ((), ())), preferred_element_type=F32)


def _dot_exact_rhs(a, b_bf16):
    hi, mid, lo = _split3(a)
    return _dot(hi, b_bf16) + _dot(mid, b_bf16) + _dot(lo, b_bf16)


def _dot_exact_lhs(a_bf16, b):
    hi, mid, lo = _split3(b)
    return _dot(a_bf16, hi) + _dot(a_bf16, mid) + _dot(a_bf16, lo)


def _dot_f32(a, b):
    ah, al = _split2(a)
    bh, bl = _split2(b)
    return _dot(ah, bh) + _dot(al, bh) + _dot(ah, bl)


def _sigmoid(x):
    return 1.0 / (1.0 + jnp.exp(-x))


def _silu(x):
    return x * _sigmoid(x)


def _expm1(x):
    u = jnp.exp(x)
    safe = jnp.where((u == 1.0) | (u == 0.0), 0.5, u)
    return jnp.where(u == 1.0, x, jnp.where(u == 0.0, -1.0, (safe - 1.0) * x / jnp.log(safe)))


def _rank_select(score, idx, n, axis):
    rank = jnp.zeros(score.shape, F32)
    for m in range(n):
        sm = score[m:m + 1, :] if axis == 0 else score[:, m:m + 1]
        beats = (sm > score) | ((sm == score) & (idx > m))
        rank = rank + jnp.where(beats, 1.0, 0.0)
    return rank


def _bias_kernel(rb_ref, bk_ref, o_ref, *, shift, scale):
    h = pl.program_id(0)
    bk = bk_ref[...]
    acc = jnp.zeros(bk.shape, F32)
    for b in range(NUM_BUCKETS):
        acc = jnp.where(bk == b, rb_ref[b, h], acc)
    if shift:
        acc = acc - rb_ref[NUM_BUCKETS - 1, h]
    o_ref[...] = acc * scale


def _bias_lookup(rel_bias, bucket, shift=False, scale=1.0):
    r, c = bucket.shape
    return pl.pallas_call(
        functools.partial(_bias_kernel, shift=shift, scale=scale),
        grid=(H_A,),
        in_specs=[pl.BlockSpec(memory_space=pltpu.SMEM),
                  pl.BlockSpec((r, c), lambda h: (0, 0))],
        out_specs=pl.BlockSpec((None, r, c), lambda h: (h, 0, 0)),
        out_shape=jax.ShapeDtypeStruct((H_A, r, c), F32),
        compiler_params=_cparams(("arbitrary",)),
        name="bias_lookup",
    )(rel_bias, jnp.asarray(bucket))


def _rms_bf16(x_ref, g_ref):
    x = x_ref[...]
    ms = jnp.mean(x * x, axis=-1, keepdims=True)
    return (x * lax.rsqrt(ms + EPS) * g_ref[...]).astype(BF16)


def _proj_kernel(x_ref, g_ref, w_ref, o_ref):
    u = _rms_bf16(x_ref, g_ref)
    step = 640
    for c in range(0, o_ref.shape[1], step):
        o_ref[:, c:c + step] = _dot_nt(u, w_ref[c:c + step, :])


def _project(h, g, w, tm):
    m = h.shape[0]
    return pl.pallas_call(
        _proj_kernel,
        grid=(m // tm,),
        in_specs=[pl.BlockSpec((tm, D_MODEL), lambda i: (i, 0)),
                  pl.BlockSpec((1, D_MODEL), lambda i: (0, 0)),
                  pl.BlockSpec((D_IN_PAD, D_MODEL), lambda i: (0, 0))],
        out_specs=pl.BlockSpec((tm, D_IN_PAD), lambda i: (i, 0)),
        out_shape=jax.ShapeDtypeStruct((m, D_IN_PAD), F32),
        compiler_params=_cparams(("arbitrary",)),
        name="project",
    )(h, g, w)


def _proj2_kernel(x_ref, g_ref, wn_ref, wt_ref, wkv_ref, *refs):
    on_ref, ot_ref = refs[N_KV], refs[N_KV + 1]
    st_refs = refs[N_KV + 2:]
    u = _rms_bf16(x_ref, g_ref)
    for c in range(0, N_NAT, 640):
        on_ref[:, c:c + 640] = _dot_nt(u, wn_ref[c:c + 640, :])
    for c in range(0, N_TR, 384):
        ot_ref[c:c + 384, :] = _dot_nt(wt_ref[c:c + 384, :], u)
    for k in range(0, N_KV, 2):
        kv = _dot_nt(wkv_ref[k * KV_COLS:(k + 2) * KV_COLS, :], u)
        st_refs[k][...] = kv[0:KV_COLS]
        st_refs[k + 1][...] = kv[KV_COLS:2 * KV_COLS]


def _project_prompt(h, g, wn, wt, wkv, states, layer, b, t, tm):
    per = t // tm
    st_spec = pl.BlockSpec((None, None, KV_COLS, tm), lambda i: (layer, i // per, 0, i % per))
    outs = pl.pallas_call(
        _proj2_kernel,
        grid=(b * per,),
        in_specs=[pl.BlockSpec((tm, D_MODEL), lambda i: (i, 0)),
                  pl.BlockSpec((1, D_MODEL), lambda i: (0, 0)),
                  pl.BlockSpec((N_NAT, D_MODEL), lambda i: (0, 0)),
                  pl.BlockSpec((N_TR, D_MODEL), lambda i: (0, 0)),
                  pl.BlockSpec((N_KV * KV_COLS, D_MODEL), lambda i: (0, 0))]
                 + [pl.BlockSpec(memory_space=pl.ANY)] * N_KV,
        out_specs=[pl.BlockSpec((tm, N_NAT), lambda i: (i, 0)),
                   pl.BlockSpec((None, N_TR, tm), lambda i: (i // per, 0, i % per))] + [st_spec] * N_KV,
        out_shape=[jax.ShapeDtypeStruct((b * t, N_NAT), F32), jax.ShapeDtypeStruct((b, N_TR, t), F32)]
                  + [jax.ShapeDtypeStruct(a.shape, a.dtype) for a in states],
        input_output_aliases={5 + k: 2 + k for k in range(N_KV)},
        compiler_params=_cparams(("arbitrary",)),
        name="project_prompt",
    )(h, g, wn, wt, wkv, *states)
    return outs[0], outs[1], list(outs[2:])


def _out_kernel(h_ref, a_ref, b_ref, c_ref, w_ref, g_ref, o_ref):
    y = (_dot(a_ref[...].astype(BF16), w_ref[0:W_A, :])
         + _dot(b_ref[...].astype(BF16), w_ref[W_A:W_A + W_B, :])
         + _dot(c_ref[...].astype(BF16), w_ref[W_A + W_B:, :]))
    ms = jnp.mean(y * y, axis=-1, keepdims=True)
    o_ref[...] = h_ref[...] + y * lax.rsqrt(ms + EPS) * g_ref[...]


def _output(h, a, b, c, w, g, tm):
    m = h.shape[0]
    return pl.pallas_call(
        _out_kernel,
        grid=(m // tm,),
        in_specs=[pl.BlockSpec((tm, D_MODEL), lambda i: (i, 0)),
                  pl.BlockSpec((tm, W_A), lambda i: (i, 0)),
                  pl.BlockSpec((tm, W_B), lambda i: (i, 0)),
                  pl.BlockSpec((tm, W_C), lambda i: (i, 0)),
                  pl.BlockSpec((D_MODEL, D_MODEL), lambda i: (0, 0)),
                  pl.BlockSpec((1, D_MODEL), lambda i: (0, 0))],
        out_specs=pl.BlockSpec((tm, D_MODEL), lambda i: (i, 0)),
        out_shape=jax.ShapeDtypeStruct((m, D_MODEL), F32),
        compiler_params=_cparams(("arbitrary",)),
        name="output",
    )(h, a, b, c, w, g)


def _softmax_steps(scores, vts, states):
    stats = []
    for s, st in zip(scores, states):
        m_new = jnp.max(s, axis=0, keepdims=True)
        alpha = None
        if st is not None:
            m_new = jnp.maximum(st[0], m_new)
            alpha = jnp.exp2(st[0] - m_new)
        stats.append((m_new, alpha, jnp.exp2(s - m_new).astype(BF16)))
    out = []
    for (m_new, alpha, p), vt, st in zip(stats, vts, states):
        pv = _dot(vt, p)
        out.append((m_new, pv if st is None else alpha * st[1] + pv))
    return out


def _softmax_finish(states):
    return jnp.concatenate([acc[0:HEAD_DIM] / acc[HEAD_DIM:HEAD_DIM + 1] for (_, acc) in states], axis=0)


def _attn_prompt_kernel(qt_ref, zat_ref, gt_ref, kct_ref, vct_ref, kst_ref, vst_ref, kwt_ref, vwt_ref,
                        wk_ref, wvt_ref, seg_ref, bnear_ref, bct_ref, ett_ref, e2t_ref,
                        o_ref, kc_s, vct_s, ks_s, kw_s, vst_s, vwt_s, *, ns, nt):
    i = pl.program_id(1)
    nsp = -(-ns // 8) * 8

    @pl.when(i == 0)
    def _():
        x3 = kct_ref[...].T.reshape(ns, SEL_BLOCK, KV_COLS)
        w = wk_ref[...]
        kc_s[0:ns, :] = jnp.sum(x3[:, :L_CMP, :] * w[None], axis=1)
        kc_s[ns:2 * ns, :] = jnp.sum(x3[:, L_CMP:, :] * w[None], axis=1)
        hi, mid = _split2(vct_ref[...] * wvt_ref[...])
        seg = seg_ref[...]
        vct_s[...] = _dot(hi, seg) + _dot(mid, seg)
        ks_n, kw_n = kst_ref[...].T, kwt_ref[...].T
        for g in range(KV_A):
            ks_s[g] = jnp.concatenate([ks_n[:, g * HEAD_DIM:(g + 1) * HEAD_DIM].astype(BF16), ett_ref[...]], axis=1)
            kw_s[g] = jnp.concatenate([kw_n[:, g * HEAD_DIM:(g + 1) * HEAD_DIM].astype(BF16),
                                       jnp.zeros(ett_ref.shape, BF16)], axis=1)
        ones_row = jnp.where(lax.broadcasted_iota(jnp.int32, (V_ROWS - HEAD_DIM, TK), 0) == 0, 1.0, 0.0).astype(BF16)
        for c in range(nt):
            for g in range(KV_A):
                gs = slice(g * HEAD_DIM, (g + 1) * HEAD_DIM)
                vst_s[c, g] = jnp.concatenate([vst_ref[gs, c * TK:(c + 1) * TK].astype(BF16), ones_row], axis=0)
                vwt_s[c, g] = jnp.concatenate([vwt_ref[gs, c * TK:(c + 1) * TK].astype(BF16), ones_row], axis=0)

    qt = qt_ref[...]
    q0 = i * TQ

    row_c = lax.broadcasted_iota(jnp.int32, (2 * ns, TQ), 0)
    qpos_c = q0 + lax.broadcasted_iota(jnp.int32, (2 * ns, TQ), 1)
    blk_c = jnp.where(row_c < ns, 2 * row_c, 2 * (row_c - ns) + 1)
    mask_c = qpos_c >= blk_c * L_CMP + (L_CMP - 1)
    blk_t = lax.broadcasted_iota(jnp.int32, (nsp, TQ), 0)
    cur_t = (q0 + lax.broadcasted_iota(jnp.int32, (nsp, TQ), 1)) // SEL_BLOCK
    forced = (blk_t == 0) | ((blk_t <= cur_t) & (blk_t > cur_t - 2))
    kq_gap = (lax.broadcasted_iota(jnp.int32, (TK, TQ), 0) - lax.broadcasted_iota(jnp.int32, (TK, TQ), 1))
    sel_neg = []

    oc_parts = []
    for g in range(KV_A):
        kh, kl = _split2(kc_s[:, g * HEAD_DIM:(g + 1) * HEAD_DIM])
        vcg = vct_s[g * HEAD_DIM:(g + 1) * HEAD_DIM, :].astype(BF16)
        imp = jnp.zeros((2 * ns, TQ), F32)
        for r in range(REP_A):
            h = g * REP_A + r
            qh, ql = _split2(qt[h * HEAD_DIM:(h + 1) * HEAD_DIM, :])
            s = (_dot(kh, qh) + _dot(kl, qh) + _dot(kh, ql)) * SM_SCALE + bct_ref[h]
            s = jnp.where(mask_c, s, NEG)
            e = jnp.exp(s - jnp.max(s, axis=0, keepdims=True))
            p = e / jnp.sum(e, axis=0, keepdims=True)
            p = jnp.where(mask_c, p, 0.0)
            oc_parts.append(_dot(vcg, p.astype(BF16)))
            imp = imp + p
        simp = imp[0:ns, :] + imp[ns:2 * ns, :]
        if nsp > ns:
            simp = jnp.concatenate([simp, jnp.zeros((nsp - ns, TQ), F32)], axis=0)
        score = jnp.where(forced, FORCE_SCORE, simp)
        score = jnp.where(blk_t <= cur_t, score, NEG)
        rank = _rank_select(score, blk_t, ns, 0)
        sel = jnp.where((rank < TOP_K) & (blk_t <= cur_t), 1.0, 0.0)
        sel = jnp.concatenate([sel, jnp.zeros((128 - nsp, TQ), F32)], axis=0)
        sel_neg.append(((sel - 1.0) * (-NEG)).astype(BF16))
    oc_t = jnp.concatenate(oc_parts, axis=0)

    ok1, ok2 = i >= 1, i >= 2
    c1, c2 = jnp.maximum(i - 1, 0), jnp.maximum(i - 2, 0)
    n_far = jnp.maximum(i - 1, 0)
    group = [h // REP_A for h in range(H_A)]
    qs = [jnp.concatenate([(qt[h * HEAD_DIM:(h + 1) * HEAD_DIM, :] * (SM_SCALE * LOG2E)).astype(BF16),
                           sel_neg[group[h]]], axis=0) for h in range(H_A)]

    def kv_sel(g, c):
        r0 = pl.multiple_of(c * TK, TK)
        return ks_s[g, pl.ds(r0, TK), :], vst_s[c, g]

    def kv_win(g, c):
        r0 = pl.multiple_of(c * TK, TK)
        return kw_s[g, pl.ds(r0, TK), :], vwt_s[c, g]

    def far(c, carry):
        kv = [kv_sel(g, c) for g in range(KV_A)]
        scores = [_dot(kv[group[h]][0], qs[h]) for h in range(H_A)]
        return tuple(_softmax_steps(scores, [kv[group[h]][1] for h in range(H_A)], carry))

    init = tuple((jnp.full((1, TQ), NEG, F32), jnp.zeros((V_ROWS, TQ), F32)) for _ in range(H_A))
    sel_st = list(lax.fori_loop(0, n_far, far, init))
    win_st = [None] * H_A
    for c, ok, d_idx in ((c1, ok1, 1), (i, None, 0)):
        kvs = [kv_sel(g, c) for g in range(KV_A)]
        kvw = [kv_win(g, c) for g in range(KV_A)]
        kcat = [jnp.concatenate([kvs[g][0], kvw[g][0]], axis=0) for g in range(KV_A)]
        s2 = [_dot(kcat[group[h]], qs[h]) for h in range(H_A)]
        scores, vts, states = [], [], []
        for h in range(H_A):
            bias = bnear_ref[h, d_idx] if ok is None else jnp.where(ok, bnear_ref[h, d_idx], NEG)
            scores += [s2[h][0:TK] + bias, s2[h][TK:2 * TK] + bias]
            vts += [kvs[group[h]][1], kvw[group[h]][1]]
            states += [sel_st[h], win_st[h]]
        new = _softmax_steps(scores, vts, states)
        sel_st, win_st = new[0::2], new[1::2]
    kvw = [kv_win(g, c2) for g in range(KV_A)]
    far_add = jnp.where((kq_gap > 0) & ok2, 0.0, NEG)
    win_st = _softmax_steps([_dot(kvw[group[h]][0], qs[h]) + far_add for h in range(H_A)],
                            [kvw[group[h]][1] for h in range(H_A)], win_st)
    os_t = _softmax_finish(sel_st)
    ow_t = _softmax_finish(win_st)

    ge = _dot_exact_lhs(e2t_ref[...], _sigmoid(gt_ref[...]))
    o_t = ge[0:W_A] * oc_t + ge[W_A:2 * W_A] * os_t + ge[2 * W_A:3 * W_A] * ow_t
    o_ref[...] = (o_t * _silu(zat_ref[...])).T


def _attn_prompt(layer, proj_t, states, wk, wvt, seg, bnear, bct, ett, e2t, b, t):
    ns, nt = t // SEL_BLOCK, t // TQ
    st_spec = pl.BlockSpec((None, None, KV_COLS, t), lambda bi, i: (layer, bi, 0, 0))

    def tr_spec(rows, row0, width):
        return pl.BlockSpec((None, rows, width), lambda bi, i: (bi, row0 // rows, i))

    def const_spec(a):
        return pl.BlockSpec(a.shape, lambda bi, i: (0,) * a.ndim)

    in_specs = [tr_spec(W_A, TR_Q, TQ), tr_spec(W_A, TR_ZA, TQ), tr_spec(128, TR_GATE, TQ),
                st_spec, st_spec, st_spec, st_spec, st_spec, st_spec,
                const_spec(wk), const_spec(wvt), const_spec(seg), const_spec(bnear),
                pl.BlockSpec((H_A, 2 * ns, TQ), lambda bi, i: (0, 0, i)),
                const_spec(ett), const_spec(e2t)]
    scratch = [pltpu.VMEM((2 * ns, KV_COLS), F32), pltpu.VMEM((KV_COLS, 2 * ns), F32),
               pltpu.VMEM((KV_A, t, HEAD_DIM + 128), BF16), pltpu.VMEM((KV_A, t, HEAD_DIM + 128), BF16),
               pltpu.VMEM((nt, KV_A, V_ROWS, TK), BF16), pltpu.VMEM((nt, KV_A, V_ROWS, TK), BF16)]
    return pl.pallas_call(
        functools.partial(_attn_prompt_kernel, ns=ns, nt=nt),
        grid=(b, nt),
        in_specs=in_specs,
        out_specs=pl.BlockSpec((TQ, W_A), lambda bi, i: (bi * nt + i, 0)),
        out_shape=jax.ShapeDtypeStruct((b * t, W_A), F32),
        scratch_shapes=scratch,
        compiler_params=_cparams(("arbitrary", "arbitrary")),
        name="attn_prompt",
    )(proj_t, proj_t, proj_t, *states, wk, wvt, seg, bnear, bct, ett, e2t)


def _dcmp_kernel(pt_ref, *refs, pages):
    del pt_ref
    k_refs, v_refs = refs[0:pages], refs[pages:2 * pages]
    wkt_ref, wvt_ref, seg_ref = refs[2 * pages:2 * pages + 3]
    ko_ref, vo_ref = refs[2 * pages + 3:]
    for src, w_ref, o_ref in ((k_refs, wkt_ref, ko_ref), (v_refs, wvt_ref, vo_ref)):
        for l in range(o_ref.shape[0]):
            w = w_ref[l]
            parts = [_split2(src[p][l] * w) for p in range(pages)]
            x = jnp.concatenate([hi for hi, _ in parts] + [lo for _, lo in parts], axis=1)
            o_ref[l] = _dot(x, seg_ref[...])


def _decode_compress(page_table, pool_k, pool_v, wkt, wvt, seg, pages):
    depth, db, n_pages = pool_k.shape[0], page_table.shape[0], page_table.shape[1]
    nc = n_pages // pages
    ncol = seg.shape[1]

    def page_spec(p):
        return pl.BlockSpec((depth, None, KV_COLS, PAGE), lambda b, c, pt: (0, pt[b, c * pages + p], 0, 0))

    w_spec = pl.BlockSpec((depth, KV_COLS, PAGE), lambda b, c, pt: (0, 0, 0))
    s_spec = pl.BlockSpec(seg.shape, lambda b, c, pt: (0, 0))
    o_spec = pl.BlockSpec((depth, None, None, KV_COLS, ncol), lambda b, c, pt: (0, b, c, 0, 0))
    o_shape = jax.ShapeDtypeStruct((depth, db, nc, KV_COLS, ncol), F32)
    return pl.pallas_call(
        functools.partial(_dcmp_kernel, pages=pages),
        grid_spec=pltpu.PrefetchScalarGridSpec(
            num_scalar_prefetch=1,
            grid=(db, nc),
            in_specs=[page_spec(p) for p in range(pages)] * 2 + [w_spec, w_spec, s_spec],
            out_specs=[o_spec] * 2),
        out_shape=[o_shape] * 2,
        compiler_params=_cparams(("arbitrary", "arbitrary")),
        name="decode_compress",
    )(page_table, *([pool_k] * pages), *([pool_v] * pages), wkt, wvt, seg)


def _softmax_rows(qbd, kt, add, m, l):
    s = _dot(qbd, kt) + add
    m_new = jnp.maximum(m, jnp.max(s, axis=-1, keepdims=True))
    alpha = jnp.exp2(m - m_new)
    p = jnp.exp2(s - m_new)
    return m_new, alpha, alpha * l + jnp.sum(p, axis=-1, keepdims=True), p.astype(BF16)


def _attn_decode_kernel(pt_ref, *refs, pages, nps, nc, dt):
    del pt_ref
    (q_ref, za_ref, kv_ref, gate_ref, kct_ref, vct_ref) = refs[0:6]
    sk_refs, sv_refs = refs[6:6 + pages], refs[6 + pages:6 + 2 * pages]
    (wink_ref, winv_ref, bcd_ref, blast_ref, bnew_ref, bwin_ref, e_ref, e2_ref,
     o_ref, wk_o, wv_o, qbd_s, m_s, l_s, acc_s, madd_s, oc_s) = refs[6 + 2 * pages:]
    c = pl.program_id(1)
    ck = pages * PAGE
    rows = KV_A * REP_A * dt
    zpad = jnp.zeros((dt, HEAD_DIM), F32)

    @pl.when(c == 0)
    def _():
        q = q_ref[...]
        blocks = []
        for g in range(KV_A):
            for r in range(REP_A):
                h = g * REP_A + r
                piece = q[:, h * HEAD_DIM:(h + 1) * HEAD_DIM]
                blocks.append(jnp.concatenate([piece, zpad] if g == 0 else [zpad, piece], axis=1))
        qbd = jnp.concatenate(blocks, axis=0)
        qbd_s[...] = qbd
        s = _dot_f32(qbd, kct_ref[...]) * SM_SCALE + bcd_ref[...]
        e = jnp.exp(s - jnp.max(s, axis=-1, keepdims=True))
        p = e / jnp.sum(e, axis=-1, keepdims=True)
        oc_s[...] = _dot_nt(p.astype(BF16), vct_ref[...].astype(BF16))
        scores = []
        for g in range(KV_A):
            pe = [p[(g * REP_A + r) * dt:(g * REP_A + r + 1) * dt, 0:nps] for r in range(REP_A)]
            po = [p[(g * REP_A + r) * dt:(g * REP_A + r + 1) * dt, nps:2 * nps] for r in range(REP_A)]
            scores.append((pe[0] + pe[1] + pe[2] + pe[3]) + (po[0] + po[1] + po[2] + po[3]))
        simp = jnp.concatenate(scores, axis=0)
        blk = lax.broadcasted_iota(jnp.int32, simp.shape, 1)
        score = jnp.where((blk == 0) | (blk >= nps - 1), FORCE_SCORE, simp)
        rank = _rank_select(score, blk, nps, 1)
        sel = jnp.where(rank < TOP_K - 1, 1.0, 0.0).astype(BF16)
        madd = (_dot(sel, e_ref[...]) - 1.0) * (-NEG)
        madd = jnp.concatenate([madd[0:dt]] * REP_A + [madd[dt:2 * dt]] * REP_A, axis=0)
        for j in range(nc):
            tile = madd[:, j * ck:(j + 1) * ck]
            if j == nc - 1:
                tile = jnp.concatenate([tile[:, :ck - PAGE], tile[:, ck - PAGE:] + blast_ref[...]], axis=1)
            madd_s[j] = tile
        m_s[...] = jnp.full(m_s.shape, NEG, F32)
        l_s[...] = jnp.zeros(l_s.shape, F32)
        acc_s[...] = jnp.zeros(acc_s.shape, F32)

    qbd = (qbd_s[...] * (SM_SCALE * LOG2E)).astype(BF16)
    kt = jnp.concatenate([r[...].astype(BF16) for r in sk_refs], axis=1)
    vt = jnp.concatenate([r[...].astype(BF16) for r in sv_refs], axis=1)
    m, alpha, l, p = _softmax_rows(qbd, kt, madd_s[c], m_s[...], l_s[...])
    acc = alpha * acc_s[...] + _dot_nt(p, vt)
    m_s[...], l_s[...], acc_s[...] = m, l, acc

    @pl.when(c == nc - 1)
    def _():
        row_t = lax.broadcasted_iota(jnp.int32, (rows, 128), 0) % dt
        col = lax.broadcasted_iota(jnp.int32, (rows, 128), 1)
        new_add = jnp.where((col <= row_t) & (col < dt), bnew_ref[...], NEG)
        row_w = lax.broadcasted_iota(jnp.int32, (rows, WINDOW), 0) % dt
        col_w = lax.broadcasted_iota(jnp.int32, (rows, WINDOW), 1)
        win_add = jnp.where(col_w > row_w, bwin_ref[...], NEG)
        pad = jnp.zeros((128 - dt, KV_COLS), F32)

        def new_tile(src):
            return jnp.concatenate([kv_ref[:, src * KV_COLS:(src + 1) * KV_COLS], pad], axis=0).astype(BF16)

        def attend_nat(kn, vn, add, m, l, acc):
            s = _dot_nt(qbd, kn) + add
            m_new = jnp.maximum(m, jnp.max(s, axis=-1, keepdims=True))
            alpha = jnp.exp2(m - m_new)
            p = jnp.exp2(s - m_new)
            return m_new, alpha * l + jnp.sum(p, axis=-1, keepdims=True), alpha * acc + _dot(p.astype(BF16), vn)

        m2, l2, acc2 = attend_nat(new_tile(2), new_tile(3), new_add, m, l, acc)
        os_full = acc2 / l2
        init = (jnp.full((rows, 1), NEG, F32), jnp.zeros((rows, 1), F32), jnp.zeros((rows, KV_COLS), F32))
        m_w, _, l_w, p_w = _softmax_rows(qbd, wink_ref[...].astype(BF16), win_add, init[0], init[1])
        carry = (m_w, l_w, _dot_nt(p_w, winv_ref[...].astype(BF16)))
        m3, l3, acc3 = attend_nat(new_tile(4), new_tile(5), new_add, *carry)
        ow_full = acc3 / l3

        def heads(full):
            parts = []
            for g in range(KV_A):
                for r in range(REP_A):
                    r0 = (g * REP_A + r) * dt
                    parts.append(full[r0:r0 + dt, g * HEAD_DIM:(g + 1) * HEAD_DIM])
            return jnp.concatenate(parts, axis=1)

        sg = _sigmoid(gate_ref[...])
        ge = _dot_exact_rhs(sg, e2_ref[...])
        o = ge[:, 0:W_A] * heads(oc_s[...]) + ge[:, W_A:2 * W_A] * heads(os_full) + ge[:, 2 * W_A:] * heads(ow_full)
        o_ref[...] = o * _silu(za_ref[...])
        for src, w_ref, wo_ref in ((4, wink_ref, wk_o), (5, winv_ref, wv_o)):
            new_t = jnp.concatenate([kv_ref[:, src * KV_COLS:(src + 1) * KV_COLS], pad], axis=0).T
            wo_ref[...] = jnp.concatenate([w_ref[:, dt:], new_t[:, 0:dt]], axis=1)


def _attn_decode(layer, page_table, proj, kct, vct, pool_sk, pool_sv, win_k, win_v, tabs, e_sel, e2, pages, dt):
    db, n_pages = page_table.shape
    nps = n_pages * (PAGE // SEL_BLOCK)
    nc = n_pages // pages
    ck = pages * PAGE
    rows = KV_A * REP_A * dt
    bcd, blast, bnew, bwin = tabs

    def row_spec(width, col):
        return pl.BlockSpec((dt, width), lambda b, c, pt: (b, col // width))

    def page_spec(p):
        return pl.BlockSpec((None, None, KV_COLS, PAGE), lambda b, c, pt: (layer, pt[b, c * pages + p], 0, 0))

    cmp_spec = pl.BlockSpec((None, None, KV_COLS, 2 * nps), lambda b, c, pt: (layer, b, 0, 0))
    win_spec = pl.BlockSpec((None, None, KV_COLS, WINDOW), lambda b, c, pt: (layer, b, 0, 0))

    def const_spec(a):
        return pl.BlockSpec(a.shape, lambda b, c, pt: (0,) * a.ndim)

    in_specs = ([row_spec(W_A, COL_Q), row_spec(W_A, COL_ZA), row_spec(6 * KV_COLS, COL_KV), row_spec(128, COL_GATE)]
                + [cmp_spec] * 2 + [page_spec(p) for p in range(pages)] * 2 + [win_spec] * 2
                + [const_spec(a) for a in (bcd, blast, bnew, bwin, e_sel, e2)])
    out_specs = [pl.BlockSpec((dt, W_A), lambda b, c, pt: (b, 0)),
                 pl.BlockSpec((None, KV_COLS, WINDOW), lambda b, c, pt: (b, 0, 0)),
                 pl.BlockSpec((None, KV_COLS, WINDOW), lambda b, c, pt: (b, 0, 0))]
    out_shape = [jax.ShapeDtypeStruct((db * dt, W_A), F32),
                 jax.ShapeDtypeStruct((db, KV_COLS, WINDOW), F32),
                 jax.ShapeDtypeStruct((db, KV_COLS, WINDOW), F32)]
    scratch = [pltpu.VMEM((rows, KV_COLS), F32), pltpu.VMEM((rows, 1), F32), pltpu.VMEM((rows, 1), F32),
               pltpu.VMEM((rows, KV_COLS), F32), pltpu.VMEM((nc, rows, ck), F32), pltpu.VMEM((rows, KV_COLS), F32)]
    return pl.pallas_call(
        functools.partial(_attn_decode_kernel, pages=pages, nps=nps, nc=nc, dt=dt),
        grid_spec=pltpu.PrefetchScalarGridSpec(
            num_scalar_prefetch=1, grid=(db, nc), in_specs=in_specs, out_specs=out_specs, scratch_shapes=scratch),
        out_shape=out_shape,
        compiler_params=_cparams(("arbitrary", "arbitrary")),
        name="attn_decode",
    )(page_table, proj, proj, proj, proj, kct, vct, *([pool_sk] * pages), *([pool_sv] * pages),
      win_k, win_v, bcd, blast, bnew, bwin, e_sel, e2)


EXT_B0 = 32
EXT_C0 = 8
CONV_ROWS = 32
SH_PAD = (CONV_B - 1) // 8 * 8


def _mixer_kernel(glu_ref, zb_ref, xc_ref, zc_ref, bufb_ref, bufc_ref, h0_ref,
                  cbw_ref, cbb_ref, gng_ref, gnb_ref, wpw_ref, gones_ref,
                  ccw_ref, ccb_ref, wa_ref, ba_ref, wx_ref, bx_ref, lam_ref,
                  bo_ref, co_ref, cbs_ref, ccs_ref, hs_ref,
                  extb, extc, hcar, cbuf, shb, *, tt):
    j = pl.program_id(1)
    nb, nc = CONV_B - 1, CONV_C - 1

    @pl.when(j == 0)
    def _():
        extb[EXT_B0 - nb:EXT_B0, :] = bufb_ref[...]
        extc[EXT_C0 - nc:EXT_C0, :] = bufc_ref[...]
        hcar[...] = h0_ref[...]

    glu = glu_ref[...]
    extb[EXT_B0:EXT_B0 + tt, :] = glu[:, 0:W_B] * _sigmoid(glu[:, W_B:2 * W_B])
    for r in range(8):
        n_r = tt + (CONV_B - 1 - r) // 8 * 8
        shb[r, 0:n_r, :] = extb[EXT_B0 - nb + r:EXT_B0 - nb + r + n_r, :]
    step = min(CONV_ROWS, tt)
    for r0 in range(0, tt, step):
        acc = jnp.zeros((step, W_B), F32)
        for k in range(CONV_B):
            lo = (k // 8) * 8 + r0
            acc = acc + cbw_ref[k:k + 1, :] * shb[k % 8, lo:lo + step, :]
        cbuf[r0:r0 + step, :] = acc + cbb_ref[...]
    cv = cbuf[...]
    gones = gones_ref[...]
    mu = _dot_exact_rhs(cv, gones) * (1.0 / GN_GROUP)
    d = cv - mu
    var = _dot_exact_rhs(d * d, gones) * (1.0 / GN_GROUP)
    cn = d * lax.rsqrt(var + EPS) * gng_ref[...] + gnb_ref[...]
    bo_ref[...] = _dot(_silu(cn).astype(BF16), wpw_ref[...]) * _silu(zb_ref[...])

    extc[EXT_C0:EXT_C0 + tt, :] = xc_ref[...]
    u = jnp.zeros((tt, W_C), F32)
    for k in range(CONV_C):
        lo = EXT_C0 - nc + k
        u = u + ccw_ref[k:k + 1, :] * extc[lo:lo + tt, :]
    u = u + ccb_ref[...]
    ub = u.astype(BF16)
    r = _sigmoid(_dot(ub, wa_ref[...]) + ba_ref[...])
    ig = _sigmoid(_dot(ub, wx_ref[...]) + bx_ref[...])
    nl = -lam_ref[...]
    softplus = jnp.maximum(nl, 0.0) + jnp.log1p(jnp.exp(-jnp.abs(nl)))
    log_a = -LRU_C * r * softplus
    a = jnp.exp(log_a)
    b = jnp.sqrt(-_expm1(2.0 * log_a)) * (ig * u)
    row = lax.broadcasted_iota(jnp.int32, (tt, W_C), 0)
    s = 1
    while s < tt:
        a_sh = jnp.where(row < s, 1.0, pltpu.roll(a, s, axis=0))
        b_sh = jnp.where(row < s, 0.0, pltpu.roll(b, s, axis=0))
        b = a * b_sh + b
        a = a * a_sh
        s *= 2
    hh = a * hcar[...] + b
    co_ref[...] = hh * _silu(zc_ref[...])
    hcar[...] = hh[tt - 1:tt, :]

    new_b = extb[EXT_B0 + tt - nb:EXT_B0 + tt, :]
    new_c = extc[EXT_C0 + tt - nc:EXT_C0 + tt, :]
    extb[EXT_B0 - nb:EXT_B0, :] = new_b
    extc[EXT_C0 - nc:EXT_C0, :] = new_c

    @pl.when(j == pl.num_programs(1) - 1)
    def _():
        cbs_ref[...] = new_b
        ccs_ref[...] = new_c
        hs_ref[...] = hh[tt - 1:tt, :]


def _mixers(proj, cols, bufb, bufc, h0, lw, gones, b, t, tt):
    nj = t // tt
    c_glu, c_zb, c_xc, c_zc = cols

    def row_spec(width, col):
        return pl.BlockSpec((tt, width), lambda bi, j: (bi * nj + j, col // width))

    def st_spec(n, w):
        return pl.BlockSpec((None, n, w), lambda bi, j: (bi, 0, 0))

    def const_spec(a):
        return pl.BlockSpec(a.shape, lambda bi, j: (0,) * a.ndim)

    consts = [lw["cbw"], lw["cbb"], lw["gng"], lw["gnb"], lw["wpw"], gones,
              lw["ccw"], lw["ccb"], lw["wa"], lw["ba"], lw["wx"], lw["bx"], lw["lam"]]
    in_specs = ([row_spec(2 * W_B, c_glu), row_spec(W_B, c_zb), row_spec(W_C, c_xc), row_spec(W_C, c_zc),
                 st_spec(CONV_B - 1, W_B), st_spec(CONV_C - 1, W_C), st_spec(1, W_C)]
                + [const_spec(a) for a in consts])
    out_specs = [pl.BlockSpec((tt, W_B), lambda bi, j: (bi * nj + j, 0)),
                 pl.BlockSpec((tt, W_C), lambda bi, j: (bi * nj + j, 0)),
                 st_spec(CONV_B - 1, W_B), st_spec(CONV_C - 1, W_C), st_spec(1, W_C)]
    out_shape = [jax.ShapeDtypeStruct((b * t, W_B), F32), jax.ShapeDtypeStruct((b * t, W_C), F32),
                 jax.ShapeDtypeStruct((b, CONV_B - 1, W_B), F32), jax.ShapeDtypeStruct((b, CONV_C - 1, W_C), F32),
                 jax.ShapeDtypeStruct((b, 1, W_C), F32)]
    scratch = [pltpu.VMEM((EXT_B0 + tt, W_B), F32), pltpu.VMEM((EXT_C0 + tt, W_C), F32),
               pltpu.VMEM((1, W_C), F32), pltpu.VMEM((tt, W_B), F32), pltpu.VMEM((8, tt + SH_PAD, W_B), F32)]
    return pl.pallas_call(
        functools.partial(_mixer_kernel, tt=tt),
        grid=(b, nj),
        in_specs=in_specs, out_specs=out_specs, out_shape=out_shape, scratch_shapes=scratch,
        compiler_params=_cparams(("arbitrary", "arbitrary")),
        name="mixers",
    )(proj, proj, proj, proj, bufb, bufc, h0, *consts)


def _block_diag(w):
    nblk, c, d = w.shape
    eye = jnp.eye(nblk, dtype=w.dtype)
    return (eye[:, None, :, None] * w[:, :, None, :]).reshape(nblk * c, nblk * d)


def _row_tile(m, cap, step=8):
    t = min(m, cap)
    while m % t:
        t -= step
    return t


def kernel(x_prompt, x_sample, cache_cmp_k, cache_cmp_v, cache_sel_k, cache_sel_v, cache_win_k, cache_win_v,
           state_conv_b, state_conv_c, state_rglru, page_table, rel_bias, g_pre, g_post, w_in, w_out,
           w_cmp_k, w_cmp_v, conv_b_w, conv_b_b, gn_gain, gn_bias, w_pw_b, conv_c_w, conv_c_b,
           w_lru_a, b_lru_a, w_lru_x, b_lru_x, lru_lambda):
    depth = w_in.shape[0]
    b, t, _ = x_prompt.shape
    db, dt, _ = x_sample.shape
    n_pages = page_table.shape[1]
    past = n_pages * PAGE
    nps = past // SEL_BLOCK
    ns, nt = t // SEL_BLOCK, t // TQ
    assert t % TQ == 0 and TQ == TK and WINDOW == 2 * TK and TOP_K <= ns <= 128 and TOP_K < nps <= 128
    assert dt <= 8 and (past + dt) // L_CMP == past // L_CMP and cache_win_k.shape[2] == WINDOW
    pages = min(64, n_pages)
    pages_c = min(16, n_pages)
    assert n_pages % pages == 0 and n_pages % pages_c == 0
    nc = n_pages // pages_c
    rows = KV_A * REP_A * dt

    w_it = jnp.swapaxes(w_in, 1, 2)
    w_r = jnp.concatenate(
        [w_it[:, 0:512], w_it[:, 1304:1816], w_it[:, 1816:2328], w_it[:, 512:1280], w_it[:, 2328:3096],
         w_it[:, 1280:1304], jnp.zeros((depth, D_IN_PAD - D_IN, D_MODEL), w_in.dtype)], axis=1).astype(BF16)
    w_n = w_it[:, 1816:3096].astype(BF16)
    w_t = jnp.concatenate(
        [w_it[:, 0:512], w_it[:, 1304:1816], w_it[:, 1280:1304],
         jnp.zeros((depth, N_TR - 1048, D_MODEL), w_in.dtype)], axis=1).astype(BF16)
    w_kv = w_it[:, 512:1280].astype(BF16)
    w_o = w_out.astype(BF16)
    wck = jnp.tile(w_cmp_k, (1, 1, KV_A))
    wvt_p = jnp.tile(jnp.swapaxes(w_cmp_v, 1, 2), (1, KV_A, t // L_CMP))
    wkt_d = jnp.tile(jnp.swapaxes(w_cmp_k, 1, 2), (1, KV_A, PAGE // L_CMP))
    wvt_d = jnp.tile(jnp.swapaxes(w_cmp_v, 1, 2), (1, KV_A, PAGE // L_CMP))
    gones = jnp.asarray(_group_ones_np(), BF16)
    e2 = jnp.asarray(_gate_expand_np(), BF16)
    e2t = jnp.asarray(_gate_expand_np().T, BF16)
    ett = jnp.asarray(_expand_np(ns, 128).T, BF16)
    e_d = jnp.asarray(_expand_np(nps, nps), BF16)
    seg_p = jnp.asarray(_segment_np(t, ns), BF16)
    seg_d = jnp.asarray(np.tile(_segment_np(pages_c * PAGE, pages_c * PAGE // SEL_BLOCK), (2, 1)), BF16)

    kk, qq = np.arange(TK)[:, None], np.arange(TQ)[None, :]
    bk_near = np.concatenate([_bucket_np(d0 + qq - kk) for d0 in (0, TQ)], axis=0)
    bnear = _bias_lookup(rel_bias, bk_near, shift=True, scale=LOG2E).reshape(H_A, 2, TK, TQ)
    bnear = jnp.where(jnp.asarray((kk > qq)[None, None] & (np.arange(2) == 0)[None, :, None, None]), NEG, bnear)
    blk_eo = np.concatenate([2 * np.arange(ns), 2 * np.arange(ns) + 1])[:, None]
    bct = _bias_lookup(rel_bias, _bucket_np(np.arange(t)[None, :] - (blk_eo * L_CMP + L_CMP - 1)))
    qd = past + np.arange(dt)[:, None]
    blk_d = np.concatenate([2 * np.arange(nps), 2 * np.arange(nps) + 1])[None, :]
    bcd = _bias_lookup(rel_bias, _bucket_np(qd - (blk_d * L_CMP + L_CMP - 1))).reshape(rows, 2 * nps)
    blast = _bias_lookup(rel_bias, _bucket_np(qd - (past - PAGE + np.arange(PAGE))[None, :]), shift=True, scale=LOG2E)
    bnew = _bias_lookup(rel_bias, _bucket_np(np.arange(dt)[:, None] - np.arange(128)[None, :]), shift=True, scale=LOG2E)
    bwin = _bias_lookup(rel_bias, _bucket_np(WINDOW + np.arange(dt)[:, None] - np.arange(WINDOW)[None, :]),
                        shift=True, scale=LOG2E)
    dtabs = (bcd, blast.reshape(rows, PAGE), bnew.reshape(rows, 128), bwin.reshape(rows, WINDOW))

    pool = lambda a: jnp.transpose(a, (0, 1, 3, 4, 2)).reshape(a.shape[0], a.shape[1], KV_COLS, PAGE)
    kc_ch, vc_ch = _decode_compress(page_table, pool(cache_cmp_k), pool(cache_cmp_v), wkt_d, wvt_d, seg_d, pages_c)

    def eo_table(x):
        x = x.reshape(depth, db, nc, KV_COLS, 2, nps // nc).transpose(0, 1, 3, 4, 2, 5)
        return x.reshape(depth, db, KV_COLS, 2 * nps)

    kct, vct = eo_table(kc_ch), eo_table(vc_ch)
    pool_sk, pool_sv = pool(cache_sel_k), pool(cache_sel_v)
    win_k = jnp.transpose(cache_win_k, (0, 1, 3, 4, 2)).reshape(depth, db, KV_COLS, WINDOW)
    win_v = jnp.transpose(cache_win_v, (0, 1, 3, 4, 2)).reshape(depth, db, KV_COLS, WINDOW)

    hp = x_prompt.reshape(b * t, D_MODEL)
    hs = x_sample.reshape(db * dt, D_MODEL)
    zeros_b = jnp.zeros((b, CONV_B - 1, W_B), F32)
    zeros_c = jnp.zeros((b, CONV_C - 1, W_C), F32)
    zeros_h = jnp.zeros((b, 1, W_C), F32)
    tm_p, tm_s = _row_tile(t, 512, 128), _row_tile(db * dt, 512)
    tt_p = _row_tile(t, 256)
    nat_cols = (NAT_GLU, NAT_ZB, NAT_XC, NAT_ZC)
    dec_cols = (COL_GLU, COL_ZB, COL_XC, COL_ZC)

    kv_states = [jnp.zeros((depth, b, KV_COLS, t), F32) for _ in range(N_KV)]
    p_states, s_states = [], []
    for l in range(depth):
        row = lambda a: a[l][None, :]
        lw = dict(cbw=conv_b_w[l], cbb=row(conv_b_b), gng=row(gn_gain), gnb=row(gn_bias), wpw=w_pw_b[l].astype(BF16),
                  ccw=conv_c_w[l], ccb=row(conv_c_b), wa=_block_diag(w_lru_a[l]).astype(BF16), ba=row(b_lru_a),
                  wx=_block_diag(w_lru_x[l]).astype(BF16), bx=row(b_lru_x), lam=row(lru_lambda))
        proj_n, proj_t, kv_states = _project_prompt(hp, row(g_pre), w_n[l], w_t[l], w_kv[l], kv_states, l, b, t, tm_p)
        a_out = _attn_prompt(l, proj_t, kv_states, wck[l], wvt_p[l], seg_p, bnear, bct, ett, e2t, b, t)
        b_out, c_out, cb, cc, hc = _mixers(proj_n, nat_cols, zeros_b, zeros_c, zeros_h, lw, gones, b, t, tt_p)
        hp = _output(hp, a_out, b_out, c_out, w_o[l], row(g_post), tm_p)
        p_states.append((cb, cc, hc[:, 0]))
        proj = _project(hs, row(g_pre), w_r[l], tm_s)
        a_out, wk_n, wv_n = _attn_decode(l, page_table, proj, kct, vct, pool_sk, pool_sv, win_k, win_v,
                                         dtabs, e_d, e2, pages, dt)
        b_out, c_out, cb, cc, hc = _mixers(proj, dec_cols, state_conv_b[l], state_conv_c[l],
                                           state_rglru[l][:, None, :], lw, gones, db, dt, dt)
        hs = _output(hs, a_out, b_out, c_out, w_o[l], row(g_post), tm_s)
        kv = proj[:, COL_KV:COL_KV + 6 * KV_COLS].reshape(db, dt, 6, KV_A, HEAD_DIM)
        s_states.append((kv[:, :, 0], kv[:, :, 1], kv[:, :, 2], kv[:, :, 3],
                         wk_n, wv_n, cb, cc, hc[:, 0]))

    cb_p, cc_p, h_p = [jnp.stack(a) for a in zip(*p_states)]
    kv_states = kv_states[:4] + [a[..., t - WINDOW:] for a in kv_states[4:]]
    ck_p, cv_p, sk_p, sv_p, wk_p, wv_p = [
        jnp.transpose(a.reshape(depth, b, KV_A, HEAD_DIM, a.shape[-1]), (0, 1, 4, 2, 3)) for a in kv_states]
    ck_s, cv_s, sk_s, sv_s, wk_s, wv_s, cb_s, cc_s, h_s = [jnp.stack(a) for a in zip(*s_states)]
    wk_s, wv_s = [jnp.transpose(a.reshape(depth, db, KV_A, HEAD_DIM, WINDOW), (0, 1, 4, 2, 3)) for a in (wk_s, wv_s)]
    return (hp.reshape(b, t, D_MODEL), hs.reshape(db, dt, D_MODEL),
            ck_p, ck_s, cv_p, cv_s, sk_p, sk_s, sv_p, sv_s, wk_p, wk_s, wv_p, wv_s,
            cb_p, cb_s, cc_p, cc_s, h_p, h_s)
```

```python
import functools
import math

import numpy as np
import jax
import jax.numpy as jnp
from jax import lax
from jax.experimental import pallas as pl
from jax.experimental.pallas import tpu as pltpu

F32 = jnp.float32
BF16 = jnp.bfloat16

D_MODEL = 1024
HEAD_DIM = 64
W_A = D_MODEL // 2
W_B = D_MODEL // 4
W_C = D_MODEL // 4
H_A = W_A // HEAD_DIM
KV_A = 2
REP_A = H_A // KV_A
KV_COLS = KV_A * HEAD_DIM
L_CMP = 32
SEL_BLOCK = 64
TOP_K = 16
FORCE_SCORE = 1.0e4
WINDOW = 512
PAGE = 128
CONV_B = 31
CONV_C = 4
GN_GROUP = W_B // 4
LRU_C = 8.0
NUM_BUCKETS = 32
MAX_DISTANCE = 128
SM_SCALE = HEAD_DIM ** -0.5
LOG2E = 1.4426950408889634
EPS = 1e-6
NEG = -1e30

TQ = 256
TK = 256
V_ROWS = HEAD_DIM + 16
D_IN = 3096
D_IN_PAD = 3200
COL_Q, COL_ZA, COL_GLU, COL_KV, COL_ZB, COL_XC, COL_ZC, COL_GATE = 0, 512, 1024, 1536, 2304, 2560, 2816, 3072
N_NAT = 1280
NAT_GLU, NAT_ZB, NAT_XC, NAT_ZC = 0, 512, 768, 1024
N_TR = 1152
TR_Q, TR_ZA, TR_GATE = 0, 512, 1024
N_KV = 6
VMEM_LIMIT = 56 * 1024 * 1024


def _cparams(sem):
    return pltpu.CompilerParams(dimension_semantics=sem, vmem_limit_bytes=VMEM_LIMIT)


def _bucket_np(dist):
    n = np.maximum(dist, 0)
    max_exact = NUM_BUCKETS // 2
    nf = np.maximum(n, 1).astype(np.float32)
    large = max_exact + (np.log(nf / np.float32(max_exact)) / np.float32(math.log(MAX_DISTANCE / max_exact))
                         * np.float32(NUM_BUCKETS - max_exact)).astype(np.int32)
    return np.where(n < max_exact, n, np.minimum(large, NUM_BUCKETS - 1)).astype(np.int32)


def _expand_np(n_blocks, rows):
    e = np.zeros((rows, n_blocks * SEL_BLOCK), np.float32)
    for m in range(n_blocks):
        e[m, m * SEL_BLOCK:(m + 1) * SEL_BLOCK] = 1.0
    return e


def _gate_expand_np():
    e = np.zeros((128, 3 * W_A), np.float32)
    for br in range(3):
        for h in range(H_A):
            e[br * H_A + h, br * W_A + h * HEAD_DIM: br * W_A + (h + 1) * HEAD_DIM] = 1.0
    return e


def _group_ones_np():
    g = np.zeros((W_B, W_B), np.float32)
    for k in range(W_B // GN_GROUP):
        g[k * GN_GROUP:(k + 1) * GN_GROUP, k * GN_GROUP:(k + 1) * GN_GROUP] = 1.0
    return g


def _segment_np(n_rows, n_half):
    s = np.zeros((n_rows, 2 * n_half), np.float32)
    blk = np.arange(n_rows) // L_CMP
    s[np.arange(n_rows), np.where(blk % 2 == 0, blk // 2, n_half + blk // 2)] = 1.0
    return s


def _split2(a):
    hi = a.astype(BF16)
    return hi, (a - hi.astype(F32)).astype(BF16)


def _split3(a):
    hi = a.astype(BF16)
    r1 = a - hi.astype(F32)
    mid = r1.astype(BF16)
    return hi, mid, (r1 - mid.astype(F32)).astype(BF16)


def _dot(a, b):
    return jnp.dot(a, b, preferred_element_type=F32)


def _dot_nt(a, b):
    return lax.dot_general(a, b, (((1,), (1,)), ((), ())), preferred_element_type=F32)


def _dot_exact_rhs(a, b_bf16):
    hi, mid, lo = _split3(a)
    return _dot(hi, b_bf16) + _dot(mid, b_bf16) + _dot(lo, b_bf16)


def _dot_exact_lhs(a_bf16, b):
    hi, mid, lo = _split3(b)
    return _dot(a_bf16, hi) + _dot(a_bf16, mid) + _dot(a_bf16, lo)


def _dot_f32(a, b):
    ah, al = _split2(a)
    bh, bl = _split2(b)
    return _dot(ah, bh) + _dot(al, bh) + _dot(ah, bl)


def _sigmoid(x):
    return 1.0 / (1.0 + jnp.exp(-x))


def _silu(x):
    return x * _sigmoid(x)


def _expm1(x):
    u = jnp.exp(x)
    safe = jnp.where((u == 1.0) | (u == 0.0), 0.5, u)
    return jnp.where(u == 1.0, x, jnp.where(u == 0.0, -1.0, (safe - 1.0) * x / jnp.log(safe)))


def _rank_select(score, idx, n, axis):
    rank = jnp.zeros(score.shape, F32)
    for m in range(n):
        sm = score[m:m + 1, :] if axis == 0 else score[:, m:m + 1]
        beats = (sm > score) | ((sm == score) & (idx > m))
        rank = rank + jnp.where(beats, 1.0, 0.0)
    return rank


def _bias_kernel(rb_ref, bk_ref, o_ref, *, shift, scale):
    h = pl.program_id(0)
    bk = bk_ref[...]
    acc = jnp.zeros(bk.shape, F32)
    for b in range(NUM_BUCKETS):
        acc = jnp.where(bk == b, rb_ref[b, h], acc)
    if shift:
        acc = acc - rb_ref[NUM_BUCKETS - 1, h]
    o_ref[...] = acc * scale


def _bias_lookup(rel_bias, bucket, shift=False, scale=1.0):
    r, c = bucket.shape
    return pl.pallas_call(
        functools.partial(_bias_kernel, shift=shift, scale=scale),
        grid=(H_A,),
        in_specs=[pl.BlockSpec(memory_space=pltpu.SMEM),
                  pl.BlockSpec((r, c), lambda h: (0, 0))],
        out_specs=pl.BlockSpec((None, r, c), lambda h: (h, 0, 0)),
        out_shape=jax.ShapeDtypeStruct((H_A, r, c), F32),
        compiler_params=_cparams(("arbitrary",)),
        name="bias_lookup",
    )(rel_bias, jnp.asarray(bucket))


def _rms_bf16(x_ref, g_ref):
    x = x_ref[...]
    ms = jnp.mean(x * x, axis=-1, keepdims=True)
    return (x * lax.rsqrt(ms + EPS) * g_ref[...]).astype(BF16)


def _proj_kernel(x_ref, g_ref, w_ref, o_ref):
    u = _rms_bf16(x_ref, g_ref)
    step = 640
    for c in range(0, o_ref.shape[1], step):
        o_ref[:, c:c + step] = _dot_nt(u, w_ref[c:c + step, :])


def _project(h, g, w, tm):
    m = h.shape[0]
    return pl.pallas_call(
        _proj_kernel,
        grid=(m // tm,),
        in_specs=[pl.BlockSpec((tm, D_MODEL), lambda i: (i, 0)),
                  pl.BlockSpec((1, D_MODEL), lambda i: (0, 0)),
                  pl.BlockSpec((D_IN_PAD, D_MODEL), lambda i: (0, 0))],
        out_specs=pl.BlockSpec((tm, D_IN_PAD), lambda i: (i, 0)),
        out_shape=jax.ShapeDtypeStruct((m, D_IN_PAD), F32),
        compiler_params=_cparams(("arbitrary",)),
        name="project",
    )(h, g, w)


def _proj2_kernel(x_ref, g_ref, wn_ref, wt_ref, wkv_ref, *refs):
    on_ref, ot_ref = refs[N_KV], refs[N_KV + 1]
    st_refs = refs[N_KV + 2:]
    u = _rms_bf16(x_ref, g_ref)
    for c in range(0, N_NAT, 640):
        on_ref[:, c:c + 640] = _dot_nt(u, wn_ref[c:c + 640, :])
    for c in range(0, N_TR, 384):
        ot_ref[c:c + 384, :] = _dot_nt(wt_ref[c:c + 384, :], u)
    for k in range(0, N_KV, 2):
        kv = _dot_nt(wkv_ref[k * KV_COLS:(k + 2) * KV_COLS, :], u)
        st_refs[k][...] = kv[0:KV_COLS]
        st_refs[k + 1][...] = kv[KV_COLS:2 * KV_COLS]


def _project_prompt(h, g, wn, wt, wkv, states, layer, b, t, tm):
    per = t // tm
    st_spec = pl.BlockSpec((None, None, KV_COLS, tm), lambda i: (layer, i // per, 0, i % per))
    outs = pl.pallas_call(
        _proj2_kernel,
        grid=(b * per,),
        in_specs=[pl.BlockSpec((tm, D_MODEL), lambda i: (i, 0)),
                  pl.BlockSpec((1, D_MODEL), lambda i: (0, 0)),
                  pl.BlockSpec((N_NAT, D_MODEL), lambda i: (0, 0)),
                  pl.BlockSpec((N_TR, D_MODEL), lambda i: (0, 0)),
                  pl.BlockSpec((N_KV * KV_COLS, D_MODEL), lambda i: (0, 0))]
                 + [pl.BlockSpec(memory_space=pl.ANY)] * N_KV,
        out_specs=[pl.BlockSpec((tm, N_NAT), lambda i: (i, 0)),
                   pl.BlockSpec((None, N_TR, tm), lambda i: (i // per, 0, i % per))] + [st_spec] * N_KV,
        out_shape=[jax.ShapeDtypeStruct((b * t, N_NAT), F32), jax.ShapeDtypeStruct((b, N_TR, t), F32)]
                  + [jax.ShapeDtypeStruct(a.shape, a.dtype) for a in states],
        input_output_aliases={5 + k: 2 + k for k in range(N_KV)},
        compiler_params=_cparams(("arbitrary",)),
        name="project_prompt",
    )(h, g, wn, wt, wkv, *states)
    return outs[0], outs[1], list(outs[2:])


def _out_kernel(h_ref, a_ref, b_ref, c_ref, w_ref, g_ref, o_ref):
    y = (_dot(a_ref[...].astype(BF16), w_ref[0:W_A, :])
         + _dot(b_ref[...].astype(BF16), w_ref[W_A:W_A + W_B, :])
         + _dot(c_ref[...].astype(BF16), w_ref[W_A + W_B:, :]))
    ms = jnp.mean(y * y, axis=-1, keepdims=True)
    o_ref[...] = h_ref[...] + y * lax.rsqrt(ms + EPS) * g_ref[...]


def _output(h, a, b, c, w, g, tm):
    m = h.shape[0]
    return pl.pallas_call(
        _out_kernel,
        grid=(m // tm,),
        in_specs=[pl.BlockSpec((tm, D_MODEL), lambda i: (i, 0)),
                  pl.BlockSpec((tm, W_A), lambda i: (i, 0)),
                  pl.BlockSpec((tm, W_B), lambda i: (i, 0)),
                  pl.BlockSpec((tm, W_C), lambda i: (i, 0)),
                  pl.BlockSpec((D_MODEL, D_MODEL), lambda i: (0, 0)),
                  pl.BlockSpec((1, D_MODEL), lambda i: (0, 0))],
        out_specs=pl.BlockSpec((tm, D_MODEL), lambda i: (i, 0)),
        out_shape=jax.ShapeDtypeStruct((m, D_MODEL), F32),
        compiler_params=_cparams(("arbitrary",)),
        name="output",
    )(h, a, b, c, w, g)


def _softmax_steps(scores, vts, states):
    stats = []
    for s, st in zip(scores, states):
        m_new = jnp.max(s, axis=0, keepdims=True)
        alpha = None
        if st is not None:
            m_new = jnp.maximum(st[0], m_new)
            alpha = jnp.exp2(st[0] - m_new)
        stats.append((m_new, alpha, jnp.exp2(s - m_new).astype(BF16)))
    out = []
    for (m_new, alpha, p), vt, st in zip(stats, vts, states):
        pv = _dot(vt, p)
        out.append((m_new, pv if st is None else alpha * st[1] + pv))
    return out


def _softmax_finish(states):
    return jnp.concatenate([acc[0:HEAD_DIM] / acc[HEAD_DIM:HEAD_DIM + 1] for (_, acc) in states], axis=0)


def _attn_prompt_kernel(qt_ref, zat_ref, gt_ref, kct_ref, vct_ref, kst_ref, vst_ref, kwt_ref, vwt_ref,
                        wk_ref, wvt_ref, seg_ref, bnear_ref, bct_ref, ett_ref, e2t_ref,
                        o_ref, kc_s, vct_s, ks_s, kw_s, vst_s, vwt_s, *, ns, nt):
    i = pl.program_id(1)
    nsp = -(-ns // 8) * 8

    @pl.when(i == 0)
    def _():
        x3 = kct_ref[...].T.reshape(ns, SEL_BLOCK, KV_COLS)
        w = wk_ref[...]
        kc_s[0:ns, :] = jnp.sum(x3[:, :L_CMP, :] * w[None], axis=1)
        kc_s[ns:2 * ns, :] = jnp.sum(x3[:, L_CMP:, :] * w[None], axis=1)
        hi, mid = _split2(vct_ref[...] * wvt_ref[...])
        seg = seg_ref[...]
        vct_s[...] = _dot(hi, seg) + _dot(mid, seg)
        ks_n, kw_n = kst_ref[...].T, kwt_ref[...].T
        for g in range(KV_A):
            ks_s[g] = jnp.concatenate([ks_n[:, g * HEAD_DIM:(g + 1) * HEAD_DIM].astype(BF16), ett_ref[...]], axis=1)
            kw_s[g] = jnp.concatenate([kw_n[:, g * HEAD_DIM:(g + 1) * HEAD_DIM].astype(BF16),
                                       jnp.zeros(ett_ref.shape, BF16)], axis=1)
        ones_row = jnp.where(lax.broadcasted_iota(jnp.int32, (V_ROWS - HEAD_DIM, TK), 0) == 0, 1.0, 0.0).astype(BF16)
        for c in range(nt):
            for g in range(KV_A):
                gs = slice(g * HEAD_DIM, (g + 1) * HEAD_DIM)
                vst_s[c, g] = jnp.concatenate([vst_ref[gs, c * TK:(c + 1) * TK].astype(BF16), ones_row], axis=0)
                vwt_s[c, g] = jnp.concatenate([vwt_ref[gs, c * TK:(c + 1) * TK].astype(BF16), ones_row], axis=0)

    qt = qt_ref[...]
    q0 = i * TQ

    row_c = lax.broadcasted_iota(jnp.int32, (2 * ns, TQ), 0)
    qpos_c = q0 + lax.broadcasted_iota(jnp.int32, (2 * ns, TQ), 1)
    blk_c = jnp.where(row_c < ns, 2 * row_c, 2 * (row_c - ns) + 1)
    mask_c = qpos_c >= blk_c * L_CMP + (L_CMP - 1)
    blk_t = lax.broadcasted_iota(jnp.int32, (nsp, TQ), 0)
    cur_t = (q0 + lax.broadcasted_iota(jnp.int32, (nsp, TQ), 1)) // SEL_BLOCK
    forced = (blk_t == 0) | ((blk_t <= cur_t) & (blk_t > cur_t - 2))
    kq_gap = (lax.broadcasted_iota(jnp.int32, (TK, TQ), 0) - lax.broadcasted_iota(jnp.int32, (TK, TQ), 1))
    sel_neg = []

    oc_parts = []
    for g in range(KV_A):
        kh, kl = _split2(kc_s[:, g * HEAD_DIM:(g + 1) * HEAD_DIM])
        vcg = vct_s[g * HEAD_DIM:(g + 1) * HEAD_DIM, :].astype(BF16)
        imp = jnp.zeros((2 * ns, TQ), F32)
        for r in range(REP_A):
            h = g * REP_A + r
            qh, ql = _split2(qt[h * HEAD_DIM:(h + 1) * HEAD_DIM, :])
            s = (_dot(kh, qh) + _dot(kl, qh) + _dot(kh, ql)) * SM_SCALE + bct_ref[h]
            s = jnp.where(mask_c, s, NEG)
            e = jnp.exp(s - jnp.max(s, axis=0, keepdims=True))
            p = e / jnp.sum(e, axis=0, keepdims=True)
            p = jnp.where(mask_c, p, 0.0)
            oc_parts.append(_dot(vcg, p.astype(BF16)))
            imp = imp + p
        simp = imp[0:ns, :] + imp[ns:2 * ns, :]
        if nsp > ns:
            simp = jnp.concatenate([simp, jnp.zeros((nsp - ns, TQ), F32)], axis=0)
        score = jnp.where(forced, FORCE_SCORE, simp)
        score = jnp.where(blk_t <= cur_t, score, NEG)
        rank = _rank_select(score, blk_t, ns, 0)
        sel = jnp.where((rank < TOP_K) & (blk_t <= cur_t), 1.0, 0.0)
        sel = jnp.concatenate([sel, jnp.zeros((128 - nsp, TQ), F32)], axis=0)
        sel_neg.append(((sel - 1.0) * (-NEG)).astype(BF16))
    oc_t = jnp.concatenate(oc_parts, axis=0)

    ok1, ok2 = i >= 1, i >= 2
    c1, c2 = jnp.maximum(i - 1, 0), jnp.maximum(i - 2, 0)
    n_far = jnp.maximum(i - 1, 0)
    group = [h // REP_A for h in range(H_A)]
    qs = [jnp.concatenate([(qt[h * HEAD_DIM:(h + 1) * HEAD_DIM, :] * (SM_SCALE * LOG2E)).astype(BF16),
                           sel_neg[group[h]]], axis=0) for h in range(H_A)]

    def kv_sel(g, c):
        r0 = pl.multiple_of(c * TK, TK)
        return ks_s[g, pl.ds(r0, TK), :], vst_s[c, g]

    def kv_win(g, c):
        r0 = pl.multiple_of(c * TK, TK)
        return kw_s[g, pl.ds(r0, TK), :], vwt_s[c, g]

    def far(c, carry):
        kv = [kv_sel(g, c) for g in range(KV_A)]
        scores = [_dot(kv[group[h]][0], qs[h]) for h in range(H_A)]
        return tuple(_softmax_steps(scores, [kv[group[h]][1] for h in range(H_A)], carry))

    init = tuple((jnp.full((1, TQ), NEG, F32), jnp.zeros((V_ROWS, TQ), F32)) for _ in range(H_A))
    sel_st = list(lax.fori_loop(0, n_far, far, init))
    win_st = [None] * H_A
    for c, ok, d_idx in ((c1, ok1, 1), (i, None, 0)):
        kvs = [kv_sel(g, c) for g in range(KV_A)]
        kvw = [kv_win(g, c) for g in range(KV_A)]
        kcat = [jnp.concatenate([kvs[g][0], kvw[g][0]], axis=0) for g in range(KV_A)]
        s2 = [_dot(kcat[group[h]], qs[h]) for h in range(H_A)]
        scores, vts, states = [], [], []
        for h in range(H_A):
            bias = bnear_ref[h, d_idx] if ok is None else jnp.where(ok, bnear_ref[h, d_idx], NEG)
            scores += [s2[h][0:TK] + bias, s2[h][TK:2 * TK] + bias]
            vts += [kvs[group[h]][1], kvw[group[h]][1]]
            states += [sel_st[h], win_st[h]]
        new = _softmax_steps(scores, vts, states)
        sel_st, win_st = new[0::2], new[1::2]
    kvw = [kv_win(g, c2) for g in range(KV_A)]
    far_add = jnp.where((kq_gap > 0) & ok2, 0.0, NEG)
    win_st = _softmax_steps([_dot(kvw[group[h]][0], qs[h]) + far_add for h in range(H_A)],
                            [kvw[group[h]][1] for h in range(H_A)], win_st)
    os_t = _softmax_finish(sel_st)
    ow_t = _softmax_finish(win_st)

    ge = _dot_exact_lhs(e2t_ref[...], _sigmoid(gt_ref[...]))
    o_t = ge[0:W_A] * oc_t + ge[W_A:2 * W_A] * os_t + ge[2 * W_A:3 * W_A] * ow_t
    o_ref[...] = (o_t * _silu(zat_ref[...])).T


def _attn_prompt(layer, proj_t, states, wk, wvt, seg, bnear, bct, ett, e2t, b, t):
    ns, nt = t // SEL_BLOCK, t // TQ
    st_spec = pl.BlockSpec((None, None, KV_COLS, t), lambda bi, i: (layer, bi, 0, 0))

    def tr_spec(rows, row0, width):
        return pl.BlockSpec((None, rows, width), lambda bi, i: (bi, row0 // rows, i))

    def const_spec(a):
        return pl.BlockSpec(a.shape, lambda bi, i: (0,) * a.ndim)

    in_specs = [tr_spec(W_A, TR_Q, TQ), tr_spec(W_A, TR_ZA, TQ), tr_spec(128, TR_GATE, TQ),
                st_spec, st_spec, st_spec, st_spec, st_spec, st_spec,
                const_spec(wk), const_spec(wvt), const_spec(seg), const_spec(bnear),
                pl.BlockSpec((H_A, 2 * ns, TQ), lambda bi, i: (0, 0, i)),
                const_spec(ett), const_spec(e2t)]
    scratch = [pltpu.VMEM((2 * ns, KV_COLS), F32), pltpu.VMEM((KV_COLS, 2 * ns), F32),
               pltpu.VMEM((KV_A, t, HEAD_DIM + 128), BF16), pltpu.VMEM((KV_A, t, HEAD_DIM + 128), BF16),
               pltpu.VMEM((nt, KV_A, V_ROWS, TK), BF16), pltpu.VMEM((nt, KV_A, V_ROWS, TK), BF16)]
    return pl.pallas_call(
        functools.partial(_attn_prompt_kernel, ns=ns, nt=nt),
        grid=(b, nt),
        in_specs=in_specs,
        out_specs=pl.BlockSpec((TQ, W_A), lambda bi, i: (bi * nt + i, 0)),
        out_shape=jax.ShapeDtypeStruct((b * t, W_A), F32),
        scratch_shapes=scratch,
        compiler_params=_cparams(("arbitrary", "arbitrary")),
        name="attn_prompt",
    )(proj_t, proj_t, proj_t, *states, wk, wvt, seg, bnear, bct, ett, e2t)


def _dcmp_kernel(pt_ref, *refs, pages):
    del pt_ref
    k_refs, v_refs = refs[0:pages], refs[pages:2 * pages]
    wkt_ref, wvt_ref, seg_ref = refs[2 * pages:2 * pages + 3]
    ko_ref, vo_ref = refs[2 * pages + 3:]
    for src, w_ref, o_ref in ((k_refs, wkt_ref, ko_ref), (v_refs, wvt_ref, vo_ref)):
        for l in range(o_ref.shape[0]):
            w = w_ref[l]
            parts = [_split2(src[p][l] * w) for p in range(pages)]
            x = jnp.concatenate([hi for hi, _ in parts] + [lo for _, lo in parts], axis=1)
            o_ref[l] = _dot(x, seg_ref[...])


def _decode_compress(page_table, pool_k, pool_v, wkt, wvt, seg, pages):
    depth, db, n_pages = pool_k.shape[0], page_table.shape[0], page_table.shape[1]
    nc = n_pages // pages
    ncol = seg.shape[1]

    def page_spec(p):
        return pl.BlockSpec((depth, None, KV_COLS, PAGE), lambda b, c, pt: (0, pt[b, c * pages + p], 0, 0))

    w_spec = pl.BlockSpec((depth, KV_COLS, PAGE), lambda b, c, pt: (0, 0, 0))
    s_spec = pl.BlockSpec(seg.shape, lambda b, c, pt: (0, 0))
    o_spec = pl.BlockSpec((depth, None, None, KV_COLS, ncol), lambda b, c, pt: (0, b, c, 0, 0))
    o_shape = jax.ShapeDtypeStruct((depth, db, nc, KV_COLS, ncol), F32)
    return pl.pallas_call(
        functools.partial(_dcmp_kernel, pages=pages),
        grid_spec=pltpu.PrefetchScalarGridSpec(
            num_scalar_prefetch=1,
            grid=(db, nc),
            in_specs=[page_spec(p) for p in range(pages)] * 2 + [w_spec, w_spec, s_spec],
            out_specs=[o_spec] * 2),
        out_shape=[o_shape] * 2,
        compiler_params=_cparams(("arbitrary", "arbitrary")),
        name="decode_compress",
    )(page_table, *([pool_k] * pages), *([pool_v] * pages), wkt, wvt, seg)


def _softmax_rows(qbd, kt, add, m, l):
    s = _dot(qbd, kt) + add
    m_new = jnp.maximum(m, jnp.max(s, axis=-1, keepdims=True))
    alpha = jnp.exp2(m - m_new)
    p = jnp.exp2(s - m_new)
    return m_new, alpha, alpha * l + jnp.sum(p, axis=-1, keepdims=True), p.astype(BF16)


def _attn_decode_kernel(pt_ref, *refs, pages, nps, nc, dt):
    del pt_ref
    (q_ref, za_ref, kv_ref, gate_ref, kct_ref, vct_ref) = refs[0:6]
    sk_refs, sv_refs = refs[6:6 + pages], refs[6 + pages:6 + 2 * pages]
    (wink_ref, winv_ref, bcd_ref, blast_ref, bnew_ref, bwin_ref, e_ref, e2_ref,
     o_ref, wk_o, wv_o, qbd_s, m_s, l_s, acc_s, madd_s, oc_s) = refs[6 + 2 * pages:]
    c = pl.program_id(1)
    ck = pages * PAGE
    rows = KV_A * REP_A * dt
    zpad = jnp.zeros((dt, HEAD_DIM), F32)

    @pl.when(c == 0)
    def _():
        q = q_ref[...]
        blocks = []
        for g in range(KV_A):
            for r in range(REP_A):
                h = g * REP_A + r
                piece = q[:, h * HEAD_DIM:(h + 1) * HEAD_DIM]
                blocks.append(jnp.concatenate([piece, zpad] if g == 0 else [zpad, piece], axis=1))
        qbd = jnp.concatenate(blocks, axis=0)
        qbd_s[...] = qbd
        s = _dot_f32(qbd, kct_ref[...]) * SM_SCALE + bcd_ref[...]
        e = jnp.exp(s - jnp.max(s, axis=-1, keepdims=True))
        p = e / jnp.sum(e, axis=-1, keepdims=True)
        oc_s[...] = _dot_nt(p.astype(BF16), vct_ref[...].astype(BF16))
        scores = []
        for g in range(KV_A):
            pe = [p[(g * REP_A + r) * dt:(g * REP_A + r + 1) * dt, 0:nps] for r in range(REP_A)]
            po = [p[(g * REP_A + r) * dt:(g * REP_A + r + 1) * dt, nps:2 * nps] for r in range(REP_A)]
            scores.append((pe[0] + pe[1] + pe[2] + pe[3]) + (po[0] + po[1] + po[2] + po[3]))
        simp = jnp.concatenate(scores, axis=0)
        blk = lax.broadcasted_iota(jnp.int32, simp.shape, 1)
        score = jnp.where((blk == 0) | (blk >= nps - 1), FORCE_SCORE, simp)
        rank = _rank_select(score, blk, nps, 1)
        sel = jnp.where(rank < TOP_K - 1, 1.0, 0.0).astype(BF16)
        madd = (_dot(sel, e_ref[...]) - 1.0) * (-NEG)
        madd = jnp.concatenate([madd[0:dt]] * REP_A + [madd[dt:2 * dt]] * REP_A, axis=0)
        for j in range(nc):
            tile = madd[:, j * ck:(j + 1) * ck]
            if j == nc - 1:
                tile = jnp.concatenate([tile[:, :ck - PAGE], tile[:, ck - PAGE:] + blast_ref[...]], axis=1)
            madd_s[j] = tile
        m_s[...] = jnp.full(m_s.shape, NEG, F32)
        l_s[...] = jnp.zeros(l_s.shape, F32)
        acc_s[...] = jnp.zeros(acc_s.shape, F32)

    qbd = (qbd_s[...] * (SM_SCALE * LOG2E)).astype(BF16)
    kt = jnp.concatenate([r[...].astype(BF16) for r in sk_refs], axis=1)
    vt = jnp.concatenate([r[...].astype(BF16) for r in sv_refs], axis=1)
    m, alpha, l, p = _softmax_rows(qbd, kt, madd_s[c], m_s[...], l_s[...])
    acc = alpha * acc_s[...] + _dot_nt(p, vt)
    m_s[...], l_s[...], acc_s[...] = m, l, acc

    @pl.when(c == nc - 1)
    def _():
        row_t = lax.broadcasted_iota(jnp.int32, (rows, 128), 0) % dt
        col = lax.broadcasted_iota(jnp.int32, (rows, 128), 1)
        new_add = jnp.where((col <= row_t) & (col < dt), bnew_ref[...], NEG)
        row_w = lax.broadcasted_iota(jnp.int32, (rows, WINDOW), 0) % dt
        col_w = lax.broadcasted_iota(jnp.int32, (rows, WINDOW), 1)
        win_add = jnp.where(col_w > row_w, bwin_ref[...], NEG)
        pad = jnp.zeros((128 - dt, KV_COLS), F32)

        def new_tile(src):
            return jnp.concatenate([kv_ref[:, src * KV_COLS:(src + 1) * KV_COLS], pad], axis=0).astype(BF16)

        def attend_nat(kn, vn, add, m, l, acc):
            s = _dot_nt(qbd, kn) + add
            m_new = jnp.maximum(m, jnp.max(s, axis=-1, keepdims=True))
            alpha = jnp.exp2(m - m_new)
            p = jnp.exp2(s - m_new)
            return m_new, alpha * l + jnp.sum(p, axis=-1, keepdims=True), alpha * acc + _dot(p.astype(BF16), vn)

        m2, l2, acc2 = attend_nat(new_tile(2), new_tile(3), new_add, m, l, acc)
        os_full = acc2 / l2
        init = (jnp.full((rows, 1), NEG, F32), jnp.zeros((rows, 1), F32), jnp.zeros((rows, KV_COLS), F32))
        m_w, _, l_w, p_w = _softmax_rows(qbd, wink_ref[...].astype(BF16), win_add, init[0], init[1])
        carry = (m_w, l_w, _dot_nt(p_w, winv_ref[...].astype(BF16)))
        m3, l3, acc3 = attend_nat(new_tile(4), new_tile(5), new_add, *carry)
        ow_full = acc3 / l3

        def heads(full):
            parts = []
            for g in range(KV_A):
                for r in range(REP_A):
                    r0 = (g * REP_A + r) * dt
                    parts.append(full[r0:r0 + dt, g * HEAD_DIM:(g + 1) * HEAD_DIM])
            return jnp.concatenate(parts, axis=1)

        sg = _sigmoid(gate_ref[...])
        ge = _dot_exact_rhs(sg, e2_ref[...])
        o = ge[:, 0:W_A] * heads(oc_s[...]) + ge[:, W_A:2 * W_A] * heads(os_full) + ge[:, 2 * W_A:] * heads(ow_full)
        o_ref[...] = o * _silu(za_ref[...])
        for src, w_ref, wo_ref in ((4, wink_ref, wk_o), (5, winv_ref, wv_o)):
            new_t = jnp.concatenate([kv_ref[:, src * KV_COLS:(src + 1) * KV_COLS], pad], axis=0).T
            wo_ref[...] = jnp.concatenate([w_ref[:, dt:], new_t[:, 0:dt]], axis=1)


def _attn_decode(layer, page_table, proj, kct, vct, pool_sk, pool_sv, win_k, win_v, tabs, e_sel, e2, pages, dt):
    db, n_pages = page_table.shape
    nps = n_pages * (PAGE // SEL_BLOCK)
    nc = n_pages // pages
    ck = pages * PAGE
    rows = KV_A * REP_A * dt
    bcd, blast, bnew, bwin = tabs

    def row_spec(width, col):
        return pl.BlockSpec((dt, width), lambda b, c, pt: (b, col // width))

    def page_spec(p):
        return pl.BlockSpec((None, None, KV_COLS, PAGE), lambda b, c, pt: (layer, pt[b, c * pages + p], 0, 0))

    cmp_spec = pl.BlockSpec((None, None, KV_COLS, 2 * nps), lambda b, c, pt: (layer, b, 0, 0))
    win_spec = pl.BlockSpec((None, None, KV_COLS, WINDOW), lambda b, c, pt: (layer, b, 0, 0))

    def const_spec(a):
        return pl.BlockSpec(a.shape, lambda b, c, pt: (0,) * a.ndim)

    in_specs = ([row_spec(W_A, COL_Q), row_spec(W_A, COL_ZA), row_spec(6 * KV_COLS, COL_KV), row_spec(128, COL_GATE)]
                + [cmp_spec] * 2 + [page_spec(p) for p in range(pages)] * 2 + [win_spec] * 2
                + [const_spec(a) for a in (bcd, blast, bnew, bwin, e_sel, e2)])
    out_specs = [pl.BlockSpec((dt, W_A), lambda b, c, pt: (b, 0)),
                 pl.BlockSpec((None, KV_COLS, WINDOW), lambda b, c, pt: (b, 0, 0)),
                 pl.BlockSpec((None, KV_COLS, WINDOW), lambda b, c, pt: (b, 0, 0))]
    out_shape = [jax.ShapeDtypeStruct((db * dt, W_A), F32),
                 jax.ShapeDtypeStruct((db, KV_COLS, WINDOW), F32),
                 jax.ShapeDtypeStruct((db, KV_COLS, WINDOW), F32)]
    scratch = [pltpu.VMEM((rows, KV_COLS), F32), pltpu.VMEM((rows, 1), F32), pltpu.VMEM((rows, 1), F32),
               pltpu.VMEM((rows, KV_COLS), F32), pltpu.VMEM((nc, rows, ck), F32), pltpu.VMEM((rows, KV_COLS), F32)]
    return pl.pallas_call(
        functools.partial(_attn_decode_kernel, pages=pages, nps=nps, nc=nc, dt=dt),
        grid_spec=pltpu.PrefetchScalarGridSpec(
            num_scalar_prefetch=1, grid=(db, nc), in_specs=in_specs, out_specs=out_specs, scratch_shapes=scratch),
        out_shape=out_shape,
        compiler_params=_cparams(("arbitrary", "arbitrary")),
        name="attn_decode",
    )(page_table, proj, proj, proj, proj, kct, vct, *([pool_sk] * pages), *([pool_sv] * pages),
      win_k, win_v, bcd, blast, bnew, bwin, e_sel, e2)


EXT_B0 = 32
EXT_C0 = 8
CONV_ROWS = 32
SH_PAD = (CONV_B - 1) // 8 * 8


def _mixer_kernel(glu_ref, zb_ref, xc_ref, zc_ref, bufb_ref, bufc_ref, h0_ref,
                  cbw_ref, cbb_ref, gng_ref, gnb_ref, wpw_ref, gones_ref,
                  ccw_ref, ccb_ref, wa_ref, ba_ref, wx_ref, bx_ref, lam_ref,
                  bo_ref, co_ref, cbs_ref, ccs_ref, hs_ref,
                  extb, extc, hcar, cbuf, shb, *, tt):
    j = pl.program_id(1)
    nb, nc = CONV_B - 1, CONV_C - 1

    @pl.when(j == 0)
    def _():
        extb[EXT_B0 - nb:EXT_B0, :] = bufb_ref[...]
        extc[EXT_C0 - nc:EXT_C0, :] = bufc_ref[...]
        hcar[...] = h0_ref[...]

    glu = glu_ref[...]
    extb[EXT_B0:EXT_B0 + tt, :] = glu[:, 0:W_B] * _sigmoid(glu[:, W_B:2 * W_B])
    for r in range(8):
        n_r = tt + (CONV_B - 1 - r) // 8 * 8
        shb[r, 0:n_r, :] = extb[EXT_B0 - nb + r:EXT_B0 - nb + r + n_r, :]
    step = min(CONV_ROWS, tt)
    for r0 in range(0, tt, step):
        acc = jnp.zeros((step, W_B), F32)
        for k in range(CONV_B):
            lo = (k // 8) * 8 + r0
            acc = acc + cbw_ref[k:k + 1, :] * shb[k % 8, lo:lo + step, :]
        cbuf[r0:r0 + step, :] = acc + cbb_ref[...]
    cv = cbuf[...]
    gones = gones_ref[...]
    mu = _dot_exact_rhs(cv, gones) * (1.0 / GN_GROUP)
    d = cv - mu
    var = _dot_exact_rhs(d * d, gones) * (1.0 / GN_GROUP)
    cn = d * lax.rsqrt(var + EPS) * gng_ref[...] + gnb_ref[...]
    bo_ref[...] = _dot(_silu(cn).astype(BF16), wpw_ref[...]) * _silu(zb_ref[...])

    extc[EXT_C0:EXT_C0 + tt, :] = xc_ref[...]
    u = jnp.zeros((tt, W_C), F32)
    for k in range(CONV_C):
        lo = EXT_C0 - nc + k
        u = u + ccw_ref[k:k + 1, :] * extc[lo:lo + tt, :]
    u = u + ccb_ref[...]
    ub = u.astype(BF16)
    r = _sigmoid(_dot(ub, wa_ref[...]) + ba_ref[...])
    ig = _sigmoid(_dot(ub, wx_ref[...]) + bx_ref[...])
    nl = -lam_ref[...]
    softplus = jnp.maximum(nl, 0.0) + jnp.log1p(jnp.exp(-jnp.abs(nl)))
    log_a = -LRU_C * r * softplus
    a = jnp.exp(log_a)
    b = jnp.sqrt(-_expm1(2.0 * log_a)) * (ig * u)
    row = lax.broadcasted_iota(jnp.int32, (tt, W_C), 0)
    s = 1
    while s < tt:
        a_sh = jnp.where(row < s, 1.0, pltpu.roll(a, s, axis=0))
        b_sh = jnp.where(row < s, 0.0, pltpu.roll(b, s, axis=0))
        b = a * b_sh + b
        a = a * a_sh
        s *= 2
    hh = a * hcar[...] + b
    co_ref[...] = hh * _silu(zc_ref[...])
    hcar[...] = hh[tt - 1:tt, :]

    new_b = extb[EXT_B0 + tt - nb:EXT_B0 + tt, :]
    new_c = extc[EXT_C0 + tt - nc:EXT_C0 + tt, :]
    extb[EXT_B0 - nb:EXT_B0, :] = new_b
    extc[EXT_C0 - nc:EXT_C0, :] = new_c

    @pl.when(j == pl.num_programs(1) - 1)
    def _():
        cbs_ref[...] = new_b
        ccs_ref[...] = new_c
        hs_ref[...] = hh[tt - 1:tt, :]


def _mixers(proj, cols, bufb, bufc, h0, lw, gones, b, t, tt):
    nj = t // tt
    c_glu, c_zb, c_xc, c_zc = cols

    def row_spec(width, col):
        return pl.BlockSpec((tt, width), lambda bi, j: (bi * nj + j, col // width))

    def st_spec(n, w):
        return pl.BlockSpec((None, n, w), lambda bi, j: (bi, 0, 0))

    def const_spec(a):
        return pl.BlockSpec(a.shape, lambda bi, j: (0,) * a.ndim)

    consts = [lw["cbw"], lw["cbb"], lw["gng"], lw["gnb"], lw["wpw"], gones,
              lw["ccw"], lw["ccb"], lw["wa"], lw["ba"], lw["wx"], lw["bx"], lw["lam"]]
    in_specs = ([row_spec(2 * W_B, c_glu), row_spec(W_B, c_zb), row_spec(W_C, c_xc), row_spec(W_C, c_zc),
                 st_spec(CONV_B - 1, W_B), st_spec(CONV_C - 1, W_C), st_spec(1, W_C)]
                + [const_spec(a) for a in consts])
    out_specs = [pl.BlockSpec((tt, W_B), lambda bi, j: (bi * nj + j, 0)),
                 pl.BlockSpec((tt, W_C), lambda bi, j: (bi * nj + j, 0)),
                 st_spec(CONV_B - 1, W_B), st_spec(CONV_C - 1, W_C), st_spec(1, W_C)]
    out_shape = [jax.ShapeDtypeStruct((b * t, W_B), F32), jax.ShapeDtypeStruct((b * t, W_C), F32),
                 jax.ShapeDtypeStruct((b, CONV_B - 1, W_B), F32), jax.ShapeDtypeStruct((b, CONV_C - 1, W_C), F32),
                 jax.ShapeDtypeStruct((b, 1, W_C), F32)]
    scratch = [pltpu.VMEM((EXT_B0 + tt, W_B), F32), pltpu.VMEM((EXT_C0 + tt, W_C), F32),
               pltpu.VMEM((1, W_C), F32), pltpu.VMEM((tt, W_B), F32), pltpu.VMEM((8, tt + SH_PAD, W_B), F32)]
    return pl.pallas_call(
        functools.partial(_mixer_kernel, tt=tt),
        grid=(b, nj),
        in_specs=in_specs, out_specs=out_specs, out_shape=out_shape, scratch_shapes=scratch,
        compiler_params=_cparams(("arbitrary", "arbitrary")),
        name="mixers",
    )(proj, proj, proj, proj, bufb, bufc, h0, *consts)


def _block_diag(w):
    nblk, c, d = w.shape
    eye = jnp.eye(nblk, dtype=w.dtype)
    return (eye[:, None, :, None] * w[:, :, None, :]).reshape(nblk * c, nblk * d)


def _row_tile(m, cap, step=8):
    t = min(m, cap)
    while m % t:
        t -= step
    return t


def kernel(x_prompt, x_sample, cache_cmp_k, cache_cmp_v, cache_sel_k, cache_sel_v, cache_win_k, cache_win_v,
           state_conv_b, state_conv_c, state_rglru, page_table, rel_bias, g_pre, g_post, w_in, w_out,
           w_cmp_k, w_cmp_v, conv_b_w, conv_b_b, gn_gain, gn_bias, w_pw_b, conv_c_w, conv_c_b,
           w_lru_a, b_lru_a, w_lru_x, b_lru_x, lru_lambda):
    depth = w_in.shape[0]
    b, t, _ = x_prompt.shape
    db, dt, _ = x_sample.shape
    n_pages = page_table.shape[1]
    past = n_pages * PAGE
    nps = past // SEL_BLOCK
    ns, nt = t // SEL_BLOCK, t // TQ
    assert t % TQ == 0 and TQ == TK and WINDOW == 2 * TK and TOP_K <= ns <= 128 and TOP_K < nps <= 128
    assert dt <= 8 and (past + dt) // L_CMP == past // L_CMP and cache_win_k.shape[2] == WINDOW
    pages = min(64, n_pages)
    pages_c = min(16, n_pages)
    assert n_pages % pages == 0 and n_pages % pages_c == 0
    nc = n_pages // pages_c
    rows = KV_A * REP_A * dt

    w_it = jnp.swapaxes(w_in, 1, 2)
    w_r = jnp.concatenate(
        [w_it[:, 0:512], w_it[:, 1304:1816], w_it[:, 1816:2328], w_it[:, 512:1280], w_it[:, 2328:3096],
         w_it[:, 1280:1304], jnp.zeros((depth, D_IN_PAD - D_IN, D_MODEL), w_in.dtype)], axis=1).astype(BF16)
    w_n = w_it[:, 1816:3096].astype(BF16)
    w_t = jnp.concatenate(
        [w_it[:, 0:512], w_it[:, 1304:1816], w_it[:, 1280:1304],
         jnp.zeros((depth, N_TR - 1048, D_MODEL), w_in.dtype)], axis=1).astype(BF16)
    w_kv = w_it[:, 512:1280].astype(BF16)
    w_o = w_out.astype(BF16)
    wck = jnp.tile(w_cmp_k, (1, 1, KV_A))
    wvt_p = jnp.tile(jnp.swapaxes(w_cmp_v, 1, 2), (1, KV_A, t // L_CMP))
    wkt_d = jnp.tile(jnp.swapaxes(w_cmp_k, 1, 2), (1, KV_A, PAGE // L_CMP))
    wvt_d = jnp.tile(jnp.swapaxes(w_cmp_v, 1, 2), (1, KV_A, PAGE // L_CMP))
    gones = jnp.asarray(_group_ones_np(), BF16)
    e2 = jnp.asarray(_gate_expand_np(), BF16)
    e2t = jnp.asarray(_gate_expand_np().T, BF16)
    ett = jnp.asarray(_expand_np(ns, 128).T, BF16)
    e_d = jnp.asarray(_expand_np(nps, nps), BF16)
    seg_p = jnp.asarray(_segment_np(t, ns), BF16)
    seg_d = jnp.asarray(np.tile(_segment_np(pages_c * PAGE, pages_c * PAGE // SEL_BLOCK), (2, 1)), BF16)

    kk, qq = np.arange(TK)[:, None], np.arange(TQ)[None, :]
    bk_near = np.concatenate([_bucket_np(d0 + qq - kk) for d0 in (0, TQ)], axis=0)
    bnear = _bias_lookup(rel_bias, bk_near, shift=True, scale=LOG2E).reshape(H_A, 2, TK, TQ)
    bnear = jnp.where(jnp.asarray((kk > qq)[None, None] & (np.arange(2) == 0)[None, :, None, None]), NEG, bnear)
    blk_eo = np.concatenate([2 * np.arange(ns), 2 * np.arange(ns) + 1])[:, None]
    bct = _bias_lookup(rel_bias, _bucket_np(np.arange(t)[None, :] - (blk_eo * L_CMP + L_CMP - 1)))
    qd = past + np.arange(dt)[:, None]
    blk_d = np.concatenate([2 * np.arange(nps), 2 * np.arange(nps) + 1])[None, :]
    bcd = _bias_lookup(rel_bias, _bucket_np(qd - (blk_d * L_CMP + L_CMP - 1))).reshape(rows, 2 * nps)
    blast = _bias_lookup(rel_bias, _bucket_np(qd - (past - PAGE + np.arange(PAGE))[None, :]), shift=True, scale=LOG2E)
    bnew = _bias_lookup(rel_bias, _bucket_np(np.arange(dt)[:, None] - np.arange(128)[None, :]), shift=True, scale=LOG2E)
    bwin = _bias_lookup(rel_bias, _bucket_np(WINDOW + np.arange(dt)[:, None] - np.arange(WINDOW)[None, :]),
                        shift=True, scale=LOG2E)
    dtabs = (bcd, blast.reshape(rows, PAGE), bnew.reshape(rows, 128), bwin.reshape(rows, WINDOW))

    pool = lambda a: jnp.transpose(a, (0, 1, 3, 4, 2)).reshape(a.shape[0], a.shape[1], KV_COLS, PAGE)
    kc_ch, vc_ch = _decode_compress(page_table, pool(cache_cmp_k), pool(cache_cmp_v), wkt_d, wvt_d, seg_d, pages_c)

    def eo_table(x):
        x = x.reshape(depth, db, nc, KV_COLS, 2, nps // nc).transpose(0, 1, 3, 4, 2, 5)
        return x.reshape(depth, db, KV_COLS, 2 * nps)

    kct, vct = eo_table(kc_ch), eo_table(vc_ch)
    pool_sk, pool_sv = pool(cache_sel_k), pool(cache_sel_v)
    win_k = jnp.transpose(cache_win_k, (0, 1, 3, 4, 2)).reshape(depth, db, KV_COLS, WINDOW)
    win_v = jnp.transpose(cache_win_v, (0, 1, 3, 4, 2)).reshape(depth, db, KV_COLS, WINDOW)

    hp = x_prompt.reshape(b * t, D_MODEL)
    hs = x_sample.reshape(db * dt, D_MODEL)
    zeros_b = jnp.zeros((b, CONV_B - 1, W_B), F32)
    zeros_c = jnp.zeros((b, CONV_C - 1, W_C), F32)
    zeros_h = jnp.zeros((b, 1, W_C), F32)
    tm_p, tm_s = _row_tile(t, 512, 128), _row_tile(db * dt, 512)
    tt_p = _row_tile(t, 512)
    nat_cols = (NAT_GLU, NAT_ZB, NAT_XC, NAT_ZC)
    dec_cols = (COL_GLU, COL_ZB, COL_XC, COL_ZC)

    kv_states = [jnp.zeros((depth, b, KV_COLS, t), F32) for _ in range(N_KV)]
    p_states, s_states = [], []
    for l in range(depth):
        row = lambda a: a[l][None, :]
        lw = dict(cbw=conv_b_w[l], cbb=row(conv_b_b), gng=row(gn_gain), gnb=row(gn_bias), wpw=w_pw_b[l].astype(BF16),
                  ccw=conv_c_w[l], ccb=row(conv_c_b), wa=_block_diag(w_lru_a[l]).astype(BF16), ba=row(b_lru_a),
                  wx=_block_diag(w_lru_x[l]).astype(BF16), bx=row(b_lru_x), lam=row(lru_lambda))
        proj_n, proj_t, kv_states = _project_prompt(hp, row(g_pre), w_n[l], w_t[l], w_kv[l], kv_states, l, b, t, tm_p)
        a_out = _attn_prompt(l, proj_t, kv_states, wck[l], wvt_p[l], seg_p, bnear, bct, ett, e2t, b, t)
        b_out, c_out, cb, cc, hc = _mixers(proj_n, nat_cols, zeros_b, zeros_c, zeros_h, lw, gones, b, t, tt_p)
        hp = _output(hp, a_out, b_out, c_out, w_o[l], row(g_post), tm_p)
        p_states.append((cb, cc, hc[:, 0]))
        proj = _project(hs, row(g_pre), w_r[l], tm_s)
        a_out, wk_n, wv_n = _attn_decode(l, page_table, proj, kct, vct, pool_sk, pool_sv, win_k, win_v,
                                         dtabs, e_d, e2, pages, dt)
        b_out, c_out, cb, cc, hc = _mixers(proj, dec_cols, state_conv_b[l], state_conv_c[l],
                                           state_rglru[l][:, None, :], lw, gones, db, dt, dt)
        hs = _output(hs, a_out, b_out, c_out, w_o[l], row(g_post), tm_s)
        kv = proj[:, COL_KV:COL_KV + 6 * KV_COLS].reshape(db, dt, 6, KV_A, HEAD_DIM)
        s_states.append((kv[:, :, 0], kv[:, :, 1], kv[:, :, 2], kv[:, :, 3],
                         wk_n, wv_n, cb, cc, hc[:, 0]))

    cb_p, cc_p, h_p = [jnp.stack(a) for a in zip(*p_states)]
    kv_states = kv_states[:4] + [a[..., t - WINDOW:] for a in kv_states[4:]]
    ck_p, cv_p, sk_p, sv_p, wk_p, wv_p = [
        jnp.transpose(a.reshape(depth, b, KV_A, HEAD_DIM, a.shape[-1]), (0, 1, 4, 2, 3)) for a in kv_states]
    ck_s, cv_s, sk_s, sv_s, wk_s, wv_s, cb_s, cc_s, h_s = [jnp.stack(a) for a in zip(*s_states)]
    wk_s, wv_s = [jnp.transpose(a.reshape(depth, db, KV_A, HEAD_DIM, WINDOW), (0, 1, 4, 2, 3)) for a in (wk_s, wv_s)]
    return (hp.reshape(b, t, D_MODEL), hs.reshape(db, dt, D_MODEL),
            ck_p, ck_s, cv_p, cv_s, sk_p, sk_s, sv_p, sv_s, wk_p, wk_s, wv_p, wv_s,
            cb_p, cb_s, cc_p, cc_s, h_p, h_s)
```
